```python
import math
import jax, jax.numpy as jnp
from jax import lax
import numpy as np

D_MODEL = 1024
BATCH = 4
SEQ = 8192
DEPTH = 1

GRID_W = 64
CTX_LEN = 256
LN_EPS = 1e-6
DEEPNORM_ALPHA = (2 * DEPTH) ** 0.25
DEEPNORM_BETA = (8 * DEPTH) ** -0.25

GLA_HEADS = 4
GLA_DK = 64
GLA_DV = 128
GLA_KEY = GLA_HEADS * GLA_DK
GLA_VAL = GLA_HEADS * GLA_DV
GLA_RANK = 16
GLA_TAU = 16.0
GLA_CHUNK = 64

HY_WIDTH = D_MODEL - GLA_VAL
HY_ORDER = 2
HY_CONV = 3
HY_EMB = 33
HY_FH = 64
HY_TARGET = 1e-2
HY_FAST = 0.3
HY_SLOW = 1.5

MIX_WIDTH = GLA_VAL + HY_WIDTH
IN_SIZES = (GLA_KEY, GLA_KEY, GLA_VAL, GLA_VAL, GLA_RANK, GLA_RANK, (HY_ORDER + 1) * HY_WIDTH)
IN_WIDTH = sum(IN_SIZES)

N_EXPERTS = 32
TOP_K = 4
D_EXPERT = D_MODEL
SWIGLU_ALPHA = 1.702
SWIGLU_LIMIT = 7.0

kernel_name = 'hybrid_gla_hyena_moe_dit_block'


def _ln_stats(x):
    xf = x.astype(jnp.float32)
    mu = jnp.mean(xf, -1, keepdims=True)
    xc = xf - mu
    return xc * lax.rsqrt(jnp.mean(jnp.square(xc), -1, keepdims=True) + LN_EPS)


def modulate(x, shift, scale):
    return _ln_stats(x).astype(x.dtype) * (1 + scale) + shift


def post_norm(x, gain, bias):
    return (_ln_stats(x) * gain.astype(jnp.float32) + bias.astype(jnp.float32)).astype(x.dtype)


def mod_part(m, i):
    return m[..., i * D_MODEL:(i + 1) * D_MODEL]


def split_projection(u):
    return jnp.split(u, np.cumsum(IN_SIZES)[:-1].tolist(), axis=-1)


def to_heads(t, d):
    b, l, _ = t.shape
    return t.reshape(b, l, GLA_HEADS, d).transpose(0, 2, 1, 3)


def log_decay(a_low, wa, ba):
    z = a_low.astype(jnp.float32) @ wa.astype(jnp.float32) + ba.astype(jnp.float32)
    return to_heads(jax.nn.log_sigmoid(z) / GLA_TAU, GLA_DK)


def gla_chunked(q, k, v, log_a, s0):
    b, h, l, _ = q.shape
    n = l // GLA_CHUNK

    def chunks(t):
        return jnp.moveaxis(t.astype(jnp.float32).reshape(b, h, n, GLA_CHUNK, t.shape[-1]), 2, 0)

    lower = jnp.tril(jnp.ones((GLA_CHUNK, GLA_CHUNK), dtype=bool))[None, None, :, :, None]

    def step(s, inp):
        qc, kc, vc, gc = inp
        cum = jnp.cumsum(gc, axis=2)
        last = cum[:, :, -1:, :]
        o = jnp.einsum('bhid,bhdv->bhiv', qc * jnp.exp(cum), s)
        decay = jnp.exp(jnp.where(lower, cum[:, :, :, None, :] - cum[:, :, None, :, :], -jnp.inf))
        scores = jnp.einsum('bhid,bhjd,bhijd->bhij', qc, kc, decay)
        o = o + jnp.einsum('bhij,bhjv->bhiv', scores, vc)
        s = jnp.exp(last[:, :, 0, :])[..., None] * s + jnp.einsum('bhjd,bhjv->bhdv', kc * jnp.exp(last - cum), vc)
        return s, o

    s_fin, o = lax.scan(step, s0, (chunks(q), chunks(k), chunks(v), chunks(log_a)))
    return jnp.moveaxis(o, 0, 2).reshape(b, h, l, GLA_DV), s_fin


def gla_bidirectional(q, k, v, la_f, la_b, s0_f, s0_b):
    o_f, s_f = gla_chunked(q, k, v, la_f, s0_f)
    flip = lambda t: jnp.flip(t, axis=2)
    o_b, s_b = gla_chunked(flip(q), flip(k), flip(v), flip(la_b), s0_b)
    return o_f + flip(o_b), s_f, s_b


def gla_heads(q, k, v):
    return to_heads(q, GLA_DK) * (GLA_DK ** -0.5), to_heads(k, GLA_DK), to_heads(v, GLA_DV)


def gla_output(o, g, norm_g):
    o = o * lax.rsqrt(jnp.mean(jnp.square(o), -1, keepdims=True) + LN_EPS) * norm_g.astype(jnp.float32)
    b, _, l, _ = o.shape
    o = o.transpose(0, 2, 1, 3).reshape(b, l, GLA_VAL).astype(g.dtype)
    return o * jax.nn.silu(g)


def short_conv_rows(u, w, bias, rows):
    b, l, ch = u.shape
    row_len = l // rows
    up = jnp.pad(u.reshape(b, rows, row_len, ch), ((0, 0), (0, 0), (1, 1), (0, 0)))
    y = sum(up[:, :, i:i + row_len] * w[i] for i in range(HY_CONV)) + bias
    return y.reshape(b, l, ch)


def hyena_filters(length, w1, b1, w2, b2, w_out, freq):
    f32 = jnp.float32
    t = jnp.arange(length, dtype=f32)
    t_norm = t / (length - 1)
    bands = (HY_EMB - 1) // 2
    f = jnp.linspace(1e-4, bands - 1, bands, dtype=f32)
    ang = (2.0 * math.pi * t / length)[:, None] * f[None, :]
    z = jnp.concatenate([t_norm[:, None], jnp.cos(ang), -jnp.sin(ang)], -1)
    fr = freq.astype(f32)
    hid = jnp.sin(fr * (z @ w1.astype(f32) + b1.astype(f32)))
    hid = jnp.sin(fr * (hid @ w2.astype(f32) + b2.astype(f32)))
    h = (hid @ w_out.astype(f32)).reshape(length, HY_ORDER, 2, HY_WIDTH)
    deltas = jnp.linspace(math.log(HY_TARGET) / HY_SLOW, math.log(HY_TARGET) / HY_FAST, HY_WIDTH, dtype=f32)
    window = jnp.exp(-t_norm[:, None] * jnp.abs(deltas)[None, :])
    return h * window[:, None, None, :]


def long_conv_bidir(u, h_fwd, h_bwd, d_bias):
    l = u.shape[1]
    two_sided = jnp.concatenate([h_fwd, jnp.zeros_like(h_fwd[:1]), jnp.flip(h_bwd[1:], 0)], 0)
    uf = u.astype(jnp.float32)
    spec = jnp.fft.rfft(uf, n=2 * l, axis=1) * jnp.fft.rfft(two_sided, axis=0)[None]
    y = jnp.fft.irfft(spec, n=2 * l, axis=1)[:, :l]
    return (y + uf * d_bias.astype(jnp.float32)).astype(u.dtype)


def hyena(u, filters, conv_w, conv_b, d_bias, rows):
    u = short_conv_rows(u, conv_w, conv_b, rows)
    v, x1, x2 = jnp.split(u, HY_ORDER + 1, axis=-1)
    z = x1 * long_conv_bidir(v, filters[:, 0, 0], filters[:, 0, 1], d_bias[0])
    return x2 * long_conv_bidir(z, filters[:, 1, 0], filters[:, 1, 1], d_bias[1])


def clamped_swiglu(hid):
    glu, lin = hid[..., ::2], hid[..., 1::2]
    glu = jnp.minimum(glu, SWIGLU_LIMIT)
    lin = jnp.clip(lin, -SWIGLU_LIMIT, SWIGLU_LIMIT)
    return glu * jax.nn.sigmoid(SWIGLU_ALPHA * glu) * (lin + 1)


def moe(h, router_w, router_b, w1, b1, w2, b2):
    b, l, d = h.shape
    t = h.reshape(b * l, d)
    logits = (t @ router_w + router_b).astype(jnp.float32)
    top_v, top_i = lax.top_k(logits, TOP_K)
    top_w = jax.nn.softmax(top_v, axis=-1)
    gates = jnp.sum(jax.nn.one_hot(top_i, N_EXPERTS, dtype=jnp.float32) * top_w[..., None], axis=1)
    out = jnp.zeros((b * l, d), jnp.float32)
    for e in range(N_EXPERTS):
        y_e = clamped_swiglu(t @ w1[e] + b1[e]) @ w2[e] + b2[e]
        out = out + gates[:, e:e + 1] * y_e.astype(jnp.float32)
    return out.astype(h.dtype).reshape(b, l, d)


def setup_inputs(seed: int = 0) -> dict:
    key = jax.random.key(seed)
    ks = iter(jax.random.split(key, 40))
    nrm = lambda shape, std: std * jax.random.normal(next(ks), shape, jnp.float32)
    D = D_MODEL
    return {
        'x': nrm((BATCH, SEQ, D), 1.0),
        'c': nrm((BATCH, D), 1.0),
        'ctx': nrm((BATCH, CTX_LEN, D), 1.0),
        'c_ctx': nrm((D,), 1.0),
        'ada_w': nrm((DEPTH, D, 6 * D), 0.5 * D ** -0.5),
        'ada_b': nrm((DEPTH, 6 * D), 0.01),
        'w_in': nrm((DEPTH, D, IN_WIDTH), D ** -0.5),
        'gla_wa_f': nrm((DEPTH, GLA_RANK, GLA_KEY), GLA_RANK ** -0.5),
        'gla_ba_f': nrm((DEPTH, GLA_KEY), 0.1),
        'gla_wa_b': nrm((DEPTH, GLA_RANK, GLA_KEY), GLA_RANK ** -0.5),
        'gla_ba_b': nrm((DEPTH, GLA_KEY), 0.1),
        'gla_norm_g': 1.0 + nrm((DEPTH, GLA_DV), 0.01),
        'hy_conv_w': nrm((DEPTH, HY_CONV, (HY_ORDER + 1) * HY_WIDTH), HY_CONV ** -0.5),
        'hy_conv_b': nrm((DEPTH, (HY_ORDER + 1) * HY_WIDTH), 0.01),
        'hy_flt_w1': nrm((DEPTH, HY_EMB, HY_FH), HY_EMB ** -0.5),
        'hy_flt_b1': nrm((DEPTH, HY_FH), 0.1),
        'hy_flt_w2': nrm((DEPTH, HY_FH, HY_FH), HY_FH ** -0.5),
        'hy_flt_b2': nrm((DEPTH, HY_FH), 0.1),
        'hy_flt_wout': nrm((DEPTH, HY_FH, HY_ORDER * 2 * HY_WIDTH), 0.05 * HY_FH ** -0.5),
        'hy_flt_freq': 1.0 + nrm((DEPTH, HY_FH), 0.01),
        'hy_bias_d': nrm((DEPTH, HY_ORDER, HY_WIDTH), 1.0),
        'w_out': nrm((DEPTH, MIX_WIDTH, D), DEEPNORM_BETA * MIX_WIDTH ** -0.5),
        'ln1_g': 1.0 + nrm((DEPTH, D), 0.01),
        'ln1_b': nrm((DEPTH, D), 0.01),
        'router_w': nrm((DEPTH, D, N_EXPERTS), D ** -0.5),
        'router_b': nrm((DEPTH, N_EXPERTS), 0.01),
        'exp_w1': nrm((DEPTH, N_EXPERTS, D, 2 * D_EXPERT), D ** -0.5),
        'exp_b1': nrm((DEPTH, N_EXPERTS, 2 * D_EXPERT), 0.01),
        'exp_w2': nrm((DEPTH, N_EXPERTS, D_EXPERT, D), DEEPNORM_BETA * D_EXPERT ** -0.5),
        'exp_b2': nrm((DEPTH, N_EXPERTS, D), 0.01),
        'ln2_g': 1.0 + nrm((DEPTH, D), 0.01),
        'ln2_b': nrm((DEPTH, D), 0.01),
    }


def reference(x, c, ctx, c_ctx, ada_w, ada_b, w_in, gla_wa_f, gla_ba_f, gla_wa_b, gla_ba_b, gla_norm_g,
              hy_conv_w, hy_conv_b, hy_flt_w1, hy_flt_b1, hy_flt_w2, hy_flt_b2, hy_flt_wout, hy_flt_freq,
              hy_bias_d, w_out, ln1_g, ln1_b, router_w, router_b, exp_w1, exp_b1, exp_w2, exp_b2, ln2_g, ln2_b):
    batch, seq_len, _ = x.shape
    ctx_len = ctx.shape[1]
    rows = seq_len // GRID_W
    for l in range(DEPTH):
        update_ctx = l < DEPTH - 1
        mod_x = (jax.nn.silu(c) @ ada_w[l] + ada_b[l])[:, None, :]
        mod_c = (jax.nn.silu(c_ctx) @ ada_w[l] + ada_b[l])[None, None, :]
        flt = functools_free = (hy_flt_w1[l], hy_flt_b1[l], hy_flt_w2[l], hy_flt_b2[l], hy_flt_wout[l], hy_flt_freq[l])

        h = modulate(x, mod_part(mod_x, 0), mod_part(mod_x, 1))
        hc = modulate(ctx, mod_part(mod_c, 0), mod_part(mod_c, 1))
        q, k, v, g, a_f, a_b, hy_u = split_projection(h @ w_in[l])
        qc, kc, vc, gc, a_fc, a_bc, hy_uc = split_projection(hc @ w_in[l])

        zero = jnp.zeros((batch, GLA_HEADS, GLA_DK, GLA_DV), jnp.float32)
        oc, s_f, s_b = gla_bidirectional(*gla_heads(qc, kc, vc), log_decay(a_fc, gla_wa_f[l], gla_ba_f[l]),
                                         log_decay(a_bc, gla_wa_b[l], gla_ba_b[l]), zero, zero)
        o, _, _ = gla_bidirectional(*gla_heads(q, k, v), log_decay(a_f, gla_wa_f[l], gla_ba_f[l]),
                                    log_decay(a_b, gla_wa_b[l], gla_ba_b[l]), s_f, s_b)
        y_gla = gla_output(o, g, gla_norm_g[l])

        y_hy = hyena(hy_u, hyena_filters(seq_len, *flt), hy_conv_w[l], hy_conv_b[l], hy_bias_d[l], rows)
        f = jnp.concatenate([y_gla, y_hy], axis=-1) @ w_out[l]
        x_mid = post_norm(DEEPNORM_ALPHA * x + mod_part(mod_x, 2) * f, ln1_g[l], ln1_b[l])

        h2 = modulate(x_mid, mod_part(mod_x, 3), mod_part(mod_x, 4))
        y_moe = moe(h2, router_w[l], router_b[l], exp_w1[l], exp_b1[l], exp_w2[l], exp_b2[l])
        x_new = post_norm(DEEPNORM_ALPHA * x_mid + mod_part(mod_x, 5) * y_moe, ln2_g[l], ln2_b[l])

        if update_ctx:
            yc_hy = hyena(hy_uc, hyena_filters(ctx_len, *flt), hy_conv_w[l], hy_conv_b[l], hy_bias_d[l], 1)
            fc = jnp.concatenate([gla_output(oc, gc, gla_norm_g[l]), yc_hy], axis=-1) @ w_out[l]
            ctx_mid = post_norm(DEEPNORM_ALPHA * ctx + mod_part(mod_c, 2) * fc, ln1_g[l], ln1_b[l])
            hc2 = modulate(ctx_mid, mod_part(mod_c, 3), mod_part(mod_c, 4))
            yc_moe = moe(hc2, router_w[l], router_b[l], exp_w1[l], exp_b1[l], exp_w2[l], exp_b2[l])
            ctx = post_norm(DEEPNORM_ALPHA * ctx_mid + mod_part(mod_c, 5) * yc_moe, ln2_g[l], ln2_b[l])
        x = x_new
    return x
```

```python
import functools
import math

import numpy as np
import jax
import jax.numpy as jnp
from jax import lax
from jax.experimental import pallas as pl
from jax.experimental.pallas import tpu as pltpu

F32 = jnp.float32
BF16 = jnp.bfloat16

D_MODEL = 1024
DEPTH = 1
GRID_W = 64
LN_EPS = 1e-6
DEEPNORM_ALPHA = (2 * DEPTH) ** 0.25
GLA_HEADS = 4
GLA_DK = 64
GLA_DV = 128
GLA_KEY = GLA_HEADS * GLA_DK
GLA_VAL = GLA_HEADS * GLA_DV
GLA_RANK = 16
GLA_TAU = 16.0
HY_WIDTH = D_MODEL - GLA_VAL
HY_ORDER = 2
HY_CONV = 3
HY_EMB = 33
HY_FH = 64
HY_TARGET = 1e-2
HY_FAST = 0.3
HY_SLOW = 1.5
N_EXPERTS = 32
TOP_K = 4
D_EXPERT = D_MODEL
SWIGLU_ALPHA = 1.702
SWIGLU_LIMIT = 7.0

LANES = 128
VMEM_LIMIT = 56 * 1024 * 1024

TOK_TILE = 256
GLA_CHUNK = 64
GLA_BLOCK = 256
A_PAD = LANES
HY_GROUP = 2
HY_CH_BLOCK = 8
FILT_TILE = 512
MOE_TOK_TILE = 1024
NEG_BIG = -1e30


def _cparams(sem):
    return pltpu.CompilerParams(dimension_semantics=sem, vmem_limit_bytes=VMEM_LIMIT)


def _layer_norm(x):
    mu = jnp.mean(x, axis=-1, keepdims=True)
    xc = x - mu
    return xc * lax.rsqrt(jnp.mean(xc * xc, axis=-1, keepdims=True) + LN_EPS)


def _dot(a, b):
    return jnp.dot(a, b, preferred_element_type=F32)


def _dot_nt(a, b):
    return lax.dot_general(a, b, (((1,), (1,)), ((), ())), preferred_element_type=F32)


def _dot_tn(a, b):
    return lax.dot_general(a, b, (((0,), (0,)), ((), ())), preferred_element_type=F32)


def _mod_kernel(c_ref, w_ref, b_ref, o_ref):
    c = c_ref[...]
    s = c * jax.nn.sigmoid(c)
    o_ref[...] = _dot(s.astype(BF16), w_ref[...].astype(BF16)) + b_ref[...]


def _modulation(cvec, ada_w, ada_b):
    rows, d = cvec.shape
    n = ada_w.shape[1]
    tn = 1024
    return pl.pallas_call(
        _mod_kernel,
        out_shape=jax.ShapeDtypeStruct((rows, n), F32),
        grid=(n // tn,),
        in_specs=[
            pl.BlockSpec((rows, d), lambda j: (0, 0)),
            pl.BlockSpec((d, tn), lambda j: (0, j)),
            pl.BlockSpec((1, tn), lambda j: (0, j)),
        ],
        out_specs=pl.BlockSpec((rows, tn), lambda j: (0, j)),
        compiler_params=_cparams(("arbitrary",)),
        name="adaln_mod",
    )(cvec, ada_w, ada_b)


def _inproj_kernel(x_ref, sh_ref, sc_ref, wm_ref, *rest, with_hy):
    if with_hy:
        wh_ref, cw_ref, cb_ref, q_ref, k_ref, v_ref, g_ref, a_ref, u_ref = rest
    else:
        q_ref, k_ref, v_ref, a_ref = rest
    h = _layer_norm(x_ref[0]) * (1.0 + sc_ref[0]) + sh_ref[0]
    hb = h.astype(BF16)
    u = _dot(hb, wm_ref[...])
    q_ref[0] = u[:, :GLA_KEY].astype(BF16)
    k_ref[0] = u[:, GLA_KEY:2 * GLA_KEY].astype(BF16)
    v_ref[0] = u[:, 2 * GLA_KEY:2 * GLA_KEY + GLA_VAL].astype(BF16)
    off = 2 * GLA_KEY + GLA_VAL
    if with_hy:
        g_ref[0] = u[:, off:off + GLA_VAL].astype(BF16)
        off += GLA_VAL
    a_ref[0] = u[:, off:off + A_PAD]
    if with_hy:
        ut = _dot_nt(wh_ref[...], hb)
        lane = lax.broadcasted_iota(jnp.int32, (1, LANES), 1) % GRID_W
        not_first = (lane != 0).astype(F32)
        not_last = (lane != GRID_W - 1).astype(F32)
        tt = ut.shape[1]
        for j in range(tt // LANES):
            c = ut[:, j * LANES:(j + 1) * LANES]
            left = pltpu.roll(c, 1, axis=1) * not_first
            right = pltpu.roll(c, LANES - 1, axis=1) * not_last
            y = cw_ref[0] * left + cw_ref[1] * c + cw_ref[2] * right + cb_ref[...]
            u_ref[0, :, j * LANES:(j + 1) * LANES] = y.astype(BF16)


def _input_projection(x, shift, scale, w_main, w_hy_t=None, conv_w=None, conv_b=None):
    b, l, d = x.shape
    tt = TOK_TILE
    with_hy = w_hy_t is not None
    nm = w_main.shape[1]
    tok = lambda width, dt: jax.ShapeDtypeStruct((b, l, width), dt)
    tok_spec = lambda width: pl.BlockSpec((1, tt, width), lambda i, j: (i, j, 0))
    in_specs = [
        pl.BlockSpec((1, tt, d), lambda i, j: (i, j, 0)),
        pl.BlockSpec((1, 1, d), lambda i, j: (i, 0, 0)),
        pl.BlockSpec((1, 1, d), lambda i, j: (i, 0, 0)),
        pl.BlockSpec((d, nm), lambda i, j: (0, 0)),
    ]
    args = [x, shift, scale, w_main]
    if with_hy:
        ch = w_hy_t.shape[0]
        in_specs += [
            pl.BlockSpec((ch, d), lambda i, j: (0, 0)),
            pl.BlockSpec((HY_CONV, ch, LANES), lambda i, j: (0, 0, 0)),
            pl.BlockSpec((ch, LANES), lambda i, j: (0, 0)),
        ]
        args += [w_hy_t, conv_w, conv_b]
        out_shape = [tok(GLA_KEY, BF16), tok(GLA_KEY, BF16), tok(GLA_VAL, BF16), tok(GLA_VAL, BF16),
                     tok(A_PAD, F32), jax.ShapeDtypeStruct((b, ch, l), BF16)]
        out_specs = [tok_spec(GLA_KEY), tok_spec(GLA_KEY), tok_spec(GLA_VAL), tok_spec(GLA_VAL),
                     tok_spec(A_PAD), pl.BlockSpec((1, ch, tt), lambda i, j: (i, 0, j))]
    else:
        out_shape = [tok(GLA_KEY, BF16), tok(GLA_KEY, BF16), tok(GLA_VAL, BF16), tok(A_PAD, F32)]
        out_specs = [tok_spec(GLA_KEY), tok_spec(GLA_KEY), tok_spec(GLA_VAL), tok_spec(A_PAD)]
    return pl.pallas_call(
        functools.partial(_inproj_kernel, with_hy=with_hy),
        out_shape=out_shape,
        grid=(b, l // tt),
        in_specs=in_specs,
        out_specs=out_specs,
        compiler_params=_cparams(("parallel", "arbitrary")),
        name="in_proj_hy" if with_hy else "in_proj_ctx",
    )(*args)


def _gla_kernel(qc_ref, kc_ref, vc_ref, ac_ref, ql_ref, kl_ref, vl_ref, al_ref,
                wa_ref, ba_ref, tri_ref, cm_ref, hm_ref, bd_ref, o_ref, st_ref,
                *, n_ctx_blocks, chunk):
    d = pl.program_id(1)
    s = pl.program_id(2)
    blk = ql_ref.shape[1]
    n_chunks = blk // chunk

    @pl.when(s == 0)
    def _():
        st_ref[...] = jnp.zeros_like(st_ref)

    is_ctx = s < n_ctx_blocks
    tri = tri_ref[0]
    cmask = cm_ref[0]
    bdmask = bd_ref[...]
    for n in range(n_chunks):
        ci = n + d * (n_chunks - 1 - 2 * n)
        r0 = pl.multiple_of(ci * chunk, chunk)
        rows = pl.ds(r0, chunk)
        q = jnp.where(is_ctx, qc_ref[0, rows, :], ql_ref[0, rows, :]).astype(F32)
        k = jnp.where(is_ctx, kc_ref[0, rows, :], kl_ref[0, rows, :]).astype(F32)
        v = jnp.where(is_ctx, vc_ref[0, rows, :], vl_ref[0, rows, :])
        a = jnp.where(is_ctx, ac_ref[0, rows, :], al_ref[0, rows, :])
        z = _dot(a.astype(BF16), wa_ref[0]) + ba_ref[0]
        g = (jnp.minimum(z, 0.0) - jnp.log(1.0 + jnp.exp(-jnp.abs(z)))) * (1.0 / GLA_TAU)
        g_hi = g.astype(BF16)
        g_lo = (g - g_hi.astype(F32)).astype(BF16)
        cum = _dot(tri, g_hi) + _dot(tri, g_lo)
        total = jnp.sum(g, axis=0, keepdims=True)
        qs = (q * jnp.exp(cum)).astype(BF16)
        ks = (k * jnp.exp(-cum)).astype(BF16)
        kst = (k * jnp.exp(total - cum)).astype(BF16)
        stb = st_ref[...].astype(BF16)
        o = _dot_nt(qs, stb)
        parts = []
        for h in range(GLA_HEADS):
            a_h = _dot_nt(qs * hm_ref[h], ks) * cmask
            parts.append(_dot(a_h.astype(BF16), v[:, h * GLA_DV:(h + 1) * GLA_DV]))
        o = o + jnp.concatenate(parts, axis=1)
        o_ref[0, 0, rows, :] = o.astype(o_ref.dtype)
        ut = _dot_tn(v, kst)
        st_ref[...] = st_ref[...] * jnp.exp(total) + ut * bdmask


def _gla_scan(ctx_qkva, lat_qkva, wa, ba, chunk=GLA_CHUNK, blk=GLA_BLOCK):
    qc, kc, vc, ac = ctx_qkva
    ql, kl, vl, al = lat_qkva
    b, l, _ = ql.shape
    ctx_len = qc.shape[1]
    n_ctx = ctx_len // blk
    n_lat = l // blk
    c = chunk
    idx = np.arange(c)
    tri = np.stack([idx[:, None] >= idx[None, :], idx[:, None] <= idx[None, :]]).astype(np.float32)
    hmask = np.zeros((GLA_HEADS, 1, GLA_KEY), np.float32)
    for h in range(GLA_HEADS):
        hmask[h, 0, h * GLA_DK:(h + 1) * GLA_DK] = 1.0
    bd = (np.arange(GLA_VAL)[:, None] // GLA_DV == np.arange(GLA_KEY)[None, :] // GLA_DK).astype(np.float32)

    def lat_block(i, d, s):
        t = jnp.maximum(s - n_ctx, 0)
        return t + d * (n_lat - 1 - 2 * t)

    def ctx_block(i, d, s):
        t = jnp.minimum(s, n_ctx - 1)
        return t + d * (n_ctx - 1 - 2 * t)

    ctx_spec = lambda w: pl.BlockSpec((1, blk, w), lambda i, d, s: (i, ctx_block(i, d, s), 0))
    lat_spec = lambda w: pl.BlockSpec((1, blk, w), lambda i, d, s: (i, lat_block(i, d, s), 0))
    return pl.pallas_call(
        functools.partial(_gla_kernel, n_ctx_blocks=n_ctx, chunk=c),
        out_shape=jax.ShapeDtypeStruct((2, b, l, GLA_VAL), BF16),
        grid=(b, 2, n_ctx + n_lat),
        in_specs=[
            ctx_spec(GLA_KEY), ctx_spec(GLA_KEY), ctx_spec(GLA_VAL), ctx_spec(A_PAD),
            lat_spec(GLA_KEY), lat_spec(GLA_KEY), lat_spec(GLA_VAL), lat_spec(A_PAD),
            pl.BlockSpec((1, A_PAD, GLA_KEY), lambda i, d, s: (d, 0, 0)),
            pl.BlockSpec((1, 1, GLA_KEY), lambda i, d, s: (d, 0, 0)),
            pl.BlockSpec((1, c, c), lambda i, d, s: (d, 0, 0)),
            pl.BlockSpec((1, c, c), lambda i, d, s: (d, 0, 0)),
            pl.BlockSpec((GLA_HEADS, 1, GLA_KEY), lambda i, d, s: (0, 0, 0)),
            pl.BlockSpec((GLA_VAL, GLA_KEY), lambda i, d, s: (0, 0)),
        ],
        out_specs=pl.BlockSpec((1, 1, blk, GLA_VAL), lambda i, d, s: (d, i, lat_block(i, d, s), 0)),
        scratch_shapes=[pltpu.VMEM((GLA_VAL, GLA_KEY), F32)],
        compiler_params=_cparams(("parallel", "arbitrary", "arbitrary")),
        name="gla_scan",
    )(qc, kc, vc, ac, ql, kl, vl, al, wa, ba,
      jnp.asarray(tri, BF16), jnp.asarray(tri, F32), jnp.asarray(hmask, F32), jnp.asarray(bd, F32))


def _dft_constants(r1):
    n = r1 * LANES
    h = r1 // 2
    k1 = np.arange(r1)
    f1 = np.exp(-2j * np.pi * np.outer(k1, k1) / r1)
    f2 = np.exp(-2j * np.pi * np.outer(np.arange(LANES), np.arange(LANES)) / LANES)
    tw = np.exp(-2j * np.pi * np.outer(k1, np.arange(LANES)) / n)
    fa_c = np.block([[f1.real[:, :h], -f1.imag[:, :h]], [f1.imag[:, :h], f1.real[:, :h]]])
    fa_r = np.concatenate([f1.real, f1.imag], axis=0)
    gc = np.block([[f2.real, f2.imag], [-f2.imag, f2.real]])
    gci = np.block([[f2.real, -f2.imag], [f2.imag, f2.real]])
    fai = np.block([[f1.real[:h], f1.imag[:h]], [-f1.imag[:h], f1.real[:h]]]) / n
    tw_lane = np.tile(tw, (1, HY_GROUP))
    tw_row = np.tile(tw, (HY_GROUP, 1))
    f = lambda a: jnp.asarray(a, F32)
    return dict(fa_c=f(fa_c), fa_r=f(fa_r), gc=f(gc), gci=f(gci), fai=f(fai),
                twl_r=f(tw_lane.real), twl_i=f(tw_lane.imag), twr_r=f(tw_row.real), twr_i=f(tw_row.imag))


def _fwd_dft(rhs, fa, gc, twl_r, twl_i, r1):
    a = _dot(fa, rhs)
    ar, ai = a[:r1], a[r1:]
    br = (ar * twl_r - ai * twl_i).astype(BF16)
    bi = (ar * twl_i + ai * twl_r).astype(BF16)
    lhs = jnp.concatenate(
        [jnp.concatenate([br[:, c * LANES:(c + 1) * LANES], bi[:, c * LANES:(c + 1) * LANES]], axis=1)
         for c in range(HY_GROUP)], axis=0)
    return _dot(lhs, gc)


def _inv_dft(yr, yi, gci, fai, twr_r, twr_i, r1):
    lhs = jnp.concatenate([yr, yi], axis=1).astype(BF16)
    c = _dot(lhs, gci)
    cr, ci = c[:, :LANES], c[:, LANES:]
    dr = (cr * twr_r + ci * twr_i).astype(BF16)
    di = (ci * twr_r - cr * twr_i).astype(BF16)
    rhs = jnp.concatenate(
        [jnp.concatenate([dr[g * r1:(g + 1) * r1], di[g * r1:(g + 1) * r1]], axis=0)
         for g in range(HY_GROUP)], axis=1)
    return _dot(fai, rhs)


def _filter_mlp_kernel(zz_ref, win_ref, w1_ref, b1_ref, w2_ref, b2_ref, fr_ref, wo_ref, o_ref):
    hp = lax.Precision.HIGHEST
    fr = fr_ref[...]
    hid = jnp.sin(fr * (jnp.dot(zz_ref[...], w1_ref[...], precision=hp, preferred_element_type=F32) + b1_ref[...]))
    hid = jnp.sin(fr * (jnp.dot(hid, w2_ref[...], precision=hp, preferred_element_type=F32) + b2_ref[...]))
    for o in range(HY_ORDER):
        ht = lax.dot_general(wo_ref[o, 0], hid, (((1,), (1,)), ((), ())), precision=hp,
                             preferred_element_type=F32)
        o_ref[o] = ht * win_ref[...]


def _filter_mlp(zz, win_t, w1p, b1, w2, b2, freq, wo_t):
    n2l = zz.shape[0]
    l = n2l // 2
    lt = min(FILT_TILE, l)
    nt = l // lt
    ch = win_t.shape[0]
    full = lambda shape: pl.BlockSpec(shape, lambda d, j: (0,) * len(shape))
    return pl.pallas_call(
        _filter_mlp_kernel,
        out_shape=jax.ShapeDtypeStruct((HY_ORDER, ch, n2l), F32),
        grid=(2, nt),
        in_specs=[
            pl.BlockSpec((lt, LANES), lambda d, j: (d * nt + j, 0)),
            pl.BlockSpec((ch, lt), lambda d, j: (0, d * nt + j)),
            full((LANES, HY_FH)), full((1, HY_FH)), full((HY_FH, HY_FH)), full((1, HY_FH)), full((1, HY_FH)),
            pl.BlockSpec((HY_ORDER, 1, ch, HY_FH), lambda d, j: (0, d, 0, 0)),
        ],
        out_specs=pl.BlockSpec((HY_ORDER, ch, lt), lambda d, j: (0, 0, d * nt + j)),
        compiler_params=_cparams(("arbitrary", "arbitrary")),
        name="hyena_filter_mlp",
    )(zz, win_t, w1p, b1, w2, b2, freq, wo_t)


def _filter_fft_kernel(f_ref, fa_ref, gc_ref, twl_r_ref, twl_i_ref, h_ref, *, r1):
    fa = fa_ref[...].astype(BF16)
    gc = gc_ref[...].astype(BF16)
    nc = f_ref.shape[1]
    for g0 in range(0, nc, HY_GROUP):
        rhs = jnp.concatenate([f_ref[0, g0 + c].astype(BF16) for c in range(HY_GROUP)], axis=1)
        x = _fwd_dft(rhs, fa, gc, twl_r_ref[...], twl_i_ref[...], r1)
        for c in range(HY_GROUP):
            h_ref[0, g0 + c] = x[c * r1:(c + 1) * r1]


def _filter_spectra(filt, consts, r1):
    order, ch = filt.shape[:2]
    nc = HY_CH_BLOCK
    full = lambda a: pl.BlockSpec(a.shape, lambda o, j: (0,) * a.ndim)
    cs = [consts["fa_r"], consts["gc"], consts["twl_r"], consts["twl_i"]]
    return pl.pallas_call(
        functools.partial(_filter_fft_kernel, r1=r1),
        out_shape=jax.ShapeDtypeStruct((order, ch, r1, 2 * LANES), F32),
        grid=(order, ch // nc),
        in_specs=[pl.BlockSpec((1, nc, r1, LANES), lambda o, j: (o, j, 0, 0))] + [full(a) for a in cs],
        out_specs=pl.BlockSpec((1, nc, r1, 2 * LANES), lambda o, j: (o, j, 0, 0)),
        compiler_params=_cparams(("arbitrary", "arbitrary")),
        name="hyena_filter_fft",
    )(filt, *cs)


def _hyena_kernel(dbias_ref, v_ref, x1_ref, x2_ref, h_ref, fa_ref, gc_ref, gci_ref, fai_ref,
                  twl_r_ref, twl_i_ref, twr_r_ref, twr_i_ref, y_ref, *, r1):
    fa = fa_ref[...].astype(BF16)
    gc = gc_ref[...].astype(BF16)
    gci = gci_ref[...].astype(BF16)
    fai = fai_ref[...].astype(BF16)
    twl_r, twl_i = twl_r_ref[...], twl_i_ref[...]
    twr_r, twr_i = twr_r_ref[...], twr_i_ref[...]
    nc = v_ref.shape[2]
    half = r1 // 2
    c_base = pl.program_id(0) * nc

    def conv(sig, order, g0):
        rhs = jnp.concatenate(
            [jnp.concatenate([sig[c][0].astype(BF16), sig[c][1].astype(BF16)], axis=0)
             for c in range(HY_GROUP)], axis=1)
        x = _fwd_dft(rhs, fa, gc, twl_r, twl_i, r1)
        xr, xi = x[:, :LANES], x[:, LANES:]
        hh = jnp.concatenate([h_ref[order, g0 + c] for c in range(HY_GROUP)], axis=0)
        hr, hi = hh[:, :LANES], hh[:, LANES:]
        y = _inv_dft(xr * hr - xi * hi, xr * hi + xi * hr, gci, fai, twr_r, twr_i, r1)
        out = []
        for c in range(HY_GROUP):
            dcoef = dbias_ref[order, c_base + g0 + c]
            yc = y[:, c * LANES:(c + 1) * LANES]
            out.append([yc[:half] + dcoef * sig[c][0], yc[half:] + dcoef * sig[c][1]])
        return out

    for g0 in range(0, nc, HY_GROUP):
        v = [[v_ref[0, b, g0 + c].astype(F32) for b in range(2)] for c in range(HY_GROUP)]
        y1 = conv(v, 0, g0)
        z = [[x1_ref[0, b, g0 + c].astype(F32) * y1[c][b] for b in range(2)] for c in range(HY_GROUP)]
        y2 = conv(z, 1, g0)
        for c in range(HY_GROUP):
            for b in range(2):
                y_ref[0, b, g0 + c] = (x2_ref[0, b, g0 + c].astype(F32) * y2[c][b]).astype(y_ref.dtype)


def _hyena_conv(u_t, spectra, d_bias, consts, r1):
    bp, _, ch3, half, _ = u_t.shape
    ch = ch3 // 3
    nc = HY_CH_BLOCK
    nblk = ch // nc
    names = ["fa_c", "gc", "gci", "fai", "twl_r", "twl_i", "twr_r", "twr_i"]
    cs = [consts[k] for k in names]
    full = lambda a: pl.BlockSpec(a.shape, lambda j, p: (0,) * a.ndim)
    part = lambda k: pl.BlockSpec((1, 2, nc, half, LANES), lambda j, p: (p, 0, k * nblk + j, 0, 0))
    return pl.pallas_call(
        functools.partial(_hyena_kernel, r1=r1),
        out_shape=jax.ShapeDtypeStruct((bp, 2, ch, half, LANES), BF16),
        grid=(nblk, bp),
        in_specs=[pl.BlockSpec(memory_space=pltpu.SMEM), part(0), part(1), part(2),
                  pl.BlockSpec((HY_ORDER, nc, r1, 2 * LANES), lambda j, p: (0, j, 0, 0))] + [full(a) for a in cs],
        out_specs=pl.BlockSpec((1, 2, nc, half, LANES), lambda j, p: (p, 0, j, 0, 0)),
        compiler_params=_cparams(("arbitrary", "arbitrary")),
        name="hyena_conv",
    )(d_bias, u_t, u_t, u_t, spectra, *cs)


def _filter_inputs(l):
    t = jnp.arange(l, dtype=F32)
    t_norm = t / (l - 1)
    bands = (HY_EMB - 1) // 2
    f = jnp.linspace(1e-4, bands - 1, bands, dtype=F32)
    ang = (2.0 * math.pi * t / l)[:, None] * f[None, :]
    z = jnp.concatenate([t_norm[:, None], jnp.cos(ang), -jnp.sin(ang)], -1)
    deltas = jnp.linspace(math.log(HY_TARGET) / HY_SLOW, math.log(HY_TARGET) / HY_FAST, HY_WIDTH, dtype=F32)
    window = jnp.exp(-t_norm[:, None] * jnp.abs(deltas)[None, :])
    src = jnp.concatenate([jnp.arange(l), jnp.zeros((1,), jnp.int32), jnp.arange(l - 1, 0, -1)])
    valid = jnp.concatenate([jnp.ones((l,), F32), jnp.zeros((1,), F32), jnp.ones((l - 1,), F32)])
    zz = jnp.pad(z[src], ((0, 0), (0, LANES - HY_EMB)))
    win_t = (window[src] * valid[:, None]).T
    return zz, win_t


def _outproj_kernel(of_ref, ob_ref, g_ref, yh_ref, x_ref, gate_ref, sh_ref, sc_ref,
                    wg_ref, wh_ref, ng_ref, l1g_ref, l1b_ref, wr_ref, rb_ref,
                    xm_ref, h2_ref, gates_ref):
    o = of_ref[0, 0].astype(F32) + ob_ref[0, 0].astype(F32)
    g = g_ref[0].astype(F32)
    parts = []
    for h in range(GLA_HEADS):
        oh = o[:, h * GLA_DV:(h + 1) * GLA_DV]
        parts.append(oh * lax.rsqrt(jnp.mean(oh * oh, axis=-1, keepdims=True) + LN_EPS))
    y_gla = jnp.concatenate(parts, axis=1) * ng_ref[...] * (g * jax.nn.sigmoid(g))
    f = _dot(y_gla.astype(BF16), wg_ref[...]) + _dot_tn(yh_ref[0], wh_ref[...])
    x_mid = _layer_norm(DEEPNORM_ALPHA * x_ref[0] + gate_ref[0] * f) * l1g_ref[...] + l1b_ref[...]
    xm_ref[0] = x_mid
    h2 = _layer_norm(x_mid) * (1.0 + sc_ref[0]) + sh_ref[0]
    h2b = h2.astype(BF16)
    h2_ref[0] = h2b
    logits = _dot(h2b, wr_ref[...]) + rb_ref[...]
    lane = lax.broadcasted_iota(jnp.int32, logits.shape, 1)
    gates = jnp.zeros_like(logits)
    denom = jnp.zeros((logits.shape[0], 1), F32)
    m0 = None
    for _ in range(TOP_K):
        m = jnp.max(logits, axis=-1, keepdims=True)
        idx = jnp.min(jnp.where(logits == m, lane, LANES), axis=-1, keepdims=True)
        hit = lane == idx
        if m0 is None:
            m0 = m
        e = jnp.exp(m - m0)
        gates = gates + jnp.where(hit, e, 0.0)
        denom = denom + e
        logits = jnp.where(hit, NEG_BIG, logits)
    gates_ref[0] = gates / denom


def _output_projection(o_dirs, g, y_hy_t, x, gate1, shift2, scale2, w_gla, w_hy, norm_g, ln1_g, ln1_b, wr, rb):
    b, l, d = x.shape
    tt = TOK_TILE
    tok = lambda w: pl.BlockSpec((1, tt, w), lambda i, j: (i, j, 0))
    row = lambda: pl.BlockSpec((1, 1, d), lambda i, j: (i, 0, 0))
    full = lambda a: pl.BlockSpec(a.shape, lambda i, j: (0,) * a.ndim)
    consts = [w_gla, w_hy, norm_g, ln1_g, ln1_b, wr, rb]
    return pl.pallas_call(
        _outproj_kernel,
        out_shape=[jax.ShapeDtypeStruct((b, l, d), F32), jax.ShapeDtypeStruct((b, l, d), BF16),
                   jax.ShapeDtypeStruct((b, l, LANES), F32)],
        grid=(b, l // tt),
        in_specs=[
            pl.BlockSpec((1, 1, tt, GLA_VAL), lambda i, j: (0, i, j, 0)),
            pl.BlockSpec((1, 1, tt, GLA_VAL), lambda i, j: (1, i, j, 0)),
            tok(GLA_VAL),
            pl.BlockSpec((1, HY_WIDTH, tt), lambda i, j: (i, 0, j)),
            tok(d), row(), row(), row(),
        ] + [full(a) for a in consts],
        out_specs=[tok(d), tok(d), tok(LANES)],
        compiler_params=_cparams(("parallel", "arbitrary")),
        name="out_proj_router",
    )(o_dirs, o_dirs, g, y_hy_t, x, gate1, shift2, scale2, *consts)


def _moe_dense_kernel(h2_ref, gates_ref, xm_ref, gate_ref, w1_ref, b1_ref, w2_ref, b2_ref,
                      l2g_ref, l2b_ref, o_ref, acc_ref):
    e = pl.program_id(2)

    @pl.when(e == 0)
    def _():
        acc_ref[...] = jnp.zeros_like(acc_ref)

    hid = _dot(h2_ref[0], w1_ref[0]) + b1_ref[0]
    glu = jnp.minimum(hid[:, :D_EXPERT], SWIGLU_LIMIT)
    lin = jnp.clip(hid[:, D_EXPERT:], -SWIGLU_LIMIT, SWIGLU_LIMIT)
    act = glu * jax.nn.sigmoid(SWIGLU_ALPHA * glu) * (lin + 1.0)
    y = _dot(act.astype(BF16), w2_ref[0]) + b2_ref[0]
    gates = gates_ref[0]
    lane = lax.broadcasted_iota(jnp.int32, gates.shape, 1)
    ge = jnp.sum(jnp.where(lane == e, gates, 0.0), axis=-1, keepdims=True)
    acc_ref[...] += ge * y

    @pl.when(e == pl.num_programs(2) - 1)
    def _():
        pre = DEEPNORM_ALPHA * xm_ref[0] + gate_ref[0] * acc_ref[...]
        o_ref[0] = _layer_norm(pre) * l2g_ref[...] + l2b_ref[...]


def _moe_dense(h2, gates, x_mid, gate2, w1p, b1p, w2b, b2, ln2_g, ln2_b):
    b, l, d = x_mid.shape
    tm = min(MOE_TOK_TILE, l)
    ne = w1p.shape[0]
    tok = lambda w: pl.BlockSpec((1, tm, w), lambda i, j, e: (i, j, 0))
    return pl.pallas_call(
        _moe_dense_kernel,
        out_shape=jax.ShapeDtypeStruct((b, l, d), F32),
        grid=(b, l // tm, ne),
        in_specs=[
            tok(d), tok(LANES), tok(d),
            pl.BlockSpec((1, 1, d), lambda i, j, e: (i, 0, 0)),
            pl.BlockSpec((1, d, 2 * D_EXPERT), lambda i, j, e: (e, 0, 0)),
            pl.BlockSpec((1, 1, 2 * D_EXPERT), lambda i, j, e: (e, 0, 0)),
            pl.BlockSpec((1, D_EXPERT, d), lambda i, j, e: (e, 0, 0)),
            pl.BlockSpec((1, 1, d), lambda i, j, e: (e, 0, 0)),
            pl.BlockSpec((1, d), lambda i, j, e: (0, 0)),
            pl.BlockSpec((1, d), lambda i, j, e: (0, 0)),
        ],
        out_specs=tok(d),
        scratch_shapes=[pltpu.VMEM((tm, d), F32)],
        compiler_params=_cparams(("parallel", "parallel", "arbitrary")),
        name="moe_dense",
    )(h2, gates, x_mid, gate2, w1p, b1p, w2b, b2, ln2_g, ln2_b)


def kernel(x, c, ctx, c_ctx, ada_w, ada_b, w_in, gla_wa_f, gla_ba_f, gla_wa_b, gla_ba_b, gla_norm_g,
           hy_conv_w, hy_conv_b, hy_flt_w1, hy_flt_b1, hy_flt_w2, hy_flt_b2, hy_flt_wout, hy_flt_freq,
           hy_bias_d, w_out, ln1_g, ln1_b, router_w, router_b, exp_w1, exp_b1, exp_w2, exp_b2, ln2_g, ln2_b):
    batch, seq_len, d = x.shape
    lyr = 0
    ch = HY_WIDTH

    n_rows = 8 * ((batch + 1 + 7) // 8)
    cvec = jnp.zeros((n_rows, d), F32).at[:batch].set(c).at[batch].set(c_ctx)
    mod = _modulation(cvec, ada_w[lyr], ada_b[lyr][None, :])
    part = lambda rows, i: rows[:, None, i * d:(i + 1) * d]
    mod_x = mod[:batch]
    mod_c = jnp.broadcast_to(mod[batch:batch + 1], (batch, 6 * d))

    w = w_in[lyr]
    o_q, o_k, o_v, o_g = 0, GLA_KEY, 2 * GLA_KEY, 2 * GLA_KEY + GLA_VAL
    o_a = o_g + GLA_VAL
    o_h = o_a + 2 * GLA_RANK
    a_cols = jnp.pad(w[:, o_a:o_h], ((0, 0), (0, A_PAD - 2 * GLA_RANK)))
    q_cols = w[:, o_q:o_k] * (GLA_DK ** -0.5)
    w_main = jnp.concatenate([q_cols, w[:, o_k:o_a], a_cols], axis=1).astype(BF16)
    w_ctx = jnp.concatenate([q_cols, w[:, o_k:o_g], a_cols], axis=1).astype(BF16)
    w_hy_t = w[:, o_h:].T.astype(BF16)
    conv_w = jnp.broadcast_to(hy_conv_w[lyr][:, :, None], (HY_CONV, 3 * ch, LANES))
    conv_b = jnp.broadcast_to(hy_conv_b[lyr][:, None], (3 * ch, LANES))

    ctx_qkva = _input_projection(ctx, part(mod_c, 0), part(mod_c, 1), w_ctx)
    q, k, v, g, a_low, u_t = _input_projection(x, part(mod_x, 0), part(mod_x, 1), w_main, w_hy_t, conv_w, conv_b)

    wa = jnp.zeros((2, A_PAD, GLA_KEY), F32)
    wa = wa.at[0, :GLA_RANK].set(gla_wa_f[lyr]).at[1, GLA_RANK:2 * GLA_RANK].set(gla_wa_b[lyr]).astype(BF16)
    ba = jnp.stack([gla_ba_f[lyr], gla_ba_b[lyr]])[:, None, :]
    o_dirs = _gla_scan(ctx_qkva, (q, k, v, a_low), wa, ba)

    r1 = 2 * seq_len // LANES
    consts = _dft_constants(r1)
    zz, win_t = _filter_inputs(seq_len)
    w1p = jnp.pad(hy_flt_w1[lyr], ((0, LANES - HY_EMB), (0, 0)))
    wo_t = hy_flt_wout[lyr].reshape(HY_FH, HY_ORDER, 2, ch).transpose(1, 2, 3, 0)
    filt = _filter_mlp(zz, win_t, w1p, hy_flt_b1[lyr][None], hy_flt_w2[lyr], hy_flt_b2[lyr][None],
                       hy_flt_freq[lyr][None], wo_t)
    spectra = _filter_spectra(filt.reshape(HY_ORDER, ch, r1, LANES), consts, r1)
    y_hy = _hyena_conv(u_t.reshape(batch // 2, 2, 3 * ch, r1 // 2, LANES), spectra, hy_bias_d[lyr], consts, r1)
    y_hy_t = y_hy.reshape(batch, ch, seq_len)

    wo = w_out[lyr].astype(BF16)
    wr = jnp.pad(router_w[lyr], ((0, 0), (0, LANES - N_EXPERTS))).astype(BF16)
    rb = jnp.pad(router_b[lyr], (0, LANES - N_EXPERTS), constant_values=NEG_BIG)[None, :]
    norm_g = jnp.tile(gla_norm_g[lyr], GLA_HEADS)[None, :]
    x_mid, h2, gates = _output_projection(
        o_dirs, g, y_hy_t, x, part(mod_x, 2), part(mod_x, 3), part(mod_x, 4),
        wo[:GLA_VAL], wo[GLA_VAL:], norm_g, ln1_g[lyr][None], ln1_b[lyr][None], wr, rb)

    w1 = exp_w1[lyr]
    w1p_e = jnp.concatenate([w1[..., 0::2], w1[..., 1::2]], axis=-1).astype(BF16)
    b1 = exp_b1[lyr]
    b1p_e = jnp.concatenate([b1[..., 0::2], b1[..., 1::2]], axis=-1)[:, None, :]
    return _moe_dense(h2, gates, x_mid, part(mod_x, 5), w1p_e, b1p_e, exp_w2[lyr].astype(BF16),
                      exp_b2[lyr][:, None, :], ln2_g[lyr][None], ln2_b[lyr][None])
```

```python
import functools
import math

import numpy as np
import jax
import jax.numpy as jnp
from jax import lax
from jax.experimental import pallas as pl
from jax.experimental.pallas import tpu as pltpu

F32 = jnp.float32
BF16 = jnp.bfloat16

D_MODEL = 1024
DEPTH = 1
GRID_W = 64
LN_EPS = 1e-6
DEEPNORM_ALPHA = (2 * DEPTH) ** 0.25
GLA_HEADS = 4
GLA_DK = 64
GLA_DV = 128
GLA_KEY = GLA_HEADS * GLA_DK
GLA_VAL = GLA_HEADS * GLA_DV
GLA_RANK = 16
GLA_TAU = 16.0
HY_WIDTH = D_MODEL - GLA_VAL
HY_ORDER = 2
HY_CONV = 3
HY_EMB = 33
HY_FH = 64
HY_TARGET = 1e-2
HY_FAST = 0.3
HY_SLOW = 1.5
N_EXPERTS = 32
TOP_K = 4
D_EXPERT = D_MODEL
SWIGLU_ALPHA = 1.702
SWIGLU_LIMIT = 7.0

LANES = 128
VMEM_LIMIT = 56 * 1024 * 1024

TOK_TILE = 256
GLA_CHUNK = 64
GLA_BLOCK = 256
A_PAD = LANES
HY_GROUP = 2
HY_CH_BLOCK = 8
FILT_TILE = 512
MOE_TILE = 256
DMA_UNROLL = 8
SWIGLU_BLOCK = 2 * LANES
NEG_BIG = -1e30


def _cparams(sem):
    return pltpu.CompilerParams(dimension_semantics=sem, vmem_limit_bytes=VMEM_LIMIT)


def _layer_norm(x):
    mu = jnp.mean(x, axis=-1, keepdims=True)
    xc = x - mu
    return xc * lax.rsqrt(jnp.mean(xc * xc, axis=-1, keepdims=True) + LN_EPS)


def _dot(a, b):
    return jnp.dot(a, b, preferred_element_type=F32)


def _dot_nt(a, b):
    return lax.dot_general(a, b, (((1,), (1,)), ((), ())), preferred_element_type=F32)


def _dot_tn(a, b):
    return lax.dot_general(a, b, (((0,), (0,)), ((), ())), preferred_element_type=F32)


def _mod_kernel(c_ref, w_ref, b_ref, o_ref):
    c = c_ref[...]
    s = c * jax.nn.sigmoid(c)
    o_ref[...] = _dot(s.astype(BF16), w_ref[...].astype(BF16)) + b_ref[...]


def _modulation(cvec, ada_w, ada_b):
    rows, d = cvec.shape
    n = ada_w.shape[1]
    tn = 1024
    return pl.pallas_call(
        _mod_kernel,
        out_shape=jax.ShapeDtypeStruct((rows, n), F32),
        grid=(n // tn,),
        in_specs=[
            pl.BlockSpec((rows, d), lambda j: (0, 0)),
            pl.BlockSpec((d, tn), lambda j: (0, j)),
            pl.BlockSpec((1, tn), lambda j: (0, j)),
        ],
        out_specs=pl.BlockSpec((rows, tn), lambda j: (0, j)),
        compiler_params=_cparams(("arbitrary",)),
        name="adaln_mod",
    )(cvec, ada_w, ada_b)


def _inproj_kernel(x_ref, sh_ref, sc_ref, wm_ref, *rest, with_hy):
    if with_hy:
        wh_ref, cw_ref, cb_ref, q_ref, k_ref, v_ref, g_ref, a_ref, u_ref = rest
    else:
        q_ref, k_ref, v_ref, a_ref = rest
    h = _layer_norm(x_ref[0]) * (1.0 + sc_ref[0]) + sh_ref[0]
    hb = h.astype(BF16)
    u = _dot(hb, wm_ref[...])
    q_ref[0] = u[:, :GLA_KEY].astype(BF16)
    k_ref[0] = u[:, GLA_KEY:2 * GLA_KEY].astype(BF16)
    v_ref[0] = u[:, 2 * GLA_KEY:2 * GLA_KEY + GLA_VAL].astype(BF16)
    off = 2 * GLA_KEY + GLA_VAL
    if with_hy:
        g_ref[0] = u[:, off:off + GLA_VAL].astype(BF16)
        off += GLA_VAL
    a_ref[0] = u[:, off:off + A_PAD]
    if with_hy:
        ut = _dot_nt(wh_ref[...], hb)
        lane = lax.broadcasted_iota(jnp.int32, (1, LANES), 1) % GRID_W
        not_first = (lane != 0).astype(F32)
        not_last = (lane != GRID_W - 1).astype(F32)
        tt = ut.shape[1]
        for j in range(tt // LANES):
            c = ut[:, j * LANES:(j + 1) * LANES]
            left = pltpu.roll(c, 1, axis=1) * not_first
            right = pltpu.roll(c, LANES - 1, axis=1) * not_last
            y = cw_ref[0] * left + cw_ref[1] * c + cw_ref[2] * right + cb_ref[...]
            u_ref[0, :, j * LANES:(j + 1) * LANES] = y.astype(BF16)


def _input_projection(x, shift, scale, w_main, w_hy_t=None, conv_w=None, conv_b=None):
    b, l, d = x.shape
    tt = TOK_TILE
    with_hy = w_hy_t is not None
    nm = w_main.shape[1]
    tok = lambda width, dt: jax.ShapeDtypeStruct((b, l, width), dt)
    tok_spec = lambda width: pl.BlockSpec((1, tt, width), lambda i, j: (i, j, 0))
    in_specs = [
        pl.BlockSpec((1, tt, d), lambda i, j: (i, j, 0)),
        pl.BlockSpec((1, 1, d), lambda i, j: (i, 0, 0)),
        pl.BlockSpec((1, 1, d), lambda i, j: (i, 0, 0)),
        pl.BlockSpec((d, nm), lambda i, j: (0, 0)),
    ]
    args = [x, shift, scale, w_main]
    if with_hy:
        ch = w_hy_t.shape[0]
        in_specs += [
            pl.BlockSpec((ch, d), lambda i, j: (0, 0)),
            pl.BlockSpec((HY_CONV, ch, LANES), lambda i, j: (0, 0, 0)),
            pl.BlockSpec((ch, LANES), lambda i, j: (0, 0)),
        ]
        args += [w_hy_t, conv_w, conv_b]
        out_shape = [tok(GLA_KEY, BF16), tok(GLA_KEY, BF16), tok(GLA_VAL, BF16), tok(GLA_VAL, BF16),
                     tok(A_PAD, F32), jax.ShapeDtypeStruct((b, ch, l), BF16)]
        out_specs = [tok_spec(GLA_KEY), tok_spec(GLA_KEY), tok_spec(GLA_VAL), tok_spec(GLA_VAL),
                     tok_spec(A_PAD), pl.BlockSpec((1, ch, tt), lambda i, j: (i, 0, j))]
    else:
        out_shape = [tok(GLA_KEY, BF16), tok(GLA_KEY, BF16), tok(GLA_VAL, BF16), tok(A_PAD, F32)]
        out_specs = [tok_spec(GLA_KEY), tok_spec(GLA_KEY), tok_spec(GLA_VAL), tok_spec(A_PAD)]
    return pl.pallas_call(
        functools.partial(_inproj_kernel, with_hy=with_hy),
        out_shape=out_shape,
        grid=(b, l // tt),
        in_specs=in_specs,
        out_specs=out_specs,
        compiler_params=_cparams(("parallel", "arbitrary")),
        name="in_proj_hy" if with_hy else "in_proj_ctx",
    )(*args)


def _gla_kernel(qc_ref, kc_ref, vc_ref, ac_ref, ql_ref, kl_ref, vl_ref, al_ref,
                wa_ref, ba_ref, tri_ref, cm_ref, hm_ref, bd_ref, o_ref, st_ref,
                *, n_ctx_blocks, chunk):
    d = pl.program_id(1)
    s = pl.program_id(2)
    blk = ql_ref.shape[1]
    n_chunks = blk // chunk

    @pl.when(s == 0)
    def _():
        st_ref[...] = jnp.zeros_like(st_ref)

    is_ctx = s < n_ctx_blocks
    tri = tri_ref[0]
    cmask = cm_ref[0]
    bdmask = bd_ref[...]
    for n in range(n_chunks):
        ci = n + d * (n_chunks - 1 - 2 * n)
        r0 = pl.multiple_of(ci * chunk, chunk)
        rows = pl.ds(r0, chunk)
        q = jnp.where(is_ctx, qc_ref[0, rows, :], ql_ref[0, rows, :]).astype(F32)
        k = jnp.where(is_ctx, kc_ref[0, rows, :], kl_ref[0, rows, :]).astype(F32)
        v = jnp.where(is_ctx, vc_ref[0, rows, :], vl_ref[0, rows, :])
        a = jnp.where(is_ctx, ac_ref[0, rows, :], al_ref[0, rows, :])
        z = _dot(a.astype(BF16), wa_ref[0]) + ba_ref[0]
        g = (jnp.minimum(z, 0.0) - jnp.log(1.0 + jnp.exp(-jnp.abs(z)))) * (1.0 / GLA_TAU)
        g_hi = g.astype(BF16)
        g_lo = (g - g_hi.astype(F32)).astype(BF16)
        cum = _dot(tri, g_hi) + _dot(tri, g_lo)
        total = jnp.sum(g, axis=0, keepdims=True)
        qs = (q * jnp.exp(cum)).astype(BF16)
        ks = (k * jnp.exp(-cum)).astype(BF16)
        kst = (k * jnp.exp(total - cum)).astype(BF16)
        stb = st_ref[...].astype(BF16)
        o = _dot_nt(qs, stb)
        parts = []
        for h in range(GLA_HEADS):
            a_h = _dot_nt(qs * hm_ref[h], ks) * cmask
            parts.append(_dot(a_h.astype(BF16), v[:, h * GLA_DV:(h + 1) * GLA_DV]))
        o = o + jnp.concatenate(parts, axis=1)
        o_ref[0, 0, rows, :] = o.astype(o_ref.dtype)
        ut = _dot_tn(v, kst)
        st_ref[...] = st_ref[...] * jnp.exp(total) + ut * bdmask


def _gla_scan(ctx_qkva, lat_qkva, wa, ba, chunk=GLA_CHUNK, blk=GLA_BLOCK):
    qc, kc, vc, ac = ctx_qkva
    ql, kl, vl, al = lat_qkva
    b, l, _ = ql.shape
    ctx_len = qc.shape[1]
    n_ctx = ctx_len // blk
    n_lat = l // blk
    c = chunk
    idx = np.arange(c)
    tri = np.stack([idx[:, None] >= idx[None, :], idx[:, None] <= idx[None, :]]).astype(np.float32)
    hmask = np.zeros((GLA_HEADS, 1, GLA_KEY), np.float32)
    for h in range(GLA_HEADS):
        hmask[h, 0, h * GLA_DK:(h + 1) * GLA_DK] = 1.0
    bd = (np.arange(GLA_VAL)[:, None] // GLA_DV == np.arange(GLA_KEY)[None, :] // GLA_DK).astype(np.float32)

    def lat_block(i, d, s):
        t = jnp.maximum(s - n_ctx, 0)
        return t + d * (n_lat - 1 - 2 * t)

    def ctx_block(i, d, s):
        t = jnp.minimum(s, n_ctx - 1)
        return t + d * (n_ctx - 1 - 2 * t)

    ctx_spec = lambda w: pl.BlockSpec((1, blk, w), lambda i, d, s: (i, ctx_block(i, d, s), 0))
    lat_spec = lambda w: pl.BlockSpec((1, blk, w), lambda i, d, s: (i, lat_block(i, d, s), 0))
    return pl.pallas_call(
        functools.partial(_gla_kernel, n_ctx_blocks=n_ctx, chunk=c),
        out_shape=jax.ShapeDtypeStruct((2, b, l, GLA_VAL), BF16),
        grid=(b, 2, n_ctx + n_lat),
        in_specs=[
            ctx_spec(GLA_KEY), ctx_spec(GLA_KEY), ctx_spec(GLA_VAL), ctx_spec(A_PAD),
            lat_spec(GLA_KEY), lat_spec(GLA_KEY), lat_spec(GLA_VAL), lat_spec(A_PAD),
            pl.BlockSpec((1, A_PAD, GLA_KEY), lambda i, d, s: (d, 0, 0)),
            pl.BlockSpec((1, 1, GLA_KEY), lambda i, d, s: (d, 0, 0)),
            pl.BlockSpec((1, c, c), lambda i, d, s: (d, 0, 0)),
            pl.BlockSpec((1, c, c), lambda i, d, s: (d, 0, 0)),
            pl.BlockSpec((GLA_HEADS, 1, GLA_KEY), lambda i, d, s: (0, 0, 0)),
            pl.BlockSpec((GLA_VAL, GLA_KEY), lambda i, d, s: (0, 0)),
        ],
        out_specs=pl.BlockSpec((1, 1, blk, GLA_VAL), lambda i, d, s: (d, i, lat_block(i, d, s), 0)),
        scratch_shapes=[pltpu.VMEM((GLA_VAL, GLA_KEY), F32)],
        compiler_params=_cparams(("parallel", "arbitrary", "arbitrary")),
        name="gla_scan",
    )(qc, kc, vc, ac, ql, kl, vl, al, wa, ba,
      jnp.asarray(tri, BF16), jnp.asarray(tri, F32), jnp.asarray(hmask, F32), jnp.asarray(bd, F32))


def _dft_constants(r1):
    n = r1 * LANES
    h = r1 // 2
    k1 = np.arange(r1)
    f1 = np.exp(-2j * np.pi * np.outer(k1, k1) / r1)
    f2 = np.exp(-2j * np.pi * np.outer(np.arange(LANES), np.arange(LANES)) / LANES)
    tw = np.exp(-2j * np.pi * np.outer(k1, np.arange(LANES)) / n)
    fa_c = np.block([[f1.real[:, :h], -f1.imag[:, :h]], [f1.imag[:, :h], f1.real[:, :h]]])
    fa_r = np.concatenate([f1.real, f1.imag], axis=0)
    gc = np.block([[f2.real, f2.imag], [-f2.imag, f2.real]])
    gci = np.block([[f2.real, -f2.imag], [f2.imag, f2.real]])
    fai = np.block([[f1.real[:h], f1.imag[:h]], [-f1.imag[:h], f1.real[:h]]]) / n
    tw_lane = np.tile(tw, (1, HY_GROUP))
    tw_row = np.tile(tw, (HY_GROUP, 1))
    f = lambda a: jnp.asarray(a, F32)
    return dict(fa_c=f(fa_c), fa_r=f(fa_r), gc=f(gc), gci=f(gci), fai=f(fai),
                twl_r=f(tw_lane.real), twl_i=f(tw_lane.imag), twr_r=f(tw_row.real), twr_i=f(tw_row.imag))


def _fwd_dft(rhs, fa, gc, twl_r, twl_i, r1):
    a = _dot(fa, rhs)
    ar, ai = a[:r1], a[r1:]
    br = (ar * twl_r - ai * twl_i).astype(BF16)
    bi = (ar * twl_i + ai * twl_r).astype(BF16)
    lhs = jnp.concatenate(
        [jnp.concatenate([br[:, c * LANES:(c + 1) * LANES], bi[:, c * LANES:(c + 1) * LANES]], axis=1)
         for c in range(HY_GROUP)], axis=0)
    return _dot(lhs, gc)


def _inv_dft(yr, yi, gci, fai, twr_r, twr_i, r1):
    lhs = jnp.concatenate([yr, yi], axis=1).astype(BF16)
    c = _dot(lhs, gci)
    cr, ci = c[:, :LANES], c[:, LANES:]
    dr = (cr * twr_r + ci * twr_i).astype(BF16)
    di = (ci * twr_r - cr * twr_i).astype(BF16)
    rhs = jnp.concatenate(
        [jnp.concatenate([dr[g * r1:(g + 1) * r1], di[g * r1:(g + 1) * r1]], axis=0)
         for g in range(HY_GROUP)], axis=1)
    return _dot(fai, rhs)


def _filter_mlp_kernel(zz_ref, tn_ref, rate_ref, w1_ref, b1_ref, w2_ref, b2_ref, fr_ref, wo_ref, o_ref):
    hp = lax.Precision.HIGHEST
    fr = fr_ref[...]
    hid = jnp.sin(fr * (jnp.dot(zz_ref[...], w1_ref[...], precision=hp, preferred_element_type=F32) + b1_ref[...]))
    hid = jnp.sin(fr * (jnp.dot(hid, w2_ref[...], precision=hp, preferred_element_type=F32) + b2_ref[...]))
    lt = tn_ref.shape[1]
    rate = jnp.concatenate([rate_ref[...]] * (lt // LANES), axis=1)
    window = jnp.exp(-tn_ref[0:1, :] * rate) * tn_ref[1:2, :]
    for o in range(HY_ORDER):
        ht = lax.dot_general(wo_ref[o, 0], hid, (((1,), (1,)), ((), ())), precision=hp,
                             preferred_element_type=F32)
        o_ref[o] = ht * window


def _filter_mlp(zz, tn_rows, rate, w1p, b1, w2, b2, freq, wo_t):
    n2l = zz.shape[0]
    l = n2l // 2
    lt = min(FILT_TILE, l)
    nt = l // lt
    ch = rate.shape[0]
    full = lambda shape: pl.BlockSpec(shape, lambda d, j: (0,) * len(shape))
    return pl.pallas_call(
        _filter_mlp_kernel,
        out_shape=jax.ShapeDtypeStruct((HY_ORDER, ch, n2l), F32),
        grid=(2, nt),
        in_specs=[
            pl.BlockSpec((lt, LANES), lambda d, j: (d * nt + j, 0)),
            pl.BlockSpec((8, lt), lambda d, j: (0, d * nt + j)),
            full((ch, LANES)),
            full((LANES, HY_FH)), full((1, HY_FH)), full((HY_FH, HY_FH)), full((1, HY_FH)), full((1, HY_FH)),
            pl.BlockSpec((HY_ORDER, 1, ch, HY_FH), lambda d, j: (0, d, 0, 0)),
        ],
        out_specs=pl.BlockSpec((HY_ORDER, ch, lt), lambda d, j: (0, 0, d * nt + j)),
        compiler_params=_cparams(("arbitrary", "arbitrary")),
        name="hyena_filter_mlp",
    )(zz, tn_rows, rate, w1p, b1, w2, b2, freq, wo_t)


def _filter_fft_kernel(f_ref, fa_ref, gc_ref, twl_r_ref, twl_i_ref, h_ref, *, r1):
    fa = fa_ref[...].astype(BF16)
    gc = gc_ref[...].astype(BF16)
    nc = f_ref.shape[1]
    for g0 in range(0, nc, HY_GROUP):
        rhs = jnp.concatenate([f_ref[0, g0 + c].astype(BF16) for c in range(HY_GROUP)], axis=1)
        x = _fwd_dft(rhs, fa, gc, twl_r_ref[...], twl_i_ref[...], r1)
        for c in range(HY_GROUP):
            h_ref[0, g0 + c] = x[c * r1:(c + 1) * r1]


def _filter_spectra(filt, consts, r1):
    order, ch = filt.shape[:2]
    nc = HY_CH_BLOCK
    full = lambda a: pl.BlockSpec(a.shape, lambda o, j: (0,) * a.ndim)
    cs = [consts["fa_r"], consts["gc"], consts["twl_r"], consts["twl_i"]]
    return pl.pallas_call(
        functools.partial(_filter_fft_kernel, r1=r1),
        out_shape=jax.ShapeDtypeStruct((order, ch, r1, 2 * LANES), F32),
        grid=(order, ch // nc),
        in_specs=[pl.BlockSpec((1, nc, r1, LANES), lambda o, j: (o, j, 0, 0))] + [full(a) for a in cs],
        out_specs=pl.BlockSpec((1, nc, r1, 2 * LANES), lambda o, j: (o, j, 0, 0)),
        compiler_params=_cparams(("arbitrary", "arbitrary")),
        name="hyena_filter_fft",
    )(filt, *cs)


def _hyena_kernel(dbias_ref, v_ref, x1_ref, x2_ref, h_ref, fa_ref, gc_ref, gci_ref, fai_ref,
                  twl_r_ref, twl_i_ref, twr_r_ref, twr_i_ref, y_ref, *, r1):
    fa = fa_ref[...].astype(BF16)
    gc = gc_ref[...].astype(BF16)
    gci = gci_ref[...].astype(BF16)
    fai = fai_ref[...].astype(BF16)
    twl_r, twl_i = twl_r_ref[...], twl_i_ref[...]
    twr_r, twr_i = twr_r_ref[...], twr_i_ref[...]
    nc = v_ref.shape[2]
    half = r1 // 2
    c_base = pl.program_id(0) * nc

    def conv(sig, order, g0):
        rhs = jnp.concatenate(
            [jnp.concatenate([sig[c][0].astype(BF16), sig[c][1].astype(BF16)], axis=0)
             for c in range(HY_GROUP)], axis=1)
        x = _fwd_dft(rhs, fa, gc, twl_r, twl_i, r1)
        xr, xi = x[:, :LANES], x[:, LANES:]
        hh = jnp.concatenate([h_ref[order, g0 + c] for c in range(HY_GROUP)], axis=0)
        hr, hi = hh[:, :LANES], hh[:, LANES:]
        y = _inv_dft(xr * hr - xi * hi, xr * hi + xi * hr, gci, fai, twr_r, twr_i, r1)
        out = []
        for c in range(HY_GROUP):
            dcoef = dbias_ref[order, c_base + g0 + c]
            yc = y[:, c * LANES:(c + 1) * LANES]
            out.append([yc[:half] + dcoef * sig[c][0], yc[half:] + dcoef * sig[c][1]])
        return out

    for g0 in range(0, nc, HY_GROUP):
        v = [[v_ref[0, b, g0 + c].astype(F32) for b in range(2)] for c in range(HY_GROUP)]
        y1 = conv(v, 0, g0)
        z = [[x1_ref[0, b, g0 + c].astype(F32) * y1[c][b] for b in range(2)] for c in range(HY_GROUP)]
        y2 = conv(z, 1, g0)
        for c in range(HY_GROUP):
            for b in range(2):
                y_ref[0, b, g0 + c] = (x2_ref[0, b, g0 + c].astype(F32) * y2[c][b]).astype(y_ref.dtype)


def _hyena_conv(u_t, spectra, d_bias, consts, r1):
    bp, _, ch3, half, _ = u_t.shape
    ch = ch3 // 3
    nc = HY_CH_BLOCK
    nblk = ch // nc
    names = ["fa_c", "gc", "gci", "fai", "twl_r", "twl_i", "twr_r", "twr_i"]
    cs = [consts[k] for k in names]
    full = lambda a: pl.BlockSpec(a.shape, lambda j, p: (0,) * a.ndim)
    part = lambda k: pl.BlockSpec((1, 2, nc, half, LANES), lambda j, p: (p, 0, k * nblk + j, 0, 0))
    return pl.pallas_call(
        functools.partial(_hyena_kernel, r1=r1),
        out_shape=jax.ShapeDtypeStruct((bp, 2, ch, half, LANES), BF16),
        grid=(nblk, bp),
        in_specs=[pl.BlockSpec(memory_space=pltpu.SMEM), part(0), part(1), part(2),
                  pl.BlockSpec((HY_ORDER, nc, r1, 2 * LANES), lambda j, p: (0, j, 0, 0))] + [full(a) for a in cs],
        out_specs=pl.BlockSpec((1, 2, nc, half, LANES), lambda j, p: (p, 0, j, 0, 0)),
        compiler_params=_cparams(("arbitrary", "arbitrary")),
        name="hyena_conv",
    )(d_bias, u_t, u_t, u_t, spectra, *cs)


def _filter_inputs(l):
    n = jnp.arange(2 * l, dtype=jnp.int32)
    t = jnp.where(n < l, n, 2 * l - n).astype(F32)
    valid = (n != l).astype(F32)
    t_norm = t / (l - 1)
    bands = (HY_EMB - 1) // 2
    f = jnp.linspace(1e-4, bands - 1, bands, dtype=F32)
    ang = (2.0 * math.pi * t / l)[:, None] * f[None, :]
    z = jnp.concatenate([t_norm[:, None], jnp.cos(ang), -jnp.sin(ang)], -1)
    zz = jnp.pad(z, ((0, 0), (0, LANES - HY_EMB)))
    tn_rows = jnp.zeros((8, 2 * l), F32).at[0].set(t_norm).at[1].set(valid)
    deltas = jnp.linspace(math.log(HY_TARGET) / HY_SLOW, math.log(HY_TARGET) / HY_FAST, HY_WIDTH, dtype=F32)
    rate = jnp.broadcast_to(jnp.abs(deltas)[:, None], (HY_WIDTH, LANES))
    return zz, tn_rows, rate


def _outproj_kernel(of_ref, ob_ref, g_ref, yh_ref, x_ref, gate_ref, sh_ref, sc_ref,
                    wg_ref, wh_ref, ng_ref, l1g_ref, l1b_ref, wr_ref, rb_ref, ltri_ref,
                    xm_ref, h2_ref, se_ref, sr_ref, sw_ref, cnt_ref, carry_ref):
    @pl.when((pl.program_id(0) == 0) & (pl.program_id(1) == 0))
    def _():
        carry_ref[...] = jnp.zeros_like(carry_ref)

    o = of_ref[0, 0].astype(F32) + ob_ref[0, 0].astype(F32)
    g = g_ref[0].astype(F32)
    parts = []
    for h in range(GLA_HEADS):
        oh = o[:, h * GLA_DV:(h + 1) * GLA_DV]
        parts.append(oh * lax.rsqrt(jnp.mean(oh * oh, axis=-1, keepdims=True) + LN_EPS))
    y_gla = jnp.concatenate(parts, axis=1) * ng_ref[...] * (g * jax.nn.sigmoid(g))
    f = _dot(y_gla.astype(BF16), wg_ref[...]) + _dot_tn(yh_ref[0], wh_ref[...])
    x_mid = _layer_norm(DEEPNORM_ALPHA * x_ref[0] + gate_ref[0] * f) * l1g_ref[...] + l1b_ref[...]
    xm_ref[0] = x_mid
    h2 = _layer_norm(x_mid) * (1.0 + sc_ref[0]) + sh_ref[0]
    h2_ref[0] = h2

    logits = _dot(h2.astype(BF16), wr_ref[...]) + rb_ref[...]
    lane = lax.broadcasted_iota(jnp.int32, logits.shape, 1).astype(F32)
    hits, idxs, exps = [], [], []
    m0 = None
    for _ in range(TOP_K):
        m = jnp.max(logits, axis=-1, keepdims=True)
        idx = jnp.min(jnp.where(logits == m, lane, float(LANES)), axis=-1, keepdims=True)
        hit = lane == idx
        m0 = m if m0 is None else m0
        hits.append(hit)
        idxs.append(idx)
        exps.append(jnp.exp(m - m0))
        logits = jnp.where(hit, NEG_BIG, logits)
    denom = exps[0]
    sel = jnp.where(hits[0], 1.0, 0.0)
    for kk in range(1, TOP_K):
        denom = denom + exps[kk]
        sel = sel + jnp.where(hits[kk], 1.0, 0.0)
    rank_all = _dot(ltri_ref[...], sel.astype(BF16)) + carry_ref[0:1, :]
    carry_ref[0:1, :] = carry_ref[0:1, :] + jnp.sum(sel, axis=0, keepdims=True)
    cnt_ref[...] = jnp.broadcast_to(carry_ref[0:1, :], cnt_ref.shape).astype(jnp.int32)
    se = jnp.zeros(logits.shape, F32)
    sr = jnp.zeros(logits.shape, F32)
    sw = jnp.zeros(logits.shape, F32)
    for kk in range(TOP_K):
        rk = jnp.sum(jnp.where(hits[kk], rank_all, 0.0), axis=-1, keepdims=True)
        col = lane == float(kk)
        se = jnp.where(col, idxs[kk], se)
        sr = jnp.where(col, rk, sr)
        sw = jnp.where(col, exps[kk] / denom, sw)
    se_ref[0] = se.astype(jnp.int32)
    sr_ref[0] = sr.astype(jnp.int32)
    sw_ref[0] = sw


def _output_projection(o_dirs, g, y_hy_t, x, gate1, shift2, scale2, w_gla, w_hy, norm_g, ln1_g, ln1_b, wr, rb):
    b, l, d = x.shape
    tt = TOK_TILE
    tok = lambda w: pl.BlockSpec((1, tt, w), lambda i, j: (i, j, 0))
    row = lambda: pl.BlockSpec((1, 1, d), lambda i, j: (i, 0, 0))
    full = lambda a: pl.BlockSpec(a.shape, lambda i, j: (0,) * a.ndim)
    ltri = jnp.asarray(np.tril(np.ones((tt, tt), np.float32), -1), BF16)
    consts = [w_gla, w_hy, norm_g, ln1_g, ln1_b, wr, rb, ltri]
    lane_i = jax.ShapeDtypeStruct((b, l, LANES), jnp.int32)
    return pl.pallas_call(
        _outproj_kernel,
        out_shape=[jax.ShapeDtypeStruct((b, l, d), F32), jax.ShapeDtypeStruct((b, l, d), F32),
                   lane_i, lane_i, jax.ShapeDtypeStruct((b, l, LANES), F32),
                   jax.ShapeDtypeStruct((8, LANES), jnp.int32)],
        grid=(b, l // tt),
        in_specs=[
            pl.BlockSpec((1, 1, tt, GLA_VAL), lambda i, j: (0, i, j, 0)),
            pl.BlockSpec((1, 1, tt, GLA_VAL), lambda i, j: (1, i, j, 0)),
            tok(GLA_VAL),
            pl.BlockSpec((1, HY_WIDTH, tt), lambda i, j: (i, 0, j)),
            tok(d), row(), row(), row(),
        ] + [full(a) for a in consts],
        out_specs=[tok(d), tok(d), tok(LANES), tok(LANES), tok(LANES),
                   pl.BlockSpec((8, LANES), lambda i, j: (0, 0))],
        scratch_shapes=[pltpu.VMEM((8, LANES), F32)],
        compiler_params=_cparams(("arbitrary", "arbitrary")),
        name="out_proj_router",
    )(o_dirs, o_dirs, g, y_hy_t, x, gate1, shift2, scale2, *consts)


def _wprep_kernel(w1_ref, w2_ref, p_ref, w1o_ref, w2o_ref):
    p = p_ref[...]
    for j in range(w1_ref.shape[2] // SWIGLU_BLOCK):
        cols = slice(j * SWIGLU_BLOCK, (j + 1) * SWIGLU_BLOCK)
        w1o_ref[0, :, cols] = _dot(w1_ref[0, :, cols].astype(BF16), p).astype(BF16)
    w2o_ref[0] = w2_ref[0].astype(BF16)


def _expert_weight_layout(w1, w2):
    ne, d, f2 = w1.shape
    src = np.concatenate([np.arange(0, SWIGLU_BLOCK, 2), np.arange(1, SWIGLU_BLOCK, 2)])
    perm = np.zeros((SWIGLU_BLOCK, SWIGLU_BLOCK), np.float32)
    perm[src, np.arange(SWIGLU_BLOCK)] = 1.0
    return pl.pallas_call(
        _wprep_kernel,
        out_shape=[jax.ShapeDtypeStruct(w1.shape, BF16), jax.ShapeDtypeStruct(w2.shape, BF16)],
        grid=(ne,),
        in_specs=[
            pl.BlockSpec((1, d, f2), lambda e: (e, 0, 0)),
            pl.BlockSpec((1,) + w2.shape[1:], lambda e: (e, 0, 0)),
            pl.BlockSpec((SWIGLU_BLOCK, SWIGLU_BLOCK), lambda e: (0, 0)),
        ],
        out_specs=[pl.BlockSpec((1, d, f2), lambda e: (e, 0, 0)),
                   pl.BlockSpec((1,) + w2.shape[1:], lambda e: (e, 0, 0))],
        compiler_params=_cparams(("arbitrary",)),
        name="expert_weight_layout",
    )(w1, w2, jnp.asarray(perm, BF16))


def _row_copy_groups(n_tokens, make_copy):
    def group(gidx, carry):
        for u in range(DMA_UNROLL):
            r = gidx * DMA_UNROLL + u
            for kk in range(TOP_K):
                make_copy(r, kk, r * TOP_K + kk).start()
        return carry
    lax.fori_loop(0, n_tokens // DMA_UNROLL, group, 0)


def _dispatch_kernel(base_ref, cnt_ref, nt_ref, e_ref, r_ref, h2_ref, xs_hbm, zrow_ref, zblk_ref, sem, zsem):
    tt = h2_ref.shape[0]

    def row_copy(r, kk, a):
        slot = base_ref[e_ref[a]] + r_ref[a]
        return pltpu.make_async_copy(h2_ref.at[pl.ds(r, 1)], xs_hbm.at[pl.ds(slot, 1)], sem)

    _row_copy_groups(tt, row_copy)
    for _ in range(TOP_K):
        pltpu.make_async_copy(h2_ref, xs_hbm.at[pl.ds(0, tt)], sem).wait()

    @pl.when(pl.program_id(0) == pl.num_programs(0) - 1)
    def _():
        zrow_ref[...] = jnp.zeros_like(zrow_ref)

        def per_expert(e, carry):
            n = cnt_ref[e]
            end = ((n + (MOE_TILE - 1)) // MOE_TILE) * MOE_TILE

            def fill(r, c):
                pltpu.make_async_copy(zrow_ref, xs_hbm.at[pl.ds(base_ref[e] + r, 1)], zsem).start()
                return c

            def drain(r, c):
                pltpu.make_async_copy(zrow_ref, xs_hbm.at[pl.ds(0, 1)], zsem).wait()
                return c

            lax.fori_loop(n, end, fill, 0)
            lax.fori_loop(n, end, drain, 0)
            return carry

        lax.fori_loop(0, N_EXPERTS, per_expert, 0)

        zblk_ref[...] = jnp.zeros_like(zblk_ref)
        n_all = xs_hbm.shape[0] // MOE_TILE

        def tile_copy(ti):
            row0 = pl.multiple_of(ti * MOE_TILE, MOE_TILE)
            return pltpu.make_async_copy(zblk_ref, xs_hbm.at[pl.ds(row0, MOE_TILE)], zsem)

        def fill_tile(ti, c):
            tile_copy(ti).start()
            return c

        def drain_tile(ti, c):
            tile_copy(ti).wait()
            return c

        lax.fori_loop(nt_ref[0], n_all, fill_tile, 0)
        lax.fori_loop(nt_ref[0], n_all, drain_tile, 0)


def _dispatch(h2, e_flat, r_flat, base, counts, n_tiles, n_slots):
    t, d = h2.shape
    tt = TOK_TILE
    smem_blk = lambda: pl.BlockSpec((tt * TOP_K,), lambda i, *_: (i,), memory_space=pltpu.SMEM)
    return pl.pallas_call(
        _dispatch_kernel,
        out_shape=jax.ShapeDtypeStruct((n_slots, d), F32),
        grid_spec=pltpu.PrefetchScalarGridSpec(
            num_scalar_prefetch=3,
            grid=(t // tt,),
            in_specs=[smem_blk(), smem_blk(), pl.BlockSpec((tt, d), lambda i, *_: (i, 0))],
            out_specs=pl.BlockSpec(memory_space=pl.ANY),
            scratch_shapes=[pltpu.VMEM((1, d), F32), pltpu.VMEM((MOE_TILE, d), F32),
                            pltpu.SemaphoreType.DMA, pltpu.SemaphoreType.DMA],
        ),
        compiler_params=_cparams(("arbitrary",)),
        name="moe_dispatch",
    )(base, counts, n_tiles, e_flat, r_flat, h2)


def _ffn_kernel(te_ref, nt_ref, xs_ref, w1_ref, b1_ref, w2_ref, b2_ref, ys_ref):
    @pl.when(pl.program_id(0) < nt_ref[0])
    def _():
        hid = _dot(xs_ref[...].astype(BF16), w1_ref[0]) + b1_ref[0]
        acts = []
        for j in range(hid.shape[1] // SWIGLU_BLOCK):
            glu = jnp.minimum(hid[:, j * SWIGLU_BLOCK:j * SWIGLU_BLOCK + LANES], SWIGLU_LIMIT)
            lin = jnp.clip(hid[:, j * SWIGLU_BLOCK + LANES:(j + 1) * SWIGLU_BLOCK], -SWIGLU_LIMIT, SWIGLU_LIMIT)
            acts.append((glu * jax.nn.sigmoid(SWIGLU_ALPHA * glu) * (lin + 1.0)).astype(BF16))
        ys_ref[...] = _dot(jnp.concatenate(acts, axis=1), w2_ref[0]) + b2_ref[0]

    @pl.when(pl.program_id(0) >= nt_ref[0])
    def _():
        ys_ref[...] = jnp.zeros_like(ys_ref)


def _expert_ffn(xs, tile_expert, n_tiles, w1p, b1p, w2b, b2):
    n_slots, d = xs.shape
    tm = MOE_TILE
    f2 = w1p.shape[2]
    row_blk = lambda i, te, nt: (jnp.minimum(i, nt[0] - 1), 0)
    exp_blk = lambda i, te, nt: (te[i], 0, 0)
    return pl.pallas_call(
        _ffn_kernel,
        out_shape=jax.ShapeDtypeStruct((n_slots, d), F32),
        grid_spec=pltpu.PrefetchScalarGridSpec(
            num_scalar_prefetch=2,
            grid=(n_slots // tm,),
            in_specs=[
                pl.BlockSpec((tm, d), row_blk),
                pl.BlockSpec((1, d, f2), exp_blk),
                pl.BlockSpec((1, 1, f2), exp_blk),
                pl.BlockSpec((1, f2 // 2, d), exp_blk),
                pl.BlockSpec((1, 1, d), exp_blk),
            ],
            out_specs=pl.BlockSpec((tm, d), lambda i, te, nt: (i, 0)),
        ),
        compiler_params=_cparams(("arbitrary",)),
        name="moe_expert_ffn",
    )(tile_expert, n_tiles, xs, w1p, b1p, w2b, b2)


def _combine_kernel(base_ref, e_ref, r_ref, en_ref, rn_ref, w_ref, xm_ref, gate_ref, l2g_ref, l2b_ref,
                    ys_hbm, o_ref, buf_ref, sem):
    i = pl.program_id(0)
    tt = xm_ref.shape[0]
    cur = lax.rem(i, 2)

    def issue(eref, rref, sl):
        def row_copy(r, kk, a):
            src = base_ref[eref[a]] + rref[a]
            return pltpu.make_async_copy(ys_hbm.at[pl.ds(src, 1)], buf_ref.at[sl, kk, pl.ds(r, 1)], sem.at[sl])
        _row_copy_groups(tt, row_copy)

    @pl.when(i == 0)
    def _():
        issue(e_ref, r_ref, 0)

    @pl.when(i + 1 < pl.num_programs(0))
    def _():
        issue(en_ref, rn_ref, 1 - cur)

    for kk in range(TOP_K):
        pltpu.make_async_copy(ys_hbm.at[pl.ds(0, tt)], buf_ref.at[cur, kk], sem.at[cur]).wait()
    w = w_ref[...]
    acc = w[:, 0:1] * buf_ref[cur, 0]
    for kk in range(1, TOP_K):
        acc = acc + w[:, kk:kk + 1] * buf_ref[cur, kk]
    pre = DEEPNORM_ALPHA * xm_ref[...] + gate_ref[0] * acc
    o_ref[...] = _layer_norm(pre) * l2g_ref[...] + l2b_ref[...]


def _combine(ys, e_flat, r_flat, base, sel_w, x_mid, gate2, ln2_g, ln2_b, tiles_per_batch):
    t, d = x_mid.shape
    tt = TOK_TILE
    n = t // tt
    cur_blk = lambda: pl.BlockSpec((tt * TOP_K,), lambda i, *_: (i,), memory_space=pltpu.SMEM)
    nxt_blk = lambda: pl.BlockSpec((tt * TOP_K,), lambda i, *_: (jnp.minimum(i + 1, n - 1),),
                                   memory_space=pltpu.SMEM)
    return pl.pallas_call(
        _combine_kernel,
        out_shape=jax.ShapeDtypeStruct((t, d), F32),
        grid_spec=pltpu.PrefetchScalarGridSpec(
            num_scalar_prefetch=1,
            grid=(n,),
            in_specs=[
                cur_blk(), cur_blk(), nxt_blk(), nxt_blk(),
                pl.BlockSpec((tt, LANES), lambda i, *_: (i, 0)),
                pl.BlockSpec((tt, d), lambda i, *_: (i, 0)),
                pl.BlockSpec((1, 1, d), lambda i, *_: (i // tiles_per_batch, 0, 0)),
                pl.BlockSpec((1, d), lambda i, *_: (0, 0)),
                pl.BlockSpec((1, d), lambda i, *_: (0, 0)),
                pl.BlockSpec(memory_space=pl.ANY),
            ],
            out_specs=pl.BlockSpec((tt, d), lambda i, *_: (i, 0)),
            scratch_shapes=[pltpu.VMEM((2, TOP_K, tt, d), F32), pltpu.SemaphoreType.DMA((2,))],
        ),
        compiler_params=_cparams(("arbitrary",)),
        name="moe_combine",
    )(base, e_flat, r_flat, e_flat, r_flat, sel_w, x_mid, gate2, ln2_g, ln2_b, ys)


def kernel(x, c, ctx, c_ctx, ada_w, ada_b, w_in, gla_wa_f, gla_ba_f, gla_wa_b, gla_ba_b, gla_norm_g,
           hy_conv_w, hy_conv_b, hy_flt_w1, hy_flt_b1, hy_flt_w2, hy_flt_b2, hy_flt_wout, hy_flt_freq,
           hy_bias_d, w_out, ln1_g, ln1_b, router_w, router_b, exp_w1, exp_b1, exp_w2, exp_b2, ln2_g, ln2_b):
    batch, seq_len, d = x.shape
    lyr = 0
    ch = HY_WIDTH

    n_rows = 8 * ((batch + 1 + 7) // 8)
    cvec = jnp.zeros((n_rows, d), F32).at[:batch].set(c).at[batch].set(c_ctx)
    mod = _modulation(cvec, ada_w[lyr], ada_b[lyr][None, :])
    part = lambda rows, i: rows[:, None, i * d:(i + 1) * d]
    mod_x = mod[:batch]
    mod_c = jnp.broadcast_to(mod[batch:batch + 1], (batch, 6 * d))

    w = w_in[lyr]
    o_q, o_k, o_v, o_g = 0, GLA_KEY, 2 * GLA_KEY, 2 * GLA_KEY + GLA_VAL
    o_a = o_g + GLA_VAL
    o_h = o_a + 2 * GLA_RANK
    a_cols = jnp.pad(w[:, o_a:o_h], ((0, 0), (0, A_PAD - 2 * GLA_RANK)))
    q_cols = w[:, o_q:o_k] * (GLA_DK ** -0.5)
    w_main = jnp.concatenate([q_cols, w[:, o_k:o_a], a_cols], axis=1).astype(BF16)
    w_ctx = jnp.concatenate([q_cols, w[:, o_k:o_g], a_cols], axis=1).astype(BF16)
    w_hy_t = w[:, o_h:].T.astype(BF16)
    conv_w = jnp.broadcast_to(hy_conv_w[lyr][:, :, None], (HY_CONV, 3 * ch, LANES))
    conv_b = jnp.broadcast_to(hy_conv_b[lyr][:, None], (3 * ch, LANES))

    ctx_qkva = _input_projection(ctx, part(mod_c, 0), part(mod_c, 1), w_ctx)
    q, k, v, g, a_low, u_t = _input_projection(x, part(mod_x, 0), part(mod_x, 1), w_main, w_hy_t, conv_w, conv_b)

    wa = jnp.zeros((2, A_PAD, GLA_KEY), F32)
    wa = wa.at[0, :GLA_RANK].set(gla_wa_f[lyr]).at[1, GLA_RANK:2 * GLA_RANK].set(gla_wa_b[lyr]).astype(BF16)
    ba = jnp.stack([gla_ba_f[lyr], gla_ba_b[lyr]])[:, None, :]
    o_dirs = _gla_scan(ctx_qkva, (q, k, v, a_low), wa, ba)

    r1 = 2 * seq_len // LANES
    consts = _dft_constants(r1)
    zz, tn_rows, rate = _filter_inputs(seq_len)
    w1p = jnp.pad(hy_flt_w1[lyr], ((0, LANES - HY_EMB), (0, 0)))
    wo_t = hy_flt_wout[lyr].reshape(HY_FH, HY_ORDER, 2, ch).transpose(1, 2, 3, 0)
    filt = _filter_mlp(zz, tn_rows, rate, w1p, hy_flt_b1[lyr][None], hy_flt_w2[lyr], hy_flt_b2[lyr][None],
                       hy_flt_freq[lyr][None], wo_t)
    spectra = _filter_spectra(filt.reshape(HY_ORDER, ch, r1, LANES), consts, r1)
    y_hy = _hyena_conv(u_t.reshape(batch // 2, 2, 3 * ch, r1 // 2, LANES), spectra, hy_bias_d[lyr], consts, r1)
    y_hy_t = y_hy.reshape(batch, ch, seq_len)

    wo = w_out[lyr].astype(BF16)
    wr = jnp.pad(router_w[lyr], ((0, 0), (0, LANES - N_EXPERTS))).astype(BF16)
    rb = jnp.pad(router_b[lyr], (0, LANES - N_EXPERTS), constant_values=NEG_BIG)[None, :]
    norm_g = jnp.tile(gla_norm_g[lyr], GLA_HEADS)[None, :]
    x_mid, h2, sel_e, sel_r, sel_w, cnt = _output_projection(
        o_dirs, g, y_hy_t, x, part(mod_x, 2), part(mod_x, 3), part(mod_x, 4),
        wo[:GLA_VAL], wo[GLA_VAL:], norm_g, ln1_g[lyr][None], ln1_b[lyr][None], wr, rb)

    t = batch * seq_len
    counts = cnt[0, :N_EXPERTS]
    tiles_e = (counts + (MOE_TILE - 1)) // MOE_TILE
    tile_end = jnp.cumsum(tiles_e)
    base = ((tile_end - tiles_e) * MOE_TILE).astype(jnp.int32)
    n_tiles = tile_end[-1:].astype(jnp.int32)
    max_tiles = t * TOP_K // MOE_TILE + N_EXPERTS
    tile_ids = jnp.minimum(jnp.arange(max_tiles, dtype=jnp.int32), n_tiles[0] - 1)
    tile_expert = jnp.sum(tile_ids[:, None] >= tile_end[None, :], axis=1).astype(jnp.int32)
    e_flat = sel_e.reshape(t, LANES)[:, :TOP_K].reshape(t * TOP_K)
    r_flat = sel_r.reshape(t, LANES)[:, :TOP_K].reshape(t * TOP_K)

    w1p_e, w2_e = _expert_weight_layout(exp_w1[lyr], exp_w2[lyr])
    n_blk = 2 * D_EXPERT // SWIGLU_BLOCK
    b1p_e = exp_b1[lyr].reshape(N_EXPERTS, n_blk, LANES, 2).transpose(0, 1, 3, 2).reshape(N_EXPERTS, 1, 2 * D_EXPERT)
    xs = _dispatch(h2.reshape(t, d), e_flat, r_flat, base, counts, n_tiles, max_tiles * MOE_TILE)
    ys = _expert_ffn(xs, tile_expert, n_tiles, w1p_e, b1p_e, w2_e, exp_b2[lyr][:, None, :])
    out = _combine(ys, e_flat, r_flat, base, sel_w.reshape(t, LANES), x_mid.reshape(t, d), part(mod_x, 5),
                   ln2_g[lyr][None], ln2_b[lyr][None], seq_len // TOK_TILE)
    return out.reshape(batch, seq_len, d)
```

```python
import functools
import math

import numpy as np
import jax
import jax.numpy as jnp
from jax import lax
from jax.experimental import pallas as pl
from jax.experimental.pallas import tpu as pltpu

F32 = jnp.float32
BF16 = jnp.bfloat16

D_MODEL = 1024
DEPTH = 1
GRID_W = 64
LN_EPS = 1e-6
DEEPNORM_ALPHA = (2 * DEPTH) ** 0.25
GLA_HEADS = 4
GLA_DK = 64
GLA_DV = 128
GLA_KEY = GLA_HEADS * GLA_DK
GLA_VAL = GLA_HEADS * GLA_DV
GLA_RANK = 16
GLA_TAU = 16.0
HY_WIDTH = D_MODEL - GLA_VAL
HY_ORDER = 2
HY_CONV = 3
HY_EMB = 33
HY_FH = 64
HY_TARGET = 1e-2
HY_FAST = 0.3
HY_SLOW = 1.5
N_EXPERTS = 32
TOP_K = 4
D_EXPERT = D_MODEL
SWIGLU_ALPHA = 1.702
SWIGLU_LIMIT = 7.0

LANES = 128
VMEM_LIMIT = 56 * 1024 * 1024

TOK_TILE = 512
GLA_CHUNK = 128
GLA_BLOCK = 256
GLA_BATCH = 2
A_PAD = LANES
HY_GROUP = 8
HY_CH_BLOCK = 8
FILT_TILE = 512
MOE_TILE = 512
DISPATCH_TILE = 1024
COMBINE_TILE = 256
DMA_UNROLL = 8
ROW_TILES = D_MODEL // LANES


def _row_slab(n_rows, j):
    return pl.ds(j, n_rows, stride=ROW_TILES)


def _row_tile(r):
    return pl.ds(pl.multiple_of(r * ROW_TILES, ROW_TILES), ROW_TILES)
SWIGLU_BLOCK = 2 * LANES
NEG_BIG = -1e30


def _cparams(sem):
    return pltpu.CompilerParams(dimension_semantics=sem, vmem_limit_bytes=VMEM_LIMIT)


def _layer_norm(x):
    mu = jnp.mean(x, axis=-1, keepdims=True)
    xc = x - mu
    return xc * lax.rsqrt(jnp.mean(xc * xc, axis=-1, keepdims=True) + LN_EPS)


def _dot(a, b):
    return jnp.dot(a, b, preferred_element_type=F32)


def _dot_nt(a, b):
    return lax.dot_general(a, b, (((1,), (1,)), ((), ())), preferred_element_type=F32)


def _dot_tn(a, b):
    return lax.dot_general(a, b, (((0,), (0,)), ((), ())), preferred_element_type=F32)


def _mod_kernel(c_ref, w_ref, b_ref, o_ref):
    c = c_ref[...]
    s = c * jax.nn.sigmoid(c)
    o_ref[...] = _dot(s.astype(BF16), w_ref[...].astype(BF16)) + b_ref[...]


def _modulation(cvec, ada_w, ada_b):
    rows, d = cvec.shape
    n = ada_w.shape[1]
    tn = 1024
    return pl.pallas_call(
        _mod_kernel,
        out_shape=jax.ShapeDtypeStruct((rows, n), F32),
        grid=(n // tn,),
        in_specs=[
            pl.BlockSpec((rows, d), lambda j: (0, 0)),
            pl.BlockSpec((d, tn), lambda j: (0, j)),
            pl.BlockSpec((1, tn), lambda j: (0, j)),
        ],
        out_specs=pl.BlockSpec((rows, tn), lambda j: (0, j)),
        compiler_params=_cparams(("arbitrary",)),
        name="adaln_mod",
    )(cvec, ada_w, ada_b)


def _inproj_kernel(x_ref, sh_ref, sc_ref, wm_ref, *rest, with_hy):
    if with_hy:
        wh_ref, cw_ref, cb_ref, q_ref, k_ref, v_ref, g_ref, a_ref, u_ref = rest
    else:
        q_ref, k_ref, v_ref, a_ref = rest
    h = _layer_norm(x_ref[0]) * (1.0 + sc_ref[0]) + sh_ref[0]
    hb = h.astype(BF16)
    u = _dot(hb, wm_ref[...])
    q_ref[0] = u[:, :GLA_KEY].astype(BF16)
    k_ref[0] = u[:, GLA_KEY:2 * GLA_KEY].astype(BF16)
    v_ref[0] = u[:, 2 * GLA_KEY:2 * GLA_KEY + GLA_VAL].astype(BF16)
    off = 2 * GLA_KEY + GLA_VAL
    if with_hy:
        g_ref[0] = u[:, off:off + GLA_VAL].astype(BF16)
        off += GLA_VAL
    a_ref[0] = u[:, off:off + A_PAD]
    if with_hy:
        ut = _dot_nt(wh_ref[...], hb)
        lane = lax.broadcasted_iota(jnp.int32, (1, LANES), 1) % GRID_W
        not_first = (lane != 0).astype(F32)
        not_last = (lane != GRID_W - 1).astype(F32)
        tt = ut.shape[1]
        for j in range(tt // LANES):
            c = ut[:, j * LANES:(j + 1) * LANES]
            left = pltpu.roll(c, 1, axis=1) * not_first
            right = pltpu.roll(c, LANES - 1, axis=1) * not_last
            y = cw_ref[0] * left + cw_ref[1] * c + cw_ref[2] * right + cb_ref[...]
            u_ref[0, :, j * LANES:(j + 1) * LANES] = y.astype(BF16)


def _input_projection(x, shift, scale, w_main, w_hy_t=None, conv_w=None, conv_b=None):
    b, l, d = x.shape
    tt = min(TOK_TILE, l)
    with_hy = w_hy_t is not None
    nm = w_main.shape[1]
    tok = lambda width, dt: jax.ShapeDtypeStruct((b, l, width), dt)
    tok_spec = lambda width: pl.BlockSpec((1, tt, width), lambda i, j: (i, j, 0))
    in_specs = [
        pl.BlockSpec((1, tt, d), lambda i, j: (i, j, 0)),
        pl.BlockSpec((1, 1, d), lambda i, j: (i, 0, 0)),
        pl.BlockSpec((1, 1, d), lambda i, j: (i, 0, 0)),
        pl.BlockSpec((d, nm), lambda i, j: (0, 0)),
    ]
    args = [x, shift, scale, w_main]
    if with_hy:
        ch = w_hy_t.shape[0]
        in_specs += [
            pl.BlockSpec((ch, d), lambda i, j: (0, 0)),
            pl.BlockSpec((HY_CONV, ch, LANES), lambda i, j: (0, 0, 0)),
            pl.BlockSpec((ch, LANES), lambda i, j: (0, 0)),
        ]
        args += [w_hy_t, conv_w, conv_b]
        out_shape = [tok(GLA_KEY, BF16), tok(GLA_KEY, BF16), tok(GLA_VAL, BF16), tok(GLA_VAL, BF16),
                     tok(A_PAD, F32), jax.ShapeDtypeStruct((b, ch, l), BF16)]
        out_specs = [tok_spec(GLA_KEY), tok_spec(GLA_KEY), tok_spec(GLA_VAL), tok_spec(GLA_VAL),
                     tok_spec(A_PAD), pl.BlockSpec((1, ch, tt), lambda i, j: (i, 0, j))]
    else:
        out_shape = [tok(GLA_KEY, BF16), tok(GLA_KEY, BF16), tok(GLA_VAL, BF16), tok(A_PAD, F32)]
        out_specs = [tok_spec(GLA_KEY), tok_spec(GLA_KEY), tok_spec(GLA_VAL), tok_spec(A_PAD)]
    return pl.pallas_call(
        functools.partial(_inproj_kernel, with_hy=with_hy),
        out_shape=out_shape,
        grid=(b, l // tt),
        in_specs=in_specs,
        out_specs=out_specs,
        compiler_params=_cparams(("parallel", "arbitrary")),
        name="in_proj_hy" if with_hy else "in_proj_ctx",
    )(*args)


def _gla_kernel(*refs, n_ctx_blocks, chunk, n_batch):
    ins = [refs[0:8], refs[8:16]]
    wa_ref, ba_ref, cw_ref, cm_ref, hm_ref, bd_ref = refs[16:22]
    outs = refs[22:24]
    st_ref = refs[24]
    s = pl.program_id(1)
    blk = ins[0][4].shape[1]
    n_chunks = blk // chunk

    @pl.when(s == 0)
    def _():
        st_ref[...] = jnp.zeros_like(st_ref)

    is_ctx = s < n_ctx_blocks
    bdmask = bd_ref[...]
    for d in range(2):
        qc_ref, kc_ref, vc_ref, ac_ref, ql_ref, kl_ref, vl_ref, al_ref = ins[d]
        cmask = cm_ref[d]
        for bi in range(n_batch):
            q = jnp.where(is_ctx, qc_ref[bi], ql_ref[bi]).astype(F32)
            k = jnp.where(is_ctx, kc_ref[bi], kl_ref[bi]).astype(F32)
            v = jnp.where(is_ctx, vc_ref[bi], vl_ref[bi])
            a = jnp.where(is_ctx, ac_ref[bi], al_ref[bi])
            z = _dot(a.astype(BF16), wa_ref[d]) + ba_ref[d]
            g = (jnp.minimum(z, 0.0) - jnp.log(1.0 + jnp.exp(-jnp.abs(z)))) * (1.0 / GLA_TAU)
            g_hi = g.astype(BF16)
            g_lo = (g - g_hi.astype(F32)).astype(BF16)
            ct = _dot(cw_ref[d], g_hi) + _dot(cw_ref[d], g_lo)
            cum, tot = ct[:blk], ct[blk:]
            qs_all = (q * jnp.exp(cum)).astype(BF16)
            ks_all = (k * jnp.exp(-cum)).astype(BF16)
            kst_all = (k * jnp.exp(tot - cum)).astype(BF16)
            dec_all = jnp.exp(tot)
            for n in range(n_chunks):
                ci = n if d == 0 else n_chunks - 1 - n
                rows = slice(ci * chunk, (ci + 1) * chunk)
                qs, ks, kst, vv = qs_all[rows], ks_all[rows], kst_all[rows], v[rows]
                o = _dot_nt(qs, st_ref[d, bi].astype(BF16))
                parts = []
                for h in range(GLA_HEADS):
                    a_h = _dot_nt(qs * hm_ref[h], ks) * cmask
                    parts.append(_dot(a_h.astype(BF16), vv[:, h * GLA_DV:(h + 1) * GLA_DV]))
                o = o + jnp.concatenate(parts, axis=1)
                outs[d][bi, rows, :] = o.astype(outs[d].dtype)
                st_ref[d, bi] = st_ref[d, bi] * dec_all[ci * chunk:ci * chunk + 1] + _dot_tn(vv, kst) * bdmask


def _gla_scan(ctx_qkva, lat_qkva, wa, ba, chunk=GLA_CHUNK, blk=GLA_BLOCK):
    qc, kc, vc, ac = ctx_qkva
    ql, kl, vl, al = lat_qkva
    b, l, _ = ql.shape
    ctx_len = qc.shape[1]
    n_ctx = ctx_len // blk
    n_lat = l // blk
    c = chunk
    idx = np.arange(c)
    tri = np.stack([idx[:, None] >= idx[None, :], idx[:, None] <= idx[None, :]]).astype(np.float32)
    same_chunk = (np.arange(blk)[:, None] // c == np.arange(blk)[None, :] // c).astype(np.float32)
    cw = np.stack([np.concatenate([np.kron(np.eye(blk // c, dtype=np.float32), tri[dd]), same_chunk])
                   for dd in range(2)])
    hmask = np.zeros((GLA_HEADS, 1, GLA_KEY), np.float32)
    for h in range(GLA_HEADS):
        hmask[h, 0, h * GLA_DK:(h + 1) * GLA_DK] = 1.0
    bd = (np.arange(GLA_VAL)[:, None] // GLA_DV == np.arange(GLA_KEY)[None, :] // GLA_DK).astype(np.float32)

    nb = GLA_BATCH if b % GLA_BATCH == 0 else 1

    def lat_block(d):
        def f(i, s):
            t = jnp.maximum(s - n_ctx, 0)
            return (i, t if d == 0 else n_lat - 1 - t, 0)
        return f

    def ctx_block(d):
        def f(i, s):
            t = jnp.minimum(s, n_ctx - 1)
            return (i, t if d == 0 else n_ctx - 1 - t, 0)
        return f

    widths = (GLA_KEY, GLA_KEY, GLA_VAL, A_PAD)
    dir_specs = lambda d: ([pl.BlockSpec((nb, blk, w), ctx_block(d)) for w in widths]
                           + [pl.BlockSpec((nb, blk, w), lat_block(d)) for w in widths])
    consts = [wa, ba, jnp.asarray(cw, BF16), jnp.asarray(tri, F32), jnp.asarray(hmask, BF16), jnp.asarray(bd, F32)]
    full = lambda arr: pl.BlockSpec(arr.shape, lambda i, s: (0,) * arr.ndim)
    o_sds = jax.ShapeDtypeStruct((b, l, GLA_VAL), BF16)
    dir_args = [qc, kc, vc, ac, ql, kl, vl, al]
    return pl.pallas_call(
        functools.partial(_gla_kernel, n_ctx_blocks=n_ctx, chunk=c, n_batch=nb),
        out_shape=[o_sds, o_sds],
        grid=(b // nb, n_ctx + n_lat),
        in_specs=dir_specs(0) + dir_specs(1) + [full(arr) for arr in consts],
        out_specs=[pl.BlockSpec((nb, blk, GLA_VAL), lat_block(0)), pl.BlockSpec((nb, blk, GLA_VAL), lat_block(1))],
        scratch_shapes=[pltpu.VMEM((2, nb, GLA_VAL, GLA_KEY), F32)],
        compiler_params=_cparams(("parallel", "arbitrary")),
        name="gla_scan",
    )(*dir_args, *dir_args, *consts)


def _dft_constants(r1):
    n = r1 * LANES
    h = r1 // 2
    k1 = np.arange(r1)
    f1 = np.exp(-2j * np.pi * np.outer(k1, k1) / r1)
    f2 = np.exp(-2j * np.pi * np.outer(np.arange(LANES), np.arange(LANES)) / LANES)
    tw = np.exp(-2j * np.pi * np.outer(k1, np.arange(LANES)) / n)
    fa_c = np.block([[f1.real[:, :h], -f1.imag[:, :h]], [f1.imag[:, :h], f1.real[:, :h]]])
    fa_r = np.concatenate([f1.real, f1.imag], axis=0)
    gc = np.block([[f2.real, f2.imag], [-f2.imag, f2.real]])
    gci = np.block([[f2.real, -f2.imag], [f2.imag, f2.real]])
    fai = np.block([[f1.real[:h], f1.imag[:h]], [-f1.imag[:h], f1.real[:h]]]) / n
    tw_lane = np.tile(tw, (1, HY_GROUP))
    tw_row = np.tile(tw, (HY_GROUP, 1))
    f = lambda a: jnp.asarray(a, F32)
    return dict(fa_c=f(fa_c), fa_r=f(fa_r), gc=f(gc), gci=f(gci), fai=f(fai),
                twl_r=f(tw_lane.real), twl_i=f(tw_lane.imag), twr_r=f(tw_row.real), twr_i=f(tw_row.imag))


def _fwd_dft(rhs, fa, gc, twl_r, twl_i, r1):
    a = _dot(fa, rhs)
    ar, ai = a[:r1], a[r1:]
    br = (ar * twl_r - ai * twl_i).astype(BF16)
    bi = (ar * twl_i + ai * twl_r).astype(BF16)
    lhs = jnp.concatenate(
        [jnp.concatenate([br[:, c * LANES:(c + 1) * LANES], bi[:, c * LANES:(c + 1) * LANES]], axis=1)
         for c in range(HY_GROUP)], axis=0)
    return _dot(lhs, gc)


def _inv_dft(yr, yi, gci, fai, twr_r, twr_i, r1):
    lhs = jnp.concatenate([yr, yi], axis=1).astype(BF16)
    c = _dot(lhs, gci)
    cr, ci = c[:, :LANES], c[:, LANES:]
    dr = (cr * twr_r + ci * twr_i).astype(BF16)
    di = (ci * twr_r - cr * twr_i).astype(BF16)
    rhs = jnp.concatenate(
        [jnp.concatenate([dr[g * r1:(g + 1) * r1], di[g * r1:(g + 1) * r1]], axis=0)
         for g in range(HY_GROUP)], axis=1)
    return _dot(fai, rhs)


def _filter_mlp_kernel(zz_ref, tn_ref, rate_ref, w1_ref, b1_ref, w2_ref, b2_ref, fr_ref, wo_ref, o_ref):
    hp = lax.Precision.HIGHEST
    fr = fr_ref[...]
    hid = jnp.sin(fr * (jnp.dot(zz_ref[...], w1_ref[...], precision=hp, preferred_element_type=F32) + b1_ref[...]))
    hid = jnp.sin(fr * (jnp.dot(hid, w2_ref[...], precision=hp, preferred_element_type=F32) + b2_ref[...]))
    lt = tn_ref.shape[1]
    rate = jnp.concatenate([rate_ref[...]] * (lt // LANES), axis=1)
    window = jnp.exp(-tn_ref[0:1, :] * rate) * tn_ref[1:2, :]
    for o in range(HY_ORDER):
        ht = lax.dot_general(wo_ref[o, 0], hid, (((1,), (1,)), ((), ())), precision=hp,
                             preferred_element_type=F32)
        o_ref[o] = ht * window


def _filter_mlp(zz, tn_rows, rate, w1p, b1, w2, b2, freq, wo_t):
    n2l = zz.shape[0]
    l = n2l // 2
    lt = min(FILT_TILE, l)
    nt = l // lt
    ch = rate.shape[0]
    full = lambda shape: pl.BlockSpec(shape, lambda d, j: (0,) * len(shape))
    return pl.pallas_call(
        _filter_mlp_kernel,
        out_shape=jax.ShapeDtypeStruct((HY_ORDER, ch, n2l), F32),
        grid=(2, nt),
        in_specs=[
            pl.BlockSpec((lt, LANES), lambda d, j: (d * nt + j, 0)),
            pl.BlockSpec((8, lt), lambda d, j: (0, d * nt + j)),
            full((ch, LANES)),
            full((LANES, HY_FH)), full((1, HY_FH)), full((HY_FH, HY_FH)), full((1, HY_FH)), full((1, HY_FH)),
            pl.BlockSpec((HY_ORDER, 1, ch, HY_FH), lambda d, j: (0, d, 0, 0)),
        ],
        out_specs=pl.BlockSpec((HY_ORDER, ch, lt), lambda d, j: (0, 0, d * nt + j)),
        compiler_params=_cparams(("arbitrary", "arbitrary")),
        name="hyena_filter_mlp",
    )(zz, tn_rows, rate, w1p, b1, w2, b2, freq, wo_t)


def _filter_fft_kernel(f_ref, fa_ref, gc_ref, twl_r_ref, twl_i_ref, h_ref, *, r1):
    fa = fa_ref[...].astype(BF16)
    gc = gc_ref[...].astype(BF16)
    nc = f_ref.shape[1]
    for g0 in range(0, nc, HY_GROUP):
        rhs = jnp.concatenate([f_ref[0, g0 + c].astype(BF16) for c in range(HY_GROUP)], axis=1)
        x = _fwd_dft(rhs, fa, gc, twl_r_ref[...], twl_i_ref[...], r1)
        for c in range(HY_GROUP):
            h_ref[0, g0 + c] = x[c * r1:(c + 1) * r1]


def _filter_spectra(filt, consts, r1):
    order, ch = filt.shape[:2]
    nc = HY_CH_BLOCK
    full = lambda a: pl.BlockSpec(a.shape, lambda o, j: (0,) * a.ndim)
    cs = [consts["fa_r"], consts["gc"], consts["twl_r"], consts["twl_i"]]
    return pl.pallas_call(
        functools.partial(_filter_fft_kernel, r1=r1),
        out_shape=jax.ShapeDtypeStruct((order, ch, r1, 2 * LANES), F32),
        grid=(order, ch // nc),
        in_specs=[pl.BlockSpec((1, nc, r1, LANES), lambda o, j: (o, j, 0, 0))] + [full(a) for a in cs],
        out_specs=pl.BlockSpec((1, nc, r1, 2 * LANES), lambda o, j: (o, j, 0, 0)),
        compiler_params=_cparams(("arbitrary", "arbitrary")),
        name="hyena_filter_fft",
    )(filt, *cs)


def _hyena_kernel(dbias_ref, v_ref, x1_ref, x2_ref, h_ref, fa_ref, gc_ref, gci_ref, fai_ref,
                  twl_r_ref, twl_i_ref, twr_r_ref, twr_i_ref, y_ref, *, r1):
    fa = fa_ref[...].astype(BF16)
    gc = gc_ref[...].astype(BF16)
    gci = gci_ref[...].astype(BF16)
    fai = fai_ref[...].astype(BF16)
    twl_r, twl_i = twl_r_ref[...], twl_i_ref[...]
    twr_r, twr_i = twr_r_ref[...], twr_i_ref[...]
    nc = v_ref.shape[2]
    half = r1 // 2
    c_base = pl.program_id(0) * nc

    def conv(sig, order, g0):
        rhs = jnp.concatenate(
            [jnp.concatenate([sig[c][0].astype(BF16), sig[c][1].astype(BF16)], axis=0)
             for c in range(HY_GROUP)], axis=1)
        x = _fwd_dft(rhs, fa, gc, twl_r, twl_i, r1)
        xr, xi = x[:, :LANES], x[:, LANES:]
        hh = jnp.concatenate([h_ref[order, g0 + c] for c in range(HY_GROUP)], axis=0)
        hr, hi = hh[:, :LANES], hh[:, LANES:]
        y = _inv_dft(xr * hr - xi * hi, xr * hi + xi * hr, gci, fai, twr_r, twr_i, r1)
        out = []
        for c in range(HY_GROUP):
            dcoef = dbias_ref[order, c_base + g0 + c]
            yc = y[:, c * LANES:(c + 1) * LANES]
            out.append([yc[:half] + dcoef * sig[c][0], yc[half:] + dcoef * sig[c][1]])
        return out

    for g0 in range(0, nc, HY_GROUP):
        v = [[v_ref[0, b, g0 + c].astype(F32) for b in range(2)] for c in range(HY_GROUP)]
        y1 = conv(v, 0, g0)
        z = [[x1_ref[0, b, g0 + c].astype(F32) * y1[c][b] for b in range(2)] for c in range(HY_GROUP)]
        y2 = conv(z, 1, g0)
        for c in range(HY_GROUP):
            for b in range(2):
                y_ref[0, b, g0 + c] = (x2_ref[0, b, g0 + c].astype(F32) * y2[c][b]).astype(y_ref.dtype)


def _hyena_conv(u_t, spectra, d_bias, consts, r1):
    bp, _, ch3, half, _ = u_t.shape
    ch = ch3 // 3
    nc = HY_CH_BLOCK
    nblk = ch // nc
    names = ["fa_c", "gc", "gci", "fai", "twl_r", "twl_i", "twr_r", "twr_i"]
    cs = [consts[k] for k in names]
    full = lambda a: pl.BlockSpec(a.shape, lambda j, p: (0,) * a.ndim)
    part = lambda k: pl.BlockSpec((1, 2, nc, half, LANES), lambda j, p: (p, 0, k * nblk + j, 0, 0))
    return pl.pallas_call(
        functools.partial(_hyena_kernel, r1=r1),
        out_shape=jax.ShapeDtypeStruct((bp, 2, ch, half, LANES), BF16),
        grid=(nblk, bp),
        in_specs=[pl.BlockSpec(memory_space=pltpu.SMEM), part(0), part(1), part(2),
                  pl.BlockSpec((HY_ORDER, nc, r1, 2 * LANES), lambda j, p: (0, j, 0, 0))] + [full(a) for a in cs],
        out_specs=pl.BlockSpec((1, 2, nc, half, LANES), lambda j, p: (p, 0, j, 0, 0)),
        compiler_params=_cparams(("arbitrary", "arbitrary")),
        name="hyena_conv",
    )(d_bias, u_t, u_t, u_t, spectra, *cs)


def _filter_inputs(l):
    n = jnp.arange(2 * l, dtype=jnp.int32)
    t = jnp.where(n < l, n, 2 * l - n).astype(F32)
    valid = (n != l).astype(F32)
    t_norm = t / (l - 1)
    bands = (HY_EMB - 1) // 2
    f = jnp.linspace(1e-4, bands - 1, bands, dtype=F32)
    ang = (2.0 * math.pi * t / l)[:, None] * f[None, :]
    z = jnp.concatenate([t_norm[:, None], jnp.cos(ang), -jnp.sin(ang)], -1)
    zz = jnp.pad(z, ((0, 0), (0, LANES - HY_EMB)))
    tn_rows = jnp.zeros((8, 2 * l), F32).at[0].set(t_norm).at[1].set(valid)
    deltas = jnp.linspace(math.log(HY_TARGET) / HY_SLOW, math.log(HY_TARGET) / HY_FAST, HY_WIDTH, dtype=F32)
    rate = jnp.broadcast_to(jnp.abs(deltas)[:, None], (HY_WIDTH, LANES))
    return zz, tn_rows, rate


def _outproj_kernel(of_ref, ob_ref, g_ref, yh_ref, x_ref, gate_ref, sh_ref, sc_ref,
                    wg_ref, wh_ref, ng_ref, l1g_ref, l1b_ref, wr_ref, rb_ref, ltri_ref,
                    xm_ref, h2_ref, se_ref, sr_ref, sw_ref, cnt_ref, carry_ref):
    @pl.when((pl.program_id(0) == 0) & (pl.program_id(1) == 0))
    def _():
        carry_ref[...] = jnp.zeros_like(carry_ref)

    o = of_ref[0].astype(F32) + ob_ref[0].astype(F32)
    g = g_ref[0].astype(F32)
    parts = []
    for h in range(GLA_HEADS):
        oh = o[:, h * GLA_DV:(h + 1) * GLA_DV]
        parts.append(oh * lax.rsqrt(jnp.mean(oh * oh, axis=-1, keepdims=True) + LN_EPS))
    y_gla = jnp.concatenate(parts, axis=1) * ng_ref[...] * (g * jax.nn.sigmoid(g))
    f = _dot(y_gla.astype(BF16), wg_ref[...]) + _dot_tn(yh_ref[0], wh_ref[...])
    x_mid = _layer_norm(DEEPNORM_ALPHA * x_ref[0] + gate_ref[0] * f) * l1g_ref[...] + l1b_ref[...]
    xm_ref[0] = x_mid
    h2 = _layer_norm(x_mid) * (1.0 + sc_ref[0]) + sh_ref[0]
    for j in range(ROW_TILES):
        h2_ref[0, _row_slab(h2.shape[0], j), :] = h2[:, j * LANES:(j + 1) * LANES]

    logits = _dot(h2.astype(BF16), wr_ref[...]) + rb_ref[...]
    lane = lax.broadcasted_iota(jnp.int32, logits.shape, 1).astype(F32)
    hits, idxs, exps = [], [], []
    m0 = None
    for _ in range(TOP_K):
        m = jnp.max(logits, axis=-1, keepdims=True)
        idx = jnp.min(jnp.where(logits == m, lane, float(LANES)), axis=-1, keepdims=True)
        hit = lane == idx
        m0 = m if m0 is None else m0
        hits.append(hit)
        idxs.append(idx)
        exps.append(jnp.exp(m - m0))
        logits = jnp.where(hit, NEG_BIG, logits)
    denom = exps[0]
    sel = jnp.where(hits[0], 1.0, 0.0)
    for kk in range(1, TOP_K):
        denom = denom + exps[kk]
        sel = sel + jnp.where(hits[kk], 1.0, 0.0)
    rank_all = _dot(ltri_ref[...], sel.astype(BF16)) + carry_ref[0:1, :]
    carry_ref[0:1, :] = carry_ref[0:1, :] + jnp.sum(sel, axis=0, keepdims=True)
    cnt_ref[...] = jnp.broadcast_to(carry_ref[0:1, :], cnt_ref.shape).astype(jnp.int32)
    se = jnp.zeros(logits.shape, F32)
    sr = jnp.zeros(logits.shape, F32)
    sw = jnp.zeros(logits.shape, F32)
    for kk in range(TOP_K):
        rk = jnp.sum(jnp.where(hits[kk], rank_all, 0.0), axis=-1, keepdims=True)
        col = lane == float(kk)
        se = jnp.where(col, idxs[kk], se)
        sr = jnp.where(col, rk, sr)
        sw = jnp.where(col, exps[kk] / denom, sw)
    se_ref[0] = se.astype(jnp.int32)
    sr_ref[0] = sr.astype(jnp.int32)
    sw_ref[0] = sw


def _output_projection(o_f, o_b, g, y_hy_t, x, gate1, shift2, scale2, w_gla, w_hy, norm_g, ln1_g, ln1_b, wr, rb):
    b, l, d = x.shape
    tt = TOK_TILE
    tok = lambda w: pl.BlockSpec((1, tt, w), lambda i, j: (i, j, 0))
    row = lambda: pl.BlockSpec((1, 1, d), lambda i, j: (i, 0, 0))
    full = lambda a: pl.BlockSpec(a.shape, lambda i, j: (0,) * a.ndim)
    ltri = jnp.asarray(np.tril(np.ones((tt, tt), np.float32), -1), BF16)
    consts = [w_gla, w_hy, norm_g, ln1_g, ln1_b, wr, rb, ltri]
    lane_i = jax.ShapeDtypeStruct((b, l, LANES), jnp.int32)
    return pl.pallas_call(
        _outproj_kernel,
        out_shape=[jax.ShapeDtypeStruct((b, l, d), F32), jax.ShapeDtypeStruct((b, l * ROW_TILES, LANES), F32),
                   lane_i, lane_i, jax.ShapeDtypeStruct((b, l, LANES), F32),
                   jax.ShapeDtypeStruct((8, LANES), jnp.int32)],
        grid=(b, l // tt),
        in_specs=[
            tok(GLA_VAL), tok(GLA_VAL), tok(GLA_VAL),
            pl.BlockSpec((1, HY_WIDTH, tt), lambda i, j: (i, 0, j)),
            tok(d), row(), row(), row(),
        ] + [full(a) for a in consts],
        out_specs=[tok(d), pl.BlockSpec((1, tt * ROW_TILES, LANES), lambda i, j: (i, j, 0)),
                   tok(LANES), tok(LANES), tok(LANES),
                   pl.BlockSpec((8, LANES), lambda i, j: (0, 0))],
        scratch_shapes=[pltpu.VMEM((8, LANES), F32)],
        compiler_params=_cparams(("arbitrary", "arbitrary")),
        name="out_proj_router",
    )(o_f, o_b, g, y_hy_t, x, gate1, shift2, scale2, *consts)


def _wprep_kernel(w1_ref, w2_ref, p_ref, w1o_ref, w2o_ref):
    p = p_ref[...]
    for j in range(w1_ref.shape[2] // SWIGLU_BLOCK):
        cols = slice(j * SWIGLU_BLOCK, (j + 1) * SWIGLU_BLOCK)
        w1o_ref[0, :, cols] = _dot(w1_ref[0, :, cols].astype(BF16), p).astype(BF16)
    w2o_ref[0] = w2_ref[0].astype(BF16)


def _expert_weight_layout(w1, w2):
    ne, d, f2 = w1.shape
    src = np.concatenate([np.arange(0, SWIGLU_BLOCK, 2), np.arange(1, SWIGLU_BLOCK, 2)])
    perm = np.zeros((SWIGLU_BLOCK, SWIGLU_BLOCK), np.float32)
    perm[src, np.arange(SWIGLU_BLOCK)] = 1.0
    return pl.pallas_call(
        _wprep_kernel,
        out_shape=[jax.ShapeDtypeStruct(w1.shape, BF16), jax.ShapeDtypeStruct(w2.shape, BF16)],
        grid=(ne,),
        in_specs=[
            pl.BlockSpec((1, d, f2), lambda e: (e, 0, 0)),
            pl.BlockSpec((1,) + w2.shape[1:], lambda e: (e, 0, 0)),
            pl.BlockSpec((SWIGLU_BLOCK, SWIGLU_BLOCK), lambda e: (0, 0)),
        ],
        out_specs=[pl.BlockSpec((1, d, f2), lambda e: (e, 0, 0)),
                   pl.BlockSpec((1,) + w2.shape[1:], lambda e: (e, 0, 0))],
        compiler_params=_cparams(("arbitrary",)),
        name="expert_weight_layout",
    )(w1, w2, jnp.asarray(perm, BF16))


def _row_copy_groups(n_tokens, make_copy):
    def group(gidx, carry):
        for u in range(DMA_UNROLL):
            r = gidx * DMA_UNROLL + u
            for kk in range(TOP_K):
                make_copy(r, kk, r * TOP_K + kk).start(priority=kk % 2)
        return carry
    lax.fori_loop(0, n_tokens // DMA_UNROLL, group, 0)


def _dispatch_kernel(base_ref, cnt_ref, nt_ref, e_ref, r_ref, h2_ref, xs_hbm, zrow_ref, zblk_ref, sem, zsem):
    tt = h2_ref.shape[0] // ROW_TILES

    def row_copy(r, kk, a):
        slot = base_ref[e_ref[a]] + r_ref[a]
        return pltpu.make_async_copy(h2_ref.at[_row_tile(r)], xs_hbm.at[_row_tile(slot)], sem)

    _row_copy_groups(tt, row_copy)
    for _ in range(TOP_K):
        pltpu.make_async_copy(h2_ref, xs_hbm.at[pl.ds(0, tt * ROW_TILES)], sem).wait()

    @pl.when(pl.program_id(0) == pl.num_programs(0) - 1)
    def _():
        zrow_ref[...] = jnp.zeros_like(zrow_ref)

        def per_expert(e, carry):
            n = cnt_ref[e]
            end = ((n + (MOE_TILE - 1)) // MOE_TILE) * MOE_TILE

            def fill(r, c):
                pltpu.make_async_copy(zrow_ref, xs_hbm.at[_row_tile(base_ref[e] + r)], zsem).start()
                return c

            def drain(r, c):
                pltpu.make_async_copy(zrow_ref, xs_hbm.at[_row_tile(0)], zsem).wait()
                return c

            lax.fori_loop(n, end, fill, 0)
            lax.fori_loop(n, end, drain, 0)
            return carry

        lax.fori_loop(0, N_EXPERTS, per_expert, 0)

        zblk_ref[...] = jnp.zeros_like(zblk_ref)
        tile_rows = MOE_TILE * ROW_TILES
        n_all = xs_hbm.shape[0] // tile_rows

        def tile_copy(ti):
            row0 = pl.multiple_of(ti * tile_rows, tile_rows)
            return pltpu.make_async_copy(zblk_ref, xs_hbm.at[pl.ds(row0, tile_rows)], zsem)

        def fill_tile(ti, c):
            tile_copy(ti).start()
            return c

        def drain_tile(ti, c):
            tile_copy(ti).wait()
            return c

        lax.fori_loop(nt_ref[0], n_all, fill_tile, 0)
        lax.fori_loop(nt_ref[0], n_all, drain_tile, 0)


def _dispatch(h2, e_flat, r_flat, base, counts, n_tiles, n_slots):
    t = h2.shape[0] // ROW_TILES
    tt = min(DISPATCH_TILE, t)
    smem_blk = lambda: pl.BlockSpec((tt * TOP_K,), lambda i, *_: (i,), memory_space=pltpu.SMEM)
    return pl.pallas_call(
        _dispatch_kernel,
        out_shape=jax.ShapeDtypeStruct((n_slots * ROW_TILES, LANES), F32),
        grid_spec=pltpu.PrefetchScalarGridSpec(
            num_scalar_prefetch=3,
            grid=(t // tt,),
            in_specs=[smem_blk(), smem_blk(), pl.BlockSpec((tt * ROW_TILES, LANES), lambda i, *_: (i, 0))],
            out_specs=pl.BlockSpec(memory_space=pl.ANY),
            scratch_shapes=[pltpu.VMEM((ROW_TILES, LANES), F32), pltpu.VMEM((MOE_TILE * ROW_TILES, LANES), F32),
                            pltpu.SemaphoreType.DMA, pltpu.SemaphoreType.DMA],
        ),
        compiler_params=_cparams(("arbitrary",)),
        name="moe_dispatch",
    )(base, counts, n_tiles, e_flat, r_flat, h2)


def _ffn_kernel(te_ref, nt_ref, xs_ref, w1_ref, b1_ref, w2_ref, b2_ref, ys_ref):
    @pl.when(pl.program_id(0) < nt_ref[0])
    def _():
        tm = xs_ref.shape[0] // ROW_TILES
        x = jnp.concatenate([xs_ref[_row_slab(tm, j), :] for j in range(ROW_TILES)], axis=1)
        hid = _dot(x.astype(BF16), w1_ref[0]) + b1_ref[0]
        acts = []
        for j in range(hid.shape[1] // SWIGLU_BLOCK):
            glu = jnp.minimum(hid[:, j * SWIGLU_BLOCK:j * SWIGLU_BLOCK + LANES], SWIGLU_LIMIT)
            lin = jnp.clip(hid[:, j * SWIGLU_BLOCK + LANES:(j + 1) * SWIGLU_BLOCK], -SWIGLU_LIMIT, SWIGLU_LIMIT)
            acts.append((glu * jax.nn.sigmoid(SWIGLU_ALPHA * glu) * (lin + 1.0)).astype(BF16))
        y = _dot(jnp.concatenate(acts, axis=1), w2_ref[0]) + b2_ref[0]
        for j in range(ROW_TILES):
            ys_ref[_row_slab(tm, j), :] = y[:, j * LANES:(j + 1) * LANES]

    @pl.when(pl.program_id(0) >= nt_ref[0])
    def _():
        ys_ref[...] = jnp.zeros_like(ys_ref)


def _expert_ffn(xs, tile_expert, n_tiles, w1p, b1p, w2b, b2):
    n_slots = xs.shape[0] // ROW_TILES
    d = w1p.shape[1]
    tm = MOE_TILE
    f2 = w1p.shape[2]
    rows_blk = (tm * ROW_TILES, LANES)
    exp_blk = lambda i, te, nt: (te[i], 0, 0)
    return pl.pallas_call(
        _ffn_kernel,
        out_shape=jax.ShapeDtypeStruct(xs.shape, F32),
        grid_spec=pltpu.PrefetchScalarGridSpec(
            num_scalar_prefetch=2,
            grid=(n_slots // tm,),
            in_specs=[
                pl.BlockSpec(rows_blk, lambda i, te, nt: (jnp.minimum(i, nt[0] - 1), 0)),
                pl.BlockSpec((1, d, f2), exp_blk),
                pl.BlockSpec((1, 1, f2), exp_blk),
                pl.BlockSpec((1, f2 // 2, d), exp_blk),
                pl.BlockSpec((1, 1, d), exp_blk),
            ],
            out_specs=pl.BlockSpec(rows_blk, lambda i, te, nt: (i, 0)),
        ),
        compiler_params=_cparams(("arbitrary",)),
        name="moe_expert_ffn",
    )(tile_expert, n_tiles, xs, w1p, b1p, w2b, b2)


def _combine_kernel(base_ref, e_ref, r_ref, en_ref, rn_ref, w_ref, xm_ref, gate_ref, l2g_ref, l2b_ref,
                    ys_hbm, o_ref, buf_ref, sem):
    i = pl.program_id(0)
    tt = xm_ref.shape[0]
    cur = lax.rem(i, 2)

    def issue(eref, rref, sl):
        def row_copy(r, kk, a):
            src = base_ref[eref[a]] + rref[a]
            return pltpu.make_async_copy(ys_hbm.at[_row_tile(src)], buf_ref.at[sl, kk, _row_tile(r)], sem.at[sl])
        _row_copy_groups(tt, row_copy)

    @pl.when(i == 0)
    def _():
        issue(e_ref, r_ref, 0)

    @pl.when(i + 1 < pl.num_programs(0))
    def _():
        issue(en_ref, rn_ref, 1 - cur)

    for kk in range(TOP_K):
        pltpu.make_async_copy(ys_hbm.at[pl.ds(0, tt * ROW_TILES)], buf_ref.at[cur, kk], sem.at[cur]).wait()
    w = w_ref[...]
    cols = []
    for j in range(ROW_TILES):
        acc = w[:, 0:1] * buf_ref[cur, 0, _row_slab(tt, j), :]
        for kk in range(1, TOP_K):
            acc = acc + w[:, kk:kk + 1] * buf_ref[cur, kk, _row_slab(tt, j), :]
        cols.append(acc)
    pre = DEEPNORM_ALPHA * xm_ref[...] + gate_ref[0] * jnp.concatenate(cols, axis=1)
    o_ref[...] = _layer_norm(pre) * l2g_ref[...] + l2b_ref[...]


def _combine(ys, e_flat, r_flat, base, sel_w, x_mid, gate2, ln2_g, ln2_b, tiles_per_batch):
    t, d = x_mid.shape
    tt = min(COMBINE_TILE, t)
    n = t // tt
    cur_blk = lambda: pl.BlockSpec((tt * TOP_K,), lambda i, *_: (i,), memory_space=pltpu.SMEM)
    nxt_blk = lambda: pl.BlockSpec((tt * TOP_K,), lambda i, *_: (jnp.minimum(i + 1, n - 1),),
                                   memory_space=pltpu.SMEM)
    return pl.pallas_call(
        _combine_kernel,
        out_shape=jax.ShapeDtypeStruct((t, d), F32),
        grid_spec=pltpu.PrefetchScalarGridSpec(
            num_scalar_prefetch=1,
            grid=(n,),
            in_specs=[
                cur_blk(), cur_blk(), nxt_blk(), nxt_blk(),
                pl.BlockSpec((tt, LANES), lambda i, *_: (i, 0)),
                pl.BlockSpec((tt, d), lambda i, *_: (i, 0)),
                pl.BlockSpec((1, 1, d), lambda i, *_: (i // tiles_per_batch, 0, 0)),
                pl.BlockSpec((1, d), lambda i, *_: (0, 0)),
                pl.BlockSpec((1, d), lambda i, *_: (0, 0)),
                pl.BlockSpec(memory_space=pl.ANY),
            ],
            out_specs=pl.BlockSpec((tt, d), lambda i, *_: (i, 0)),
            scratch_shapes=[pltpu.VMEM((2, TOP_K, tt * ROW_TILES, LANES), F32), pltpu.SemaphoreType.DMA((2,))],
        ),
        compiler_params=_cparams(("arbitrary",)),
        name="moe_combine",
    )(base, e_flat, r_flat, e_flat, r_flat, sel_w, x_mid, gate2, ln2_g, ln2_b, ys)


def kernel(x, c, ctx, c_ctx, ada_w, ada_b, w_in, gla_wa_f, gla_ba_f, gla_wa_b, gla_ba_b, gla_norm_g,
           hy_conv_w, hy_conv_b, hy_flt_w1, hy_flt_b1, hy_flt_w2, hy_flt_b2, hy_flt_wout, hy_flt_freq,
           hy_bias_d, w_out, ln1_g, ln1_b, router_w, router_b, exp_w1, exp_b1, exp_w2, exp_b2, ln2_g, ln2_b):
    batch, seq_len, d = x.shape
    lyr = 0
    ch = HY_WIDTH

    n_rows = 8 * ((batch + 1 + 7) // 8)
    cvec = jnp.zeros((n_rows, d), F32).at[:batch].set(c).at[batch].set(c_ctx)
    mod = _modulation(cvec, ada_w[lyr], ada_b[lyr][None, :])
    part = lambda rows, i: rows[:, None, i * d:(i + 1) * d]
    mod_x = mod[:batch]
    mod_c = jnp.broadcast_to(mod[batch:batch + 1], (batch, 6 * d))

    w = w_in[lyr]
    o_q, o_k, o_v, o_g = 0, GLA_KEY, 2 * GLA_KEY, 2 * GLA_KEY + GLA_VAL
    o_a = o_g + GLA_VAL
    o_h = o_a + 2 * GLA_RANK
    a_cols = jnp.pad(w[:, o_a:o_h], ((0, 0), (0, A_PAD - 2 * GLA_RANK)))
    q_cols = w[:, o_q:o_k] * (GLA_DK ** -0.5)
    w_main = jnp.concatenate([q_cols, w[:, o_k:o_a], a_cols], axis=1).astype(BF16)
    w_ctx = jnp.concatenate([q_cols, w[:, o_k:o_g], a_cols], axis=1).astype(BF16)
    w_hy_t = w[:, o_h:].T.astype(BF16)
    conv_w = jnp.broadcast_to(hy_conv_w[lyr][:, :, None], (HY_CONV, 3 * ch, LANES))
    conv_b = jnp.broadcast_to(hy_conv_b[lyr][:, None], (3 * ch, LANES))

    ctx_qkva = _input_projection(ctx, part(mod_c, 0), part(mod_c, 1), w_ctx)
    q, k, v, g, a_low, u_t = _input_projection(x, part(mod_x, 0), part(mod_x, 1), w_main, w_hy_t, conv_w, conv_b)

    wa = jnp.zeros((2, A_PAD, GLA_KEY), F32)
    wa = wa.at[0, :GLA_RANK].set(gla_wa_f[lyr]).at[1, GLA_RANK:2 * GLA_RANK].set(gla_wa_b[lyr]).astype(BF16)
    ba = jnp.stack([gla_ba_f[lyr], gla_ba_b[lyr]])[:, None, :]
    o_f, o_b = _gla_scan(ctx_qkva, (q, k, v, a_low), wa, ba)

    r1 = 2 * seq_len // LANES
    consts = _dft_constants(r1)
    zz, tn_rows, rate = _filter_inputs(seq_len)
    w1p = jnp.pad(hy_flt_w1[lyr], ((0, LANES - HY_EMB), (0, 0)))
    wo_t = hy_flt_wout[lyr].reshape(HY_FH, HY_ORDER, 2, ch).transpose(1, 2, 3, 0)
    filt = _filter_mlp(zz, tn_rows, rate, w1p, hy_flt_b1[lyr][None], hy_flt_w2[lyr], hy_flt_b2[lyr][None],
                       hy_flt_freq[lyr][None], wo_t)
    spectra = _filter_spectra(filt.reshape(HY_ORDER, ch, r1, LANES), consts, r1)
    y_hy = _hyena_conv(u_t.reshape(batch // 2, 2, 3 * ch, r1 // 2, LANES), spectra, hy_bias_d[lyr], consts, r1)
    y_hy_t = y_hy.reshape(batch, ch, seq_len)

    wo = w_out[lyr].astype(BF16)
    wr = jnp.pad(router_w[lyr], ((0, 0), (0, LANES - N_EXPERTS))).astype(BF16)
    rb = jnp.pad(router_b[lyr], (0, LANES - N_EXPERTS), constant_values=NEG_BIG)[None, :]
    norm_g = jnp.tile(gla_norm_g[lyr], GLA_HEADS)[None, :]
    x_mid, h2, sel_e, sel_r, sel_w, cnt = _output_projection(
        o_f, o_b, g, y_hy_t, x, part(mod_x, 2), part(mod_x, 3), part(mod_x, 4),
        wo[:GLA_VAL], wo[GLA_VAL:], norm_g, ln1_g[lyr][None], ln1_b[lyr][None], wr, rb)

    t = batch * seq_len
    counts = cnt[0, :N_EXPERTS]
    tiles_e = (counts + (MOE_TILE - 1)) // MOE_TILE
    tile_end = jnp.cumsum(tiles_e)
    base = ((tile_end - tiles_e) * MOE_TILE).astype(jnp.int32)
    n_tiles = tile_end[-1:].astype(jnp.int32)
    max_tiles = t * TOP_K // MOE_TILE + N_EXPERTS
    tile_ids = jnp.minimum(jnp.arange(max_tiles, dtype=jnp.int32), n_tiles[0] - 1)
    tile_expert = jnp.sum(tile_ids[:, None] >= tile_end[None, :], axis=1).astype(jnp.int32)
    e_flat = sel_e.reshape(t, LANES)[:, :TOP_K].reshape(t * TOP_K)
    r_flat = sel_r.reshape(t, LANES)[:, :TOP_K].reshape(t * TOP_K)

    w1p_e, w2_e = _expert_weight_layout(exp_w1[lyr], exp_w2[lyr])
    n_blk = 2 * D_EXPERT // SWIGLU_BLOCK
    b1p_e = exp_b1[lyr].reshape(N_EXPERTS, n_blk, LANES, 2).transpose(0, 1, 3, 2).reshape(N_EXPERTS, 1, 2 * D_EXPERT)
    xs = _dispatch(h2.reshape(t * ROW_TILES, LANES), e_flat, r_flat, base, counts, n_tiles, max_tiles * MOE_TILE)
    ys = _expert_ffn(xs, tile_expert, n_tiles, w1p_e, b1p_e, w2_e, exp_b2[lyr][:, None, :])
    out = _combine(ys, e_flat, r_flat, base, sel_w.reshape(t, LANES), x_mid.reshape(t, d), part(mod_x, 5),
                   ln2_g[lyr][None], ln2_b[lyr][None], seq_len // min(COMBINE_TILE, t))
    return out.reshape(batch, seq_len, d)
```

```python
import functools
import math

import numpy as np
import jax
import jax.numpy as jnp
from jax import lax
from jax.experimental import pallas as pl
from jax.experimental.pallas import tpu as pltpu

F32 = jnp.float32
BF16 = jnp.bfloat16

D_MODEL = 1024
DEPTH = 1
GRID_W = 64
LN_EPS = 1e-6
DEEPNORM_ALPHA = (2 * DEPTH) ** 0.25
GLA_HEADS = 4
GLA_DK = 64
GLA_DV = 128
GLA_KEY = GLA_HEADS * GLA_DK
GLA_VAL = GLA_HEADS * GLA_DV
GLA_RANK = 16
GLA_TAU = 16.0
HY_WIDTH = D_MODEL - GLA_VAL
HY_ORDER = 2
HY_CONV = 3
HY_EMB = 33
HY_FH = 64
HY_TARGET = 1e-2
HY_FAST = 0.3
HY_SLOW = 1.5
N_EXPERTS = 32
TOP_K = 4
D_EXPERT = D_MODEL
SWIGLU_ALPHA = 1.702
SWIGLU_LIMIT = 7.0

LANES = 128
VMEM_LIMIT = 56 * 1024 * 1024

TOK_TILE = 512
GLA_CHUNK = 128
GLA_BLOCK = 256
GLA_BATCH = 2
GLA_SAFE_EXPONENT = 80.0
A_PAD = LANES
HY_GROUP = 8
HY_CH_BLOCK = 8
FILT_TILE = 512
MOE_TILE = 512
DISPATCH_TILE = 1024
COMBINE_TILE = 256
DMA_UNROLL = 8
ROW_TILES = D_MODEL // LANES


def _row_slab(n_rows, j):
    return pl.ds(j, n_rows, stride=ROW_TILES)


def _row_tile(r):
    return pl.ds(pl.multiple_of(r * ROW_TILES, ROW_TILES), ROW_TILES)
SWIGLU_BLOCK = 2 * LANES
NEG_BIG = -1e30


def _cparams(sem):
    return pltpu.CompilerParams(dimension_semantics=sem, vmem_limit_bytes=VMEM_LIMIT)


def _layer_norm(x):
    mu = jnp.mean(x, axis=-1, keepdims=True)
    xc = x - mu
    return xc * lax.rsqrt(jnp.mean(xc * xc, axis=-1, keepdims=True) + LN_EPS)


def _dot(a, b):
    return jnp.dot(a, b, preferred_element_type=F32)


def _dot_nt(a, b):
    return lax.dot_general(a, b, (((1,), (1,)), ((), ())), preferred_element_type=F32)


def _dot_tn(a, b):
    return lax.dot_general(a, b, (((0,), (0,)), ((), ())), preferred_element_type=F32)


def _mod_kernel(c_ref, w_ref, b_ref, o_ref):
    c = c_ref[...]
    s = c * jax.nn.sigmoid(c)
    o_ref[...] = _dot(s.astype(BF16), w_ref[...].astype(BF16)) + b_ref[...]


def _modulation(cvec, ada_w, ada_b):
    rows, d = cvec.shape
    n = ada_w.shape[1]
    tn = 1024
    return pl.pallas_call(
        _mod_kernel,
        out_shape=jax.ShapeDtypeStruct((rows, n), F32),
        grid=(n // tn,),
        in_specs=[
            pl.BlockSpec((rows, d), lambda j: (0, 0)),
            pl.BlockSpec((d, tn), lambda j: (0, j)),
            pl.BlockSpec((1, tn), lambda j: (0, j)),
        ],
        out_specs=pl.BlockSpec((rows, tn), lambda j: (0, j)),
        compiler_params=_cparams(("arbitrary",)),
        name="adaln_mod",
    )(cvec, ada_w, ada_b)


def _inproj_kernel(x_ref, sh_ref, sc_ref, wm_ref, *rest, with_hy):
    if with_hy:
        wh_ref, cw_ref, cb_ref, q_ref, k_ref, v_ref, g_ref, a_ref, u_ref = rest
    else:
        q_ref, k_ref, v_ref, a_ref = rest
    h = _layer_norm(x_ref[0]) * (1.0 + sc_ref[0]) + sh_ref[0]
    hb = h.astype(BF16)
    u = _dot(hb, wm_ref[...])
    q_ref[0] = u[:, :GLA_KEY].astype(BF16)
    k_ref[0] = u[:, GLA_KEY:2 * GLA_KEY].astype(BF16)
    v_ref[0] = u[:, 2 * GLA_KEY:2 * GLA_KEY + GLA_VAL].astype(BF16)
    off = 2 * GLA_KEY + GLA_VAL
    if with_hy:
        g_ref[0] = u[:, off:off + GLA_VAL].astype(BF16)
        off += GLA_VAL
    a_ref[0] = u[:, off:off + A_PAD]
    if with_hy:
        ut = _dot_nt(wh_ref[...], hb)
        lane = lax.broadcasted_iota(jnp.int32, (1, LANES), 1) % GRID_W
        not_first = (lane != 0).astype(F32)
        not_last = (lane != GRID_W - 1).astype(F32)
        tt = ut.shape[1]
        for j in range(tt // LANES):
            c = ut[:, j * LANES:(j + 1) * LANES]
            left = pltpu.roll(c, 1, axis=1) * not_first
            right = pltpu.roll(c, LANES - 1, axis=1) * not_last
            y = cw_ref[0] * left + cw_ref[1] * c + cw_ref[2] * right + cb_ref[...]
            u_ref[0, :, j * LANES:(j + 1) * LANES] = y.astype(BF16)


def _input_projection(x, shift, scale, w_main, w_hy_t=None, conv_w=None, conv_b=None):
    b, l, d = x.shape
    tt = min(TOK_TILE, l)
    with_hy = w_hy_t is not None
    nm = w_main.shape[1]
    tok = lambda width, dt: jax.ShapeDtypeStruct((b, l, width), dt)
    tok_spec = lambda width: pl.BlockSpec((1, tt, width), lambda i, j: (i, j, 0))
    in_specs = [
        pl.BlockSpec((1, tt, d), lambda i, j: (i, j, 0)),
        pl.BlockSpec((1, 1, d), lambda i, j: (i, 0, 0)),
        pl.BlockSpec((1, 1, d), lambda i, j: (i, 0, 0)),
        pl.BlockSpec((d, nm), lambda i, j: (0, 0)),
    ]
    args = [x, shift, scale, w_main]
    if with_hy:
        ch = w_hy_t.shape[0]
        in_specs += [
            pl.BlockSpec((ch, d), lambda i, j: (0, 0)),
            pl.BlockSpec((HY_CONV, ch, LANES), lambda i, j: (0, 0, 0)),
            pl.BlockSpec((ch, LANES), lambda i, j: (0, 0)),
        ]
        args += [w_hy_t, conv_w, conv_b]
        out_shape = [tok(GLA_KEY, BF16), tok(GLA_KEY, BF16), tok(GLA_VAL, BF16), tok(GLA_VAL, BF16),
                     tok(A_PAD, F32), jax.ShapeDtypeStruct((b, ch, l), BF16)]
        out_specs = [tok_spec(GLA_KEY), tok_spec(GLA_KEY), tok_spec(GLA_VAL), tok_spec(GLA_VAL),
                     tok_spec(A_PAD), pl.BlockSpec((1, ch, tt), lambda i, j: (i, 0, j))]
    else:
        out_shape = [tok(GLA_KEY, BF16), tok(GLA_KEY, BF16), tok(GLA_VAL, BF16), tok(A_PAD, F32)]
        out_specs = [tok_spec(GLA_KEY), tok_spec(GLA_KEY), tok_spec(GLA_VAL), tok_spec(A_PAD)]
    return pl.pallas_call(
        functools.partial(_inproj_kernel, with_hy=with_hy),
        out_shape=out_shape,
        grid=(b, l // tt),
        in_specs=in_specs,
        out_specs=out_specs,
        compiler_params=_cparams(("parallel", "arbitrary")),
        name="in_proj_hy" if with_hy else "in_proj_ctx",
    )(*args)


def _gla_kernel(*refs, n_ctx_blocks, chunk, n_batch, n_levels, safe_exponent):
    ins = [refs[0:8], refs[8:16]]
    wa_ref, ba_ref, cw_ref, cm_ref, hm_ref, bd_ref, dq_ref, dk_ref, lm_ref = refs[16:25]
    outs = refs[25:27]
    st_ref = refs[27]
    s = pl.program_id(1)
    blk = ins[0][4].shape[1]
    n_chunks = blk // chunk

    @pl.when(s == 0)
    def _():
        st_ref[...] = jnp.zeros_like(st_ref)

    def two_pass(m, g_hi, g_lo):
        return _dot(m, g_hi) + _dot(m, g_lo)

    is_ctx = s < n_ctx_blocks
    bdmask = bd_ref[...]
    chains = [(d, bi) for d in range(2) for bi in range(n_batch)]

    def pre_activation(d, bi):
        a = jnp.where(is_ctx, ins[d][3][bi], ins[d][7][bi])
        return _dot(a.astype(BF16), wa_ref[d]) + ba_ref[d]

    def advance(d, bi, z, single_reference):
        qc_ref, kc_ref, vc_ref, _, ql_ref, kl_ref, vl_ref, _ = ins[d]
        q = jnp.where(is_ctx, qc_ref[bi], ql_ref[bi]).astype(F32)
        k = jnp.where(is_ctx, kc_ref[bi], kl_ref[bi]).astype(F32)
        v = jnp.where(is_ctx, vc_ref[bi], vl_ref[bi])
        g = (jnp.minimum(z, 0.0) - jnp.log(1.0 + jnp.exp(-jnp.abs(z)))) * (1.0 / GLA_TAU)
        g_hi = g.astype(BF16)
        g_lo = (g - g_hi.astype(F32)).astype(BF16)
        cum = two_pass(cw_ref[d], g_hi, g_lo)
        edge = (lambda c: (c + 1) * chunk - 1) if d == 0 else (lambda c: c * chunk)
        tot = jnp.concatenate([jnp.broadcast_to(cum[edge(c):edge(c) + 1], (chunk, GLA_KEY))
                               for c in range(n_chunks)], axis=0)
        qs_all = (q * jnp.exp(cum)).astype(BF16)
        kst_all = (k * jnp.exp(tot - cum)).astype(BF16)
        dec_all = jnp.exp(tot)

        if single_reference:
            ks_all = (k * jnp.exp(-cum)).astype(BF16)
            cmask = cm_ref[d]
            rows_out = []
            for c in range(n_chunks):
                rows = slice(c * chunk, (c + 1) * chunk)
                parts = []
                for h in range(GLA_HEADS):
                    a_h = _dot_nt(qs_all[rows] * hm_ref[h], ks_all[rows]) * cmask
                    parts.append(_dot(a_h.astype(BF16), v[rows, h * GLA_DV:(h + 1) * GLA_DV]))
                rows_out.append(jnp.concatenate(parts, axis=1))
            o_intra = jnp.concatenate(rows_out, axis=0)
        else:
            def level(lv, acc):
                ql = (q * jnp.exp(two_pass(dq_ref[d, lv], g_hi, g_lo))).astype(BF16)
                kl = (k * jnp.exp(two_pass(dk_ref[d, lv], g_hi, g_lo))).astype(BF16)
                msk = lm_ref[d, lv]
                parts = []
                for h in range(GLA_HEADS):
                    a_h = _dot_nt(ql * hm_ref[h], kl) * msk
                    parts.append(_dot(a_h.astype(BF16), v[:, h * GLA_DV:(h + 1) * GLA_DV]))
                return acc + jnp.concatenate(parts, axis=1)
            o_intra = lax.fori_loop(0, n_levels, level, jnp.zeros((blk, GLA_VAL), F32))

        for n in range(n_chunks):
            ci = n if d == 0 else n_chunks - 1 - n
            rows = slice(ci * chunk, (ci + 1) * chunk)
            o = _dot_nt(qs_all[rows], st_ref[d, bi].astype(BF16)) + o_intra[rows]
            outs[d][bi, rows, :] = o.astype(outs[d].dtype)
            st_ref[d, bi] = (st_ref[d, bi] * dec_all[ci * chunk:ci * chunk + 1]
                             + _dot_tn(v[rows], kst_all[rows]) * bdmask)

    zs = [pre_activation(d, bi) for d, bi in chains]
    z_low = zs[0]
    for z in zs[1:]:
        z_low = jnp.minimum(z_low, z)
    bound = (jnp.maximum(-jnp.min(z_low), 0.0) + math.log(2.0)) * (chunk / GLA_TAU)

    def step(single_reference):
        def run():
            for (d, bi), z in zip(chains, zs):
                advance(d, bi, z, single_reference)
        return run

    lax.cond(bound < safe_exponent, step(True), step(False))


def _gla_level_tables(blk, chunk):
    sizes = []
    s = chunk // 2
    while s >= 1:
        sizes.append(s)
        s //= 2
    n_lv = len(sizes) + 1
    dq = np.zeros((2, n_lv, blk, blk), np.float32)
    dk = np.zeros((2, n_lv, blk, blk), np.float32)
    lm = np.zeros((2, n_lv, blk, blk), np.float32)
    idx = np.arange(blk)
    for lv, s in enumerate(sizes):
        blk_id = idx // (2 * s)
        right = (idx % (2 * s)) >= s
        b = blk_id * 2 * s + s
        same = blk_id[:, None] == blk_id[None, :]
        m = idx[None, :]
        dq[0, lv] = (right[:, None] & (m >= b[:, None]) & (m <= idx[:, None]))
        dk[0, lv] = (~right[:, None] & (m > idx[:, None]) & (m < b[:, None]))
        lm[0, lv] = same & right[:, None] & ~right[None, :]
        dq[1, lv] = (~right[:, None] & (m >= idx[:, None]) & (m < b[:, None]))
        dk[1, lv] = (right[:, None] & (m >= b[:, None]) & (m < idx[:, None]))
        lm[1, lv] = same & ~right[:, None] & right[None, :]
    lm[:, n_lv - 1] = np.eye(blk, dtype=np.float32)
    return dq, dk, lm, n_lv


def _gla_scan(ctx_qkva, lat_qkva, wa, ba, chunk=GLA_CHUNK, blk=GLA_BLOCK, safe_exponent=GLA_SAFE_EXPONENT):
    qc, kc, vc, ac = ctx_qkva
    ql, kl, vl, al = lat_qkva
    b, l, _ = ql.shape
    ctx_len = qc.shape[1]
    n_ctx = ctx_len // blk
    n_lat = l // blk
    c = chunk
    idx = np.arange(c)
    tri = np.stack([idx[:, None] >= idx[None, :], idx[:, None] <= idx[None, :]]).astype(np.float32)
    cw = np.stack([np.kron(np.eye(blk // c, dtype=np.float32), tri[dd]) for dd in range(2)])
    dq, dk, lm, n_levels = _gla_level_tables(blk, c)
    hmask = np.zeros((GLA_HEADS, 1, GLA_KEY), np.float32)
    for h in range(GLA_HEADS):
        hmask[h, 0, h * GLA_DK:(h + 1) * GLA_DK] = 1.0
    bd = (np.arange(GLA_VAL)[:, None] // GLA_DV == np.arange(GLA_KEY)[None, :] // GLA_DK).astype(np.float32)

    nb = GLA_BATCH if b % GLA_BATCH == 0 else 1

    def lat_block(d):
        def f(i, s):
            t = jnp.maximum(s - n_ctx, 0)
            return (i, t if d == 0 else n_lat - 1 - t, 0)
        return f

    def ctx_block(d):
        def f(i, s):
            t = jnp.minimum(s, n_ctx - 1)
            return (i, t if d == 0 else n_ctx - 1 - t, 0)
        return f

    widths = (GLA_KEY, GLA_KEY, GLA_VAL, A_PAD)
    dir_specs = lambda d: ([pl.BlockSpec((nb, blk, w), ctx_block(d)) for w in widths]
                           + [pl.BlockSpec((nb, blk, w), lat_block(d)) for w in widths])
    consts = [wa, ba, jnp.asarray(cw, BF16), jnp.asarray(tri, F32), jnp.asarray(hmask, BF16), jnp.asarray(bd, F32),
              jnp.asarray(dq, BF16), jnp.asarray(dk, BF16), jnp.asarray(lm, F32)]
    full = lambda arr: pl.BlockSpec(arr.shape, lambda i, s: (0,) * arr.ndim)
    o_sds = jax.ShapeDtypeStruct((b, l, GLA_VAL), BF16)
    dir_args = [qc, kc, vc, ac, ql, kl, vl, al]
    return pl.pallas_call(
        functools.partial(_gla_kernel, n_ctx_blocks=n_ctx, chunk=c, n_batch=nb, n_levels=n_levels,
                          safe_exponent=safe_exponent),
        out_shape=[o_sds, o_sds],
        grid=(b // nb, n_ctx + n_lat),
        in_specs=dir_specs(0) + dir_specs(1) + [full(arr) for arr in consts],
        out_specs=[pl.BlockSpec((nb, blk, GLA_VAL), lat_block(0)), pl.BlockSpec((nb, blk, GLA_VAL), lat_block(1))],
        scratch_shapes=[pltpu.VMEM((2, nb, GLA_VAL, GLA_KEY), F32)],
        compiler_params=_cparams(("parallel", "arbitrary")),
        name="gla_scan",
    )(*dir_args, *dir_args, *consts)


def _dft_constants(r1):
    n = r1 * LANES
    h = r1 // 2
    k1 = np.arange(r1)
    f1 = np.exp(-2j * np.pi * np.outer(k1, k1) / r1)
    f2 = np.exp(-2j * np.pi * np.outer(np.arange(LANES), np.arange(LANES)) / LANES)
    tw = np.exp(-2j * np.pi * np.outer(k1, np.arange(LANES)) / n)
    fa_c = np.block([[f1.real[:, :h], -f1.imag[:, :h]], [f1.imag[:, :h], f1.real[:, :h]]])
    fa_r = np.concatenate([f1.real, f1.imag], axis=0)
    gc = np.block([[f2.real, f2.imag], [-f2.imag, f2.real]])
    gci = np.block([[f2.real, -f2.imag], [f2.imag, f2.real]])
    fai = np.block([[f1.real[:h], f1.imag[:h]], [-f1.imag[:h], f1.real[:h]]]) / n
    tw_lane = np.tile(tw, (1, HY_GROUP))
    tw_row = np.tile(tw, (HY_GROUP, 1))
    f = lambda a: jnp.asarray(a, F32)
    return dict(fa_c=f(fa_c), fa_r=f(fa_r), gc=f(gc), gci=f(gci), fai=f(fai),
                twl_r=f(tw_lane.real), twl_i=f(tw_lane.imag), twr_r=f(tw_row.real), twr_i=f(tw_row.imag))


def _fwd_dft(rhs, fa, gc, twl_r, twl_i, r1):
    a = _dot(fa, rhs)
    ar, ai = a[:r1], a[r1:]
    br = (ar * twl_r - ai * twl_i).astype(BF16)
    bi = (ar * twl_i + ai * twl_r).astype(BF16)
    lhs = jnp.concatenate(
        [jnp.concatenate([br[:, c * LANES:(c + 1) * LANES], bi[:, c * LANES:(c + 1) * LANES]], axis=1)
         for c in range(HY_GROUP)], axis=0)
    return _dot(lhs, gc)


def _inv_dft(yr, yi, gci, fai, twr_r, twr_i, r1):
    lhs = jnp.concatenate([yr, yi], axis=1).astype(BF16)
    c = _dot(lhs, gci)
    cr, ci = c[:, :LANES], c[:, LANES:]
    dr = (cr * twr_r + ci * twr_i).astype(BF16)
    di = (ci * twr_r - cr * twr_i).astype(BF16)
    rhs = jnp.concatenate(
        [jnp.concatenate([dr[g * r1:(g + 1) * r1], di[g * r1:(g + 1) * r1]], axis=0)
         for g in range(HY_GROUP)], axis=1)
    return _dot(fai, rhs)


def _filter_mlp_kernel(zz_ref, tn_ref, rate_ref, w1_ref, b1_ref, w2_ref, b2_ref, fr_ref, wo_ref, o_ref):
    hp = lax.Precision.HIGHEST
    fr = fr_ref[...]
    hid = jnp.sin(fr * (jnp.dot(zz_ref[...], w1_ref[...], precision=hp, preferred_element_type=F32) + b1_ref[...]))
    hid = jnp.sin(fr * (jnp.dot(hid, w2_ref[...], precision=hp, preferred_element_type=F32) + b2_ref[...]))
    lt = tn_ref.shape[1]
    rate = jnp.concatenate([rate_ref[...]] * (lt // LANES), axis=1)
    window = jnp.exp(-tn_ref[0:1, :] * rate) * tn_ref[1:2, :]
    for o in range(HY_ORDER):
        ht = lax.dot_general(wo_ref[o, 0], hid, (((1,), (1,)), ((), ())), precision=hp,
                             preferred_element_type=F32)
        o_ref[o] = ht * window


def _filter_mlp(zz, tn_rows, rate, w1p, b1, w2, b2, freq, wo_t):
    n2l = zz.shape[0]
    l = n2l // 2
    lt = min(FILT_TILE, l)
    nt = l // lt
    ch = rate.shape[0]
    full = lambda shape: pl.BlockSpec(shape, lambda d, j: (0,) * len(shape))
    return pl.pallas_call(
        _filter_mlp_kernel,
        out_shape=jax.ShapeDtypeStruct((HY_ORDER, ch, n2l), F32),
        grid=(2, nt),
        in_specs=[
            pl.BlockSpec((lt, LANES), lambda d, j: (d * nt + j, 0)),
            pl.BlockSpec((8, lt), lambda d, j: (0, d * nt + j)),
            full((ch, LANES)),
            full((LANES, HY_FH)), full((1, HY_FH)), full((HY_FH, HY_FH)), full((1, HY_FH)), full((1, HY_FH)),
            pl.BlockSpec((HY_ORDER, 1, ch, HY_FH), lambda d, j: (0, d, 0, 0)),
        ],
        out_specs=pl.BlockSpec((HY_ORDER, ch, lt), lambda d, j: (0, 0, d * nt + j)),
        compiler_params=_cparams(("arbitrary", "arbitrary")),
        name="hyena_filter_mlp",
    )(zz, tn_rows, rate, w1p, b1, w2, b2, freq, wo_t)


def _filter_fft_kernel(f_ref, fa_ref, gc_ref, twl_r_ref, twl_i_ref, h_ref, *, r1):
    fa = fa_ref[...].astype(BF16)
    gc = gc_ref[...].astype(BF16)
    nc = f_ref.shape[1]
    for g0 in range(0, nc, HY_GROUP):
        rhs = jnp.concatenate([f_ref[0, g0 + c].astype(BF16) for c in range(HY_GROUP)], axis=1)
        x = _fwd_dft(rhs, fa, gc, twl_r_ref[...], twl_i_ref[...], r1)
        for c in range(HY_GROUP):
            h_ref[0, g0 + c] = x[c * r1:(c + 1) * r1]


def _filter_spectra(filt, consts, r1):
    order, ch = filt.shape[:2]
    nc = HY_CH_BLOCK
    full = lambda a: pl.BlockSpec(a.shape, lambda o, j: (0,) * a.ndim)
    cs = [consts["fa_r"], consts["gc"], consts["twl_r"], consts["twl_i"]]
    return pl.pallas_call(
        functools.partial(_filter_fft_kernel, r1=r1),
        out_shape=jax.ShapeDtypeStruct((order, ch, r1, 2 * LANES), F32),
        grid=(order, ch // nc),
        in_specs=[pl.BlockSpec((1, nc, r1, LANES), lambda o, j: (o, j, 0, 0))] + [full(a) for a in cs],
        out_specs=pl.BlockSpec((1, nc, r1, 2 * LANES), lambda o, j: (o, j, 0, 0)),
        compiler_params=_cparams(("arbitrary", "arbitrary")),
        name="hyena_filter_fft",
    )(filt, *cs)


def _hyena_kernel(dbias_ref, v_ref, x1_ref, x2_ref, h_ref, fa_ref, gc_ref, gci_ref, fai_ref,
                  twl_r_ref, twl_i_ref, twr_r_ref, twr_i_ref, y_ref, *, r1):
    fa = fa_ref[...].astype(BF16)
    gc = gc_ref[...].astype(BF16)
    gci = gci_ref[...].astype(BF16)
    fai = fai_ref[...].astype(BF16)
    twl_r, twl_i = twl_r_ref[...], twl_i_ref[...]
    twr_r, twr_i = twr_r_ref[...], twr_i_ref[...]
    nc = v_ref.shape[2]
    half = r1 // 2
    c_base = pl.program_id(0) * nc

    def conv(sig, order, g0):
        rhs = jnp.concatenate(
            [jnp.concatenate([sig[c][0].astype(BF16), sig[c][1].astype(BF16)], axis=0)
             for c in range(HY_GROUP)], axis=1)
        x = _fwd_dft(rhs, fa, gc, twl_r, twl_i, r1)
        xr, xi = x[:, :LANES], x[:, LANES:]
        hh = jnp.concatenate([h_ref[order, g0 + c] for c in range(HY_GROUP)], axis=0)
        hr, hi = hh[:, :LANES], hh[:, LANES:]
        y = _inv_dft(xr * hr - xi * hi, xr * hi + xi * hr, gci, fai, twr_r, twr_i, r1)
        out = []
        for c in range(HY_GROUP):
            dcoef = dbias_ref[order, c_base + g0 + c]
            yc = y[:, c * LANES:(c + 1) * LANES]
            out.append([yc[:half] + dcoef * sig[c][0], yc[half:] + dcoef * sig[c][1]])
        return out

    for g0 in range(0, nc, HY_GROUP):
        v = [[v_ref[0, b, g0 + c].astype(F32) for b in range(2)] for c in range(HY_GROUP)]
        y1 = conv(v, 0, g0)
        z = [[x1_ref[0, b, g0 + c].astype(F32) * y1[c][b] for b in range(2)] for c in range(HY_GROUP)]
        y2 = conv(z, 1, g0)
        for c in range(HY_GROUP):
            for b in range(2):
                y_ref[0, b, g0 + c] = (x2_ref[0, b, g0 + c].astype(F32) * y2[c][b]).astype(y_ref.dtype)


def _hyena_conv(u_t, spectra, d_bias, consts, r1):
    bp, _, ch3, half, _ = u_t.shape
    ch = ch3 // 3
    nc = HY_CH_BLOCK
    nblk = ch // nc
    names = ["fa_c", "gc", "gci", "fai", "twl_r", "twl_i", "twr_r", "twr_i"]
    cs = [consts[k] for k in names]
    full = lambda a: pl.BlockSpec(a.shape, lambda j, p: (0,) * a.ndim)
    part = lambda k: pl.BlockSpec((1, 2, nc, half, LANES), lambda j, p: (p, 0, k * nblk + j, 0, 0))
    return pl.pallas_call(
        functools.partial(_hyena_kernel, r1=r1),
        out_shape=jax.ShapeDtypeStruct((bp, 2, ch, half, LANES), BF16),
        grid=(nblk, bp),
        in_specs=[pl.BlockSpec(memory_space=pltpu.SMEM), part(0), part(1), part(2),
                  pl.BlockSpec((HY_ORDER, nc, r1, 2 * LANES), lambda j, p: (0, j, 0, 0))] + [full(a) for a in cs],
        out_specs=pl.BlockSpec((1, 2, nc, half, LANES), lambda j, p: (p, 0, j, 0, 0)),
        compiler_params=_cparams(("arbitrary", "arbitrary")),
        name="hyena_conv",
    )(d_bias, u_t, u_t, u_t, spectra, *cs)


def _filter_inputs(l):
    n = jnp.arange(2 * l, dtype=jnp.int32)
    t = jnp.where(n < l, n, 2 * l - n).astype(F32)
    valid = (n != l).astype(F32)
    t_norm = t / (l - 1)
    bands = (HY_EMB - 1) // 2
    f = jnp.linspace(1e-4, bands - 1, bands, dtype=F32)
    ang = (2.0 * math.pi * t / l)[:, None] * f[None, :]
    z = jnp.concatenate([t_norm[:, None], jnp.cos(ang), -jnp.sin(ang)], -1)
    zz = jnp.pad(z, ((0, 0), (0, LANES - HY_EMB)))
    tn_rows = jnp.zeros((8, 2 * l), F32).at[0].set(t_norm).at[1].set(valid)
    deltas = jnp.linspace(math.log(HY_TARGET) / HY_SLOW, math.log(HY_TARGET) / HY_FAST, HY_WIDTH, dtype=F32)
    rate = jnp.broadcast_to(jnp.abs(deltas)[:, None], (HY_WIDTH, LANES))
    return zz, tn_rows, rate


def _outproj_kernel(of_ref, ob_ref, g_ref, yh_ref, x_ref, gate_ref, sh_ref, sc_ref,
                    wg_ref, wh_ref, ng_ref, l1g_ref, l1b_ref, wr_ref, rb_ref, ltri_ref,
                    xm_ref, h2_ref, se_ref, sr_ref, sw_ref, cnt_ref, carry_ref):
    @pl.when((pl.program_id(0) == 0) & (pl.program_id(1) == 0))
    def _():
        carry_ref[...] = jnp.zeros_like(carry_ref)

    o = of_ref[0].astype(F32) + ob_ref[0].astype(F32)
    g = g_ref[0].astype(F32)
    parts = []
    for h in range(GLA_HEADS):
        oh = o[:, h * GLA_DV:(h + 1) * GLA_DV]
        parts.append(oh * lax.rsqrt(jnp.mean(oh * oh, axis=-1, keepdims=True) + LN_EPS))
    y_gla = jnp.concatenate(parts, axis=1) * ng_ref[...] * (g * jax.nn.sigmoid(g))
    f = _dot(y_gla.astype(BF16), wg_ref[...]) + _dot_tn(yh_ref[0], wh_ref[...])
    x_mid = _layer_norm(DEEPNORM_ALPHA * x_ref[0] + gate_ref[0] * f) * l1g_ref[...] + l1b_ref[...]
    xm_ref[0] = x_mid
    h2 = _layer_norm(x_mid) * (1.0 + sc_ref[0]) + sh_ref[0]
    for j in range(ROW_TILES):
        h2_ref[0, _row_slab(h2.shape[0], j), :] = h2[:, j * LANES:(j + 1) * LANES]

    logits = _dot(h2.astype(BF16), wr_ref[...]) + rb_ref[...]
    lane = lax.broadcasted_iota(jnp.int32, logits.shape, 1).astype(F32)
    hits, idxs, exps = [], [], []
    m0 = None
    for _ in range(TOP_K):
        m = jnp.max(logits, axis=-1, keepdims=True)
        idx = jnp.min(jnp.where(logits == m, lane, float(LANES)), axis=-1, keepdims=True)
        hit = lane == idx
        m0 = m if m0 is None else m0
        hits.append(hit)
        idxs.append(idx)
        exps.append(jnp.exp(m - m0))
        logits = jnp.where(hit, NEG_BIG, logits)
    denom = exps[0]
    sel = jnp.where(hits[0], 1.0, 0.0)
    for kk in range(1, TOP_K):
        denom = denom + exps[kk]
        sel = sel + jnp.where(hits[kk], 1.0, 0.0)
    rank_all = _dot(ltri_ref[...], sel.astype(BF16)) + carry_ref[0:1, :]
    carry_ref[0:1, :] = carry_ref[0:1, :] + jnp.sum(sel, axis=0, keepdims=True)
    cnt_ref[...] = jnp.broadcast_to(carry_ref[0:1, :], cnt_ref.shape).astype(jnp.int32)
    se = jnp.zeros(logits.shape, F32)
    sr = jnp.zeros(logits.shape, F32)
    sw = jnp.zeros(logits.shape, F32)
    for kk in range(TOP_K):
        rk = jnp.sum(jnp.where(hits[kk], rank_all, 0.0), axis=-1, keepdims=True)
        col = lane == float(kk)
        se = jnp.where(col, idxs[kk], se)
        sr = jnp.where(col, rk, sr)
        sw = jnp.where(col, exps[kk] / denom, sw)
    se_ref[0] = se.astype(jnp.int32)
    sr_ref[0] = sr.astype(jnp.int32)
    sw_ref[0] = sw


def _output_projection(o_f, o_b, g, y_hy_t, x, gate1, shift2, scale2, w_gla, w_hy, norm_g, ln1_g, ln1_b, wr, rb):
    b, l, d = x.shape
    tt = TOK_TILE
    tok = lambda w: pl.BlockSpec((1, tt, w), lambda i, j: (i, j, 0))
    row = lambda: pl.BlockSpec((1, 1, d), lambda i, j: (i, 0, 0))
    full = lambda a: pl.BlockSpec(a.shape, lambda i, j: (0,) * a.ndim)
    ltri = jnp.asarray(np.tril(np.ones((tt, tt), np.float32), -1), BF16)
    consts = [w_gla, w_hy, norm_g, ln1_g, ln1_b, wr, rb, ltri]
    lane_i = jax.ShapeDtypeStruct((b, l, LANES), jnp.int32)
    return pl.pallas_call(
        _outproj_kernel,
        out_shape=[jax.ShapeDtypeStruct((b, l, d), F32), jax.ShapeDtypeStruct((b, l * ROW_TILES, LANES), F32),
                   lane_i, lane_i, jax.ShapeDtypeStruct((b, l, LANES), F32),
                   jax.ShapeDtypeStruct((8, LANES), jnp.int32)],
        grid=(b, l // tt),
        in_specs=[
            tok(GLA_VAL), tok(GLA_VAL), tok(GLA_VAL),
            pl.BlockSpec((1, HY_WIDTH, tt), lambda i, j: (i, 0, j)),
            tok(d), row(), row(), row(),
        ] + [full(a) for a in consts],
        out_specs=[tok(d), pl.BlockSpec((1, tt * ROW_TILES, LANES), lambda i, j: (i, j, 0)),
                   tok(LANES), tok(LANES), tok(LANES),
                   pl.BlockSpec((8, LANES), lambda i, j: (0, 0))],
        scratch_shapes=[pltpu.VMEM((8, LANES), F32)],
        compiler_params=_cparams(("arbitrary", "arbitrary")),
        name="out_proj_router",
    )(o_f, o_b, g, y_hy_t, x, gate1, shift2, scale2, *consts)


def _wprep_kernel(w1_ref, w2_ref, p_ref, w1o_ref, w2o_ref):
    p = p_ref[...]
    for j in range(w1_ref.shape[2] // SWIGLU_BLOCK):
        cols = slice(j * SWIGLU_BLOCK, (j + 1) * SWIGLU_BLOCK)
        w1o_ref[0, :, cols] = _dot(w1_ref[0, :, cols].astype(BF16), p).astype(BF16)
    w2o_ref[0] = w2_ref[0].astype(BF16)


def _expert_weight_layout(w1, w2):
    ne, d, f2 = w1.shape
    src = np.concatenate([np.arange(0, SWIGLU_BLOCK, 2), np.arange(1, SWIGLU_BLOCK, 2)])
    perm = np.zeros((SWIGLU_BLOCK, SWIGLU_BLOCK), np.float32)
    perm[src, np.arange(SWIGLU_BLOCK)] = 1.0
    return pl.pallas_call(
        _wprep_kernel,
        out_shape=[jax.ShapeDtypeStruct(w1.shape, BF16), jax.ShapeDtypeStruct(w2.shape, BF16)],
        grid=(ne,),
        in_specs=[
            pl.BlockSpec((1, d, f2), lambda e: (e, 0, 0)),
            pl.BlockSpec((1,) + w2.shape[1:], lambda e: (e, 0, 0)),
            pl.BlockSpec((SWIGLU_BLOCK, SWIGLU_BLOCK), lambda e: (0, 0)),
        ],
        out_specs=[pl.BlockSpec((1, d, f2), lambda e: (e, 0, 0)),
                   pl.BlockSpec((1,) + w2.shape[1:], lambda e: (e, 0, 0))],
        compiler_params=_cparams(("arbitrary",)),
        name="expert_weight_layout",
    )(w1, w2, jnp.asarray(perm, BF16))


def _row_copy_groups(n_tokens, make_copy):
    def group(gidx, carry):
        for u in range(DMA_UNROLL):
            r = gidx * DMA_UNROLL + u
            for kk in range(TOP_K):
                make_copy(r, kk, r * TOP_K + kk).start(priority=kk % 2)
        return carry
    lax.fori_loop(0, n_tokens // DMA_UNROLL, group, 0)


def _dispatch_kernel(base_ref, cnt_ref, nt_ref, slot_ref, h2_ref, xs_hbm, zrow_ref, zblk_ref, sem, zsem):
    tt = h2_ref.shape[0] // ROW_TILES

    def row_copy(r, kk, a):
        return pltpu.make_async_copy(h2_ref.at[_row_tile(r)], xs_hbm.at[_row_tile(slot_ref[a])], sem)

    _row_copy_groups(tt, row_copy)
    for _ in range(TOP_K):
        pltpu.make_async_copy(h2_ref, xs_hbm.at[pl.ds(0, tt * ROW_TILES)], sem).wait()

    @pl.when(pl.program_id(0) == pl.num_programs(0) - 1)
    def _():
        zrow_ref[...] = jnp.zeros_like(zrow_ref)

        def per_expert(e, carry):
            n = cnt_ref[e]
            end = ((n + (MOE_TILE - 1)) // MOE_TILE) * MOE_TILE

            def fill(r, c):
                pltpu.make_async_copy(zrow_ref, xs_hbm.at[_row_tile(base_ref[e] + r)], zsem).start()
                return c

            def drain(r, c):
                pltpu.make_async_copy(zrow_ref, xs_hbm.at[_row_tile(0)], zsem).wait()
                return c

            lax.fori_loop(n, end, fill, 0)
            lax.fori_loop(n, end, drain, 0)
            return carry

        lax.fori_loop(0, N_EXPERTS, per_expert, 0)

        zblk_ref[...] = jnp.zeros_like(zblk_ref)
        tile_rows = MOE_TILE * ROW_TILES
        n_all = xs_hbm.shape[0] // tile_rows

        def tile_copy(ti):
            row0 = pl.multiple_of(ti * tile_rows, tile_rows)
            return pltpu.make_async_copy(zblk_ref, xs_hbm.at[pl.ds(row0, tile_rows)], zsem)

        def fill_tile(ti, c):
            tile_copy(ti).start()
            return c

        def drain_tile(ti, c):
            tile_copy(ti).wait()
            return c

        lax.fori_loop(nt_ref[0], n_all, fill_tile, 0)
        lax.fori_loop(nt_ref[0], n_all, drain_tile, 0)


def _dispatch(h2, slot_flat, base, counts, n_tiles, n_slots):
    t = h2.shape[0] // ROW_TILES
    tt = min(DISPATCH_TILE, t)
    smem_blk = lambda: pl.BlockSpec((tt * TOP_K,), lambda i, *_: (i,), memory_space=pltpu.SMEM)
    return pl.pallas_call(
        _dispatch_kernel,
        out_shape=jax.ShapeDtypeStruct((n_slots * ROW_TILES, LANES), F32),
        grid_spec=pltpu.PrefetchScalarGridSpec(
            num_scalar_prefetch=3,
            grid=(t // tt,),
            in_specs=[smem_blk(), pl.BlockSpec((tt * ROW_TILES, LANES), lambda i, *_: (i, 0))],
            out_specs=pl.BlockSpec(memory_space=pl.ANY),
            scratch_shapes=[pltpu.VMEM((ROW_TILES, LANES), F32), pltpu.VMEM((MOE_TILE * ROW_TILES, LANES), F32),
                            pltpu.SemaphoreType.DMA, pltpu.SemaphoreType.DMA],
        ),
        compiler_params=_cparams(("arbitrary",)),
        name="moe_dispatch",
    )(base, counts, n_tiles, slot_flat, h2)


def _ffn_kernel(te_ref, nt_ref, xs_ref, w1_ref, b1_ref, w2_ref, b2_ref, ys_ref):
    @pl.when(pl.program_id(0) < nt_ref[0])
    def _():
        tm = xs_ref.shape[0] // ROW_TILES
        x = jnp.concatenate([xs_ref[_row_slab(tm, j), :] for j in range(ROW_TILES)], axis=1)
        hid = _dot(x.astype(BF16), w1_ref[0]) + b1_ref[0]
        acts = []
        for j in range(hid.shape[1] // SWIGLU_BLOCK):
            glu = jnp.minimum(hid[:, j * SWIGLU_BLOCK:j * SWIGLU_BLOCK + LANES], SWIGLU_LIMIT)
            lin = jnp.clip(hid[:, j * SWIGLU_BLOCK + LANES:(j + 1) * SWIGLU_BLOCK], -SWIGLU_LIMIT, SWIGLU_LIMIT)
            acts.append((glu * jax.nn.sigmoid(SWIGLU_ALPHA * glu) * (lin + 1.0)).astype(BF16))
        y = _dot(jnp.concatenate(acts, axis=1), w2_ref[0]) + b2_ref[0]
        for j in range(ROW_TILES):
            ys_ref[_row_slab(tm, j), :] = y[:, j * LANES:(j + 1) * LANES]

    @pl.when(pl.program_id(0) >= nt_ref[0])
    def _():
        ys_ref[...] = jnp.zeros_like(ys_ref)


def _expert_ffn(xs, tile_expert, n_tiles, w1p, b1p, w2b, b2):
    n_slots = xs.shape[0] // ROW_TILES
    d = w1p.shape[1]
    tm = MOE_TILE
    f2 = w1p.shape[2]
    rows_blk = (tm * ROW_TILES, LANES)
    exp_blk = lambda i, te, nt: (te[i], 0, 0)
    return pl.pallas_call(
        _ffn_kernel,
        out_shape=jax.ShapeDtypeStruct(xs.shape, F32),
        grid_spec=pltpu.PrefetchScalarGridSpec(
            num_scalar_prefetch=2,
            grid=(n_slots // tm,),
            in_specs=[
                pl.BlockSpec(rows_blk, lambda i, te, nt: (jnp.minimum(i, nt[0] - 1), 0)),
                pl.BlockSpec((1, d, f2), exp_blk),
                pl.BlockSpec((1, 1, f2), exp_blk),
                pl.BlockSpec((1, f2 // 2, d), exp_blk),
                pl.BlockSpec((1, 1, d), exp_blk),
            ],
            out_specs=pl.BlockSpec(rows_blk, lambda i, te, nt: (i, 0)),
        ),
        compiler_params=_cparams(("arbitrary",)),
        name="moe_expert_ffn",
    )(tile_expert, n_tiles, xs, w1p, b1p, w2b, b2)


def _combine_kernel(slot_ref, slot_next_ref, w_ref, xm_ref, gate_ref, l2g_ref, l2b_ref,
                    ys_hbm, o_ref, buf_ref, sem):
    i = pl.program_id(0)
    tt = xm_ref.shape[0]
    cur = lax.rem(i, 2)

    def issue(sref, sl):
        def row_copy(r, kk, a):
            return pltpu.make_async_copy(ys_hbm.at[_row_tile(sref[a])], buf_ref.at[sl, kk, _row_tile(r)],
                                         sem.at[sl])
        _row_copy_groups(tt, row_copy)

    @pl.when(i == 0)
    def _():
        issue(slot_ref, 0)

    @pl.when(i + 1 < pl.num_programs(0))
    def _():
        issue(slot_next_ref, 1 - cur)

    for kk in range(TOP_K):
        pltpu.make_async_copy(ys_hbm.at[pl.ds(0, tt * ROW_TILES)], buf_ref.at[cur, kk], sem.at[cur]).wait()
    w = w_ref[...]
    cols = []
    for j in range(ROW_TILES):
        acc = w[:, 0:1] * buf_ref[cur, 0, _row_slab(tt, j), :]
        for kk in range(1, TOP_K):
            acc = acc + w[:, kk:kk + 1] * buf_ref[cur, kk, _row_slab(tt, j), :]
        cols.append(acc)
    pre = DEEPNORM_ALPHA * xm_ref[...] + gate_ref[0] * jnp.concatenate(cols, axis=1)
    o_ref[...] = _layer_norm(pre) * l2g_ref[...] + l2b_ref[...]


def _combine(ys, slot_flat, sel_w, x_mid, gate2, ln2_g, ln2_b, tiles_per_batch):
    t, d = x_mid.shape
    tt = min(COMBINE_TILE, t)
    n = t // tt
    return pl.pallas_call(
        _combine_kernel,
        out_shape=jax.ShapeDtypeStruct((t, d), F32),
        grid=(n,),
        in_specs=[
            pl.BlockSpec((tt * TOP_K,), lambda i: (i,), memory_space=pltpu.SMEM),
            pl.BlockSpec((tt * TOP_K,), lambda i: (jnp.minimum(i + 1, n - 1),), memory_space=pltpu.SMEM),
            pl.BlockSpec((tt, LANES), lambda i: (i, 0)),
            pl.BlockSpec((tt, d), lambda i: (i, 0)),
            pl.BlockSpec((1, 1, d), lambda i: (i // tiles_per_batch, 0, 0)),
            pl.BlockSpec((1, d), lambda i: (0, 0)),
            pl.BlockSpec((1, d), lambda i: (0, 0)),
            pl.BlockSpec(memory_space=pl.ANY),
        ],
        out_specs=pl.BlockSpec((tt, d), lambda i: (i, 0)),
        scratch_shapes=[pltpu.VMEM((2, TOP_K, tt * ROW_TILES, LANES), F32), pltpu.SemaphoreType.DMA((2,))],
        compiler_params=_cparams(("arbitrary",)),
        name="moe_combine",
    )(slot_flat, slot_flat, sel_w, x_mid, gate2, ln2_g, ln2_b, ys)


def kernel(x, c, ctx, c_ctx, ada_w, ada_b, w_in, gla_wa_f, gla_ba_f, gla_wa_b, gla_ba_b, gla_norm_g,
           hy_conv_w, hy_conv_b, hy_flt_w1, hy_flt_b1, hy_flt_w2, hy_flt_b2, hy_flt_wout, hy_flt_freq,
           hy_bias_d, w_out, ln1_g, ln1_b, router_w, router_b, exp_w1, exp_b1, exp_w2, exp_b2, ln2_g, ln2_b):
    batch, seq_len, d = x.shape
    lyr = 0
    ch = HY_WIDTH

    n_rows = 8 * ((batch + 1 + 7) // 8)
    cvec = jnp.zeros((n_rows, d), F32).at[:batch].set(c).at[batch].set(c_ctx)
    mod = _modulation(cvec, ada_w[lyr], ada_b[lyr][None, :])
    part = lambda rows, i: rows[:, None, i * d:(i + 1) * d]
    mod_x = mod[:batch]
    mod_c = jnp.broadcast_to(mod[batch:batch + 1], (batch, 6 * d))

    w = w_in[lyr]
    o_q, o_k, o_v, o_g = 0, GLA_KEY, 2 * GLA_KEY, 2 * GLA_KEY + GLA_VAL
    o_a = o_g + GLA_VAL
    o_h = o_a + 2 * GLA_RANK
    a_cols = jnp.pad(w[:, o_a:o_h], ((0, 0), (0, A_PAD - 2 * GLA_RANK)))
    q_cols = w[:, o_q:o_k] * (GLA_DK ** -0.5)
    w_main = jnp.concatenate([q_cols, w[:, o_k:o_a], a_cols], axis=1).astype(BF16)
    w_ctx = jnp.concatenate([q_cols, w[:, o_k:o_g], a_cols], axis=1).astype(BF16)
    w_hy_t = w[:, o_h:].T.astype(BF16)
    conv_w = jnp.broadcast_to(hy_conv_w[lyr][:, :, None], (HY_CONV, 3 * ch, LANES))
    conv_b = jnp.broadcast_to(hy_conv_b[lyr][:, None], (3 * ch, LANES))

    ctx_qkva = _input_projection(ctx, part(mod_c, 0), part(mod_c, 1), w_ctx)
    q, k, v, g, a_low, u_t = _input_projection(x, part(mod_x, 0), part(mod_x, 1), w_main, w_hy_t, conv_w, conv_b)

    wa = jnp.zeros((2, A_PAD, GLA_KEY), F32)
    wa = wa.at[0, :GLA_RANK].set(gla_wa_f[lyr]).at[1, GLA_RANK:2 * GLA_RANK].set(gla_wa_b[lyr]).astype(BF16)
    ba = jnp.stack([gla_ba_f[lyr], gla_ba_b[lyr]])[:, None, :]
    o_f, o_b = _gla_scan(ctx_qkva, (q, k, v, a_low), wa, ba)

    r1 = 2 * seq_len // LANES
    consts = _dft_constants(r1)
    zz, tn_rows, rate = _filter_inputs(seq_len)
    w1p = jnp.pad(hy_flt_w1[lyr], ((0, LANES - HY_EMB), (0, 0)))
    wo_t = hy_flt_wout[lyr].reshape(HY_FH, HY_ORDER, 2, ch).transpose(1, 2, 3, 0)
    filt = _filter_mlp(zz, tn_rows, rate, w1p, hy_flt_b1[lyr][None], hy_flt_w2[lyr], hy_flt_b2[lyr][None],
                       hy_flt_freq[lyr][None], wo_t)
    spectra = _filter_spectra(filt.reshape(HY_ORDER, ch, r1, LANES), consts, r1)
    y_hy = _hyena_conv(u_t.reshape(batch // 2, 2, 3 * ch, r1 // 2, LANES), spectra, hy_bias_d[lyr], consts, r1)
    y_hy_t = y_hy.reshape(batch, ch, seq_len)

    wo = w_out[lyr].astype(BF16)
    wr = jnp.pad(router_w[lyr], ((0, 0), (0, LANES - N_EXPERTS))).astype(BF16)
    rb = jnp.pad(router_b[lyr], (0, LANES - N_EXPERTS), constant_values=NEG_BIG)[None, :]
    norm_g = jnp.tile(gla_norm_g[lyr], GLA_HEADS)[None, :]
    x_mid, h2, sel_e, sel_r, sel_w, cnt = _output_projection(
        o_f, o_b, g, y_hy_t, x, part(mod_x, 2), part(mod_x, 3), part(mod_x, 4),
        wo[:GLA_VAL], wo[GLA_VAL:], norm_g, ln1_g[lyr][None], ln1_b[lyr][None], wr, rb)

    t = batch * seq_len
    counts = cnt[0, :N_EXPERTS]
    tiles_e = (counts + (MOE_TILE - 1)) // MOE_TILE
    tile_end = jnp.cumsum(tiles_e)
    base = ((tile_end - tiles_e) * MOE_TILE).astype(jnp.int32)
    n_tiles = tile_end[-1:].astype(jnp.int32)
    max_tiles = t * TOP_K // MOE_TILE + N_EXPERTS
    tile_ids = jnp.minimum(jnp.arange(max_tiles, dtype=jnp.int32), n_tiles[0] - 1)
    tile_expert = jnp.sum(tile_ids[:, None] >= tile_end[None, :], axis=1).astype(jnp.int32)
    e_flat = sel_e.reshape(t, LANES)[:, :TOP_K].reshape(t * TOP_K)
    r_flat = sel_r.reshape(t, LANES)[:, :TOP_K].reshape(t * TOP_K)
    slot_flat = base[e_flat] + r_flat

    w1p_e, w2_e = _expert_weight_layout(exp_w1[lyr], exp_w2[lyr])
    n_blk = 2 * D_EXPERT // SWIGLU_BLOCK
    b1p_e = exp_b1[lyr].reshape(N_EXPERTS, n_blk, LANES, 2).transpose(0, 1, 3, 2).reshape(N_EXPERTS, 1, 2 * D_EXPERT)
    xs = _dispatch(h2.reshape(t * ROW_TILES, LANES), slot_flat, base, counts, n_tiles, max_tiles * MOE_TILE)
    ys = _expert_ffn(xs, tile_expert, n_tiles, w1p_e, b1p_e, w2_e, exp_b2[lyr][:, None, :])
    out = _combine(ys, slot_flat, sel_w.reshape(t, LANES), x_mid.reshape(t, d), part(mod_x, 5),
                   ln2_g[lyr][None], ln2_b[lyr][None], seq_len // min(COMBINE_TILE, t))
    return out.reshape(batch, seq_len, d)
```

```python
import functools
import math

import numpy as np
import jax
import jax.numpy as jnp
from jax import lax
from jax.experimental import pallas as pl
from jax.experimental.pallas import tpu as pltpu

F32 = jnp.float32
BF16 = jnp.bfloat16

D_MODEL = 1024
DEPTH = 1
GRID_W = 64
LN_EPS = 1e-6
DEEPNORM_ALPHA = (2 * DEPTH) ** 0.25
GLA_HEADS = 4
GLA_DK = 64
GLA_DV = 128
GLA_KEY = GLA_HEADS * GLA_DK
GLA_VAL = GLA_HEADS * GLA_DV
GLA_RANK = 16
GLA_TAU = 16.0
HY_WIDTH = D_MODEL - GLA_VAL
HY_ORDER = 2
HY_CONV = 3
HY_EMB = 33
HY_FH = 64
HY_TARGET = 1e-2
HY_FAST = 0.3
HY_SLOW = 1.5
N_EXPERTS = 32
TOP_K = 4
D_EXPERT = D_MODEL
SWIGLU_ALPHA = 1.702
SWIGLU_LIMIT = 7.0

LANES = 128
VMEM_LIMIT = 56 * 1024 * 1024

TOK_TILE = 512
GLA_CHUNK = 128
GLA_BLOCK = 256
GLA_BATCH = 2
GLA_SAFE_EXPONENT = 80.0
A_PAD = LANES
HY_GROUP = 8
HY_CH_BLOCK = 8
FILT_TILE = 512
MOE_TILE = 512
FFN_TILES_PER_STEP = 2
COMBINE_TILE = 256
DMA_UNROLL = 8
ROW_TILES = D_MODEL // LANES


def _row_slab(n_rows, j):
    return pl.ds(j, n_rows, stride=ROW_TILES)


def _row_tile(r):
    return pl.ds(pl.multiple_of(r * ROW_TILES, ROW_TILES), ROW_TILES)
SWIGLU_BLOCK = 2 * LANES
NEG_BIG = -1e30


def _cparams(sem):
    return pltpu.CompilerParams(dimension_semantics=sem, vmem_limit_bytes=VMEM_LIMIT)


def _layer_norm(x):
    mu = jnp.mean(x, axis=-1, keepdims=True)
    xc = x - mu
    return xc * lax.rsqrt(jnp.mean(xc * xc, axis=-1, keepdims=True) + LN_EPS)


def _dot(a, b):
    return jnp.dot(a, b, preferred_element_type=F32)


def _dot_nt(a, b):
    return lax.dot_general(a, b, (((1,), (1,)), ((), ())), preferred_element_type=F32)


def _dot_tn(a, b):
    return lax.dot_general(a, b, (((0,), (0,)), ((), ())), preferred_element_type=F32)


def _mod_kernel(c_ref, w_ref, b_ref, o_ref):
    c = c_ref[...]
    s = c * jax.nn.sigmoid(c)
    o_ref[...] = _dot(s.astype(BF16), w_ref[...].astype(BF16)) + b_ref[...]


def _modulation(cvec, ada_w, ada_b):
    rows, d = cvec.shape
    n = ada_w.shape[1]
    tn = 1024
    return pl.pallas_call(
        _mod_kernel,
        out_shape=jax.ShapeDtypeStruct((rows, n), F32),
        grid=(n // tn,),
        in_specs=[
            pl.BlockSpec((rows, d), lambda j: (0, 0)),
            pl.BlockSpec((d, tn), lambda j: (0, j)),
            pl.BlockSpec((1, tn), lambda j: (0, j)),
        ],
        out_specs=pl.BlockSpec((rows, tn), lambda j: (0, j)),
        compiler_params=_cparams(("arbitrary",)),
        name="adaln_mod",
    )(cvec, ada_w, ada_b)


def _inproj_kernel(x_ref, sh_ref, sc_ref, wm_ref, *rest, with_hy):
    if with_hy:
        wh_ref, cw_ref, cb_ref, q_ref, k_ref, v_ref, g_ref, a_ref, u_ref = rest
    else:
        q_ref, k_ref, v_ref, a_ref = rest
    h = _layer_norm(x_ref[0]) * (1.0 + sc_ref[0]) + sh_ref[0]
    hb = h.astype(BF16)
    u = _dot(hb, wm_ref[...])
    q_ref[0] = u[:, :GLA_KEY].astype(BF16)
    k_ref[0] = u[:, GLA_KEY:2 * GLA_KEY].astype(BF16)
    v_ref[0] = u[:, 2 * GLA_KEY:2 * GLA_KEY + GLA_VAL].astype(BF16)
    off = 2 * GLA_KEY + GLA_VAL
    if with_hy:
        g_ref[0] = u[:, off:off + GLA_VAL].astype(BF16)
        off += GLA_VAL
    a_ref[0] = u[:, off:off + A_PAD]
    if with_hy:
        ut = _dot_nt(wh_ref[...], hb)
        lane = lax.broadcasted_iota(jnp.int32, (1, LANES), 1) % GRID_W
        not_first = (lane != 0).astype(F32)
        not_last = (lane != GRID_W - 1).astype(F32)
        tt = ut.shape[1]
        for j in range(tt // LANES):
            c = ut[:, j * LANES:(j + 1) * LANES]
            left = pltpu.roll(c, 1, axis=1) * not_first
            right = pltpu.roll(c, LANES - 1, axis=1) * not_last
            y = cw_ref[0] * left + cw_ref[1] * c + cw_ref[2] * right + cb_ref[...]
            u_ref[0, :, j * LANES:(j + 1) * LANES] = y.astype(BF16)


def _input_projection(x, shift, scale, w_main, w_hy_t=None, conv_w=None, conv_b=None):
    b, l, d = x.shape
    tt = min(TOK_TILE, l)
    with_hy = w_hy_t is not None
    nm = w_main.shape[1]
    tok = lambda width, dt: jax.ShapeDtypeStruct((b, l, width), dt)
    tok_spec = lambda width: pl.BlockSpec((1, tt, width), lambda i, j: (i, j, 0))
    in_specs = [
        pl.BlockSpec((1, tt, d), lambda i, j: (i, j, 0)),
        pl.BlockSpec((1, 1, d), lambda i, j: (i, 0, 0)),
        pl.BlockSpec((1, 1, d), lambda i, j: (i, 0, 0)),
        pl.BlockSpec((d, nm), lambda i, j: (0, 0)),
    ]
    args = [x, shift, scale, w_main]
    if with_hy:
        ch = w_hy_t.shape[0]
        in_specs += [
            pl.BlockSpec((ch, d), lambda i, j: (0, 0)),
            pl.BlockSpec((HY_CONV, ch, LANES), lambda i, j: (0, 0, 0)),
            pl.BlockSpec((ch, LANES), lambda i, j: (0, 0)),
        ]
        args += [w_hy_t, conv_w, conv_b]
        out_shape = [tok(GLA_KEY, BF16), tok(GLA_KEY, BF16), tok(GLA_VAL, BF16), tok(GLA_VAL, BF16),
                     tok(A_PAD, F32), jax.ShapeDtypeStruct((b, ch, l), BF16)]
        out_specs = [tok_spec(GLA_KEY), tok_spec(GLA_KEY), tok_spec(GLA_VAL), tok_spec(GLA_VAL),
                     tok_spec(A_PAD), pl.BlockSpec((1, ch, tt), lambda i, j: (i, 0, j))]
    else:
        out_shape = [tok(GLA_KEY, BF16), tok(GLA_KEY, BF16), tok(GLA_VAL, BF16), tok(A_PAD, F32)]
        out_specs = [tok_spec(GLA_KEY), tok_spec(GLA_KEY), tok_spec(GLA_VAL), tok_spec(A_PAD)]
    return pl.pallas_call(
        functools.partial(_inproj_kernel, with_hy=with_hy),
        out_shape=out_shape,
        grid=(b, l // tt),
        in_specs=in_specs,
        out_specs=out_specs,
        compiler_params=_cparams(("parallel", "arbitrary")),
        name="in_proj_hy" if with_hy else "in_proj_ctx",
    )(*args)


def _gla_kernel(*refs, n_ctx_blocks, chunk, n_batch, n_levels, safe_exponent):
    ins = [refs[0:8], refs[8:16]]
    wa_ref, ba_ref, cw_ref, cm_ref, hm_ref, bd_ref, dq_ref, dk_ref, lm_ref = refs[16:25]
    outs = refs[25:27]
    st_ref = refs[27]
    s = pl.program_id(1)
    blk = ins[0][4].shape[1]
    n_chunks = blk // chunk

    @pl.when(s == 0)
    def _():
        st_ref[...] = jnp.zeros_like(st_ref)

    def two_pass(m, g_hi, g_lo):
        return _dot(m, g_hi) + _dot(m, g_lo)

    is_ctx = s < n_ctx_blocks
    bdmask = bd_ref[...]
    chains = [(d, bi) for d in range(2) for bi in range(n_batch)]

    def pre_activation(d, bi):
        a = jnp.where(is_ctx, ins[d][3][bi], ins[d][7][bi])
        return _dot(a.astype(BF16), wa_ref[d]) + ba_ref[d]

    def advance(d, bi, z, single_reference):
        qc_ref, kc_ref, vc_ref, _, ql_ref, kl_ref, vl_ref, _ = ins[d]
        q = jnp.where(is_ctx, qc_ref[bi], ql_ref[bi]).astype(F32)
        k = jnp.where(is_ctx, kc_ref[bi], kl_ref[bi]).astype(F32)
        v = jnp.where(is_ctx, vc_ref[bi], vl_ref[bi])
        g = (jnp.minimum(z, 0.0) - jnp.log(1.0 + jnp.exp(-jnp.abs(z)))) * (1.0 / GLA_TAU)
        g_hi = g.astype(BF16)
        g_lo = (g - g_hi.astype(F32)).astype(BF16)
        cum = two_pass(cw_ref[d], g_hi, g_lo)
        edge = (lambda c: (c + 1) * chunk - 1) if d == 0 else (lambda c: c * chunk)
        tot = jnp.concatenate([jnp.broadcast_to(cum[edge(c):edge(c) + 1], (chunk, GLA_KEY))
                               for c in range(n_chunks)], axis=0)
        qs_all = (q * jnp.exp(cum)).astype(BF16)
        kst_all = (k * jnp.exp(tot - cum)).astype(BF16)
        dec_all = jnp.exp(tot)

        if single_reference:
            ks_all = (k * jnp.exp(-cum)).astype(BF16)
            cmask = cm_ref[d]
            rows_out = []
            for c in range(n_chunks):
                rows = slice(c * chunk, (c + 1) * chunk)
                parts = []
                for h in range(GLA_HEADS):
                    a_h = _dot_nt(qs_all[rows] * hm_ref[h], ks_all[rows]) * cmask
                    parts.append(_dot(a_h.astype(BF16), v[rows, h * GLA_DV:(h + 1) * GLA_DV]))
                rows_out.append(jnp.concatenate(parts, axis=1))
            o_intra = jnp.concatenate(rows_out, axis=0)
        else:
            def level(lv, acc):
                ql = (q * jnp.exp(two_pass(dq_ref[d, lv], g_hi, g_lo))).astype(BF16)
                kl = (k * jnp.exp(two_pass(dk_ref[d, lv], g_hi, g_lo))).astype(BF16)
                msk = lm_ref[d, lv]
                parts = []
                for h in range(GLA_HEADS):
                    a_h = _dot_nt(ql * hm_ref[h], kl) * msk
                    parts.append(_dot(a_h.astype(BF16), v[:, h * GLA_DV:(h + 1) * GLA_DV]))
                return acc + jnp.concatenate(parts, axis=1)
            o_intra = lax.fori_loop(0, n_levels, level, jnp.zeros((blk, GLA_VAL), F32))

        for n in range(n_chunks):
            ci = n if d == 0 else n_chunks - 1 - n
            rows = slice(ci * chunk, (ci + 1) * chunk)
            o = _dot_nt(qs_all[rows], st_ref[d, bi].astype(BF16)) + o_intra[rows]
            outs[d][bi, rows, :] = o.astype(outs[d].dtype)
            st_ref[d, bi] = (st_ref[d, bi] * dec_all[ci * chunk:ci * chunk + 1]
                             + _dot_tn(v[rows], kst_all[rows]) * bdmask)

    zs = [pre_activation(d, bi) for d, bi in chains]
    z_low = zs[0]
    for z in zs[1:]:
        z_low = jnp.minimum(z_low, z)
    bound = (jnp.maximum(-jnp.min(z_low), 0.0) + math.log(2.0)) * (chunk / GLA_TAU)

    def step(single_reference):
        def run():
            for (d, bi), z in zip(chains, zs):
                advance(d, bi, z, single_reference)
        return run

    lax.cond(bound < safe_exponent, step(True), step(False))


def _gla_level_tables(blk, chunk):
    sizes = []
    s = chunk // 2
    while s >= 1:
        sizes.append(s)
        s //= 2
    n_lv = len(sizes) + 1
    dq = np.zeros((2, n_lv, blk, blk), np.float32)
    dk = np.zeros((2, n_lv, blk, blk), np.float32)
    lm = np.zeros((2, n_lv, blk, blk), np.float32)
    idx = np.arange(blk)
    for lv, s in enumerate(sizes):
        blk_id = idx // (2 * s)
        right = (idx % (2 * s)) >= s
        b = blk_id * 2 * s + s
        same = blk_id[:, None] == blk_id[None, :]
        m = idx[None, :]
        dq[0, lv] = (right[:, None] & (m >= b[:, None]) & (m <= idx[:, None]))
        dk[0, lv] = (~right[:, None] & (m > idx[:, None]) & (m < b[:, None]))
        lm[0, lv] = same & right[:, None] & ~right[None, :]
        dq[1, lv] = (~right[:, None] & (m >= idx[:, None]) & (m < b[:, None]))
        dk[1, lv] = (right[:, None] & (m >= b[:, None]) & (m < idx[:, None]))
        lm[1, lv] = same & ~right[:, None] & right[None, :]
    lm[:, n_lv - 1] = np.eye(blk, dtype=np.float32)
    return dq, dk, lm, n_lv


def _gla_scan(ctx_qkva, lat_qkva, wa, ba, chunk=GLA_CHUNK, blk=GLA_BLOCK, safe_exponent=GLA_SAFE_EXPONENT):
    qc, kc, vc, ac = ctx_qkva
    ql, kl, vl, al = lat_qkva
    b, l, _ = ql.shape
    ctx_len = qc.shape[1]
    n_ctx = ctx_len // blk
    n_lat = l // blk
    c = chunk
    idx = np.arange(c)
    tri = np.stack([idx[:, None] >= idx[None, :], idx[:, None] <= idx[None, :]]).astype(np.float32)
    cw = np.stack([np.kron(np.eye(blk // c, dtype=np.float32), tri[dd]) for dd in range(2)])
    dq, dk, lm, n_levels = _gla_level_tables(blk, c)
    hmask = np.zeros((GLA_HEADS, 1, GLA_KEY), np.float32)
    for h in range(GLA_HEADS):
        hmask[h, 0, h * GLA_DK:(h + 1) * GLA_DK] = 1.0
    bd = (np.arange(GLA_VAL)[:, None] // GLA_DV == np.arange(GLA_KEY)[None, :] // GLA_DK).astype(np.float32)

    nb = GLA_BATCH if b % GLA_BATCH == 0 else 1

    def lat_block(d):
        def f(i, s):
            t = jnp.maximum(s - n_ctx, 0)
            return (i, t if d == 0 else n_lat - 1 - t, 0)
        return f

    def ctx_block(d):
        def f(i, s):
            t = jnp.minimum(s, n_ctx - 1)
            return (i, t if d == 0 else n_ctx - 1 - t, 0)
        return f

    widths = (GLA_KEY, GLA_KEY, GLA_VAL, A_PAD)
    dir_specs = lambda d: ([pl.BlockSpec((nb, blk, w), ctx_block(d)) for w in widths]
                           + [pl.BlockSpec((nb, blk, w), lat_block(d)) for w in widths])
    consts = [wa, ba, jnp.asarray(cw, BF16), jnp.asarray(tri, F32), jnp.asarray(hmask, BF16), jnp.asarray(bd, F32),
              jnp.asarray(dq, BF16), jnp.asarray(dk, BF16), jnp.asarray(lm, F32)]
    full = lambda arr: pl.BlockSpec(arr.shape, lambda i, s: (0,) * arr.ndim)
    o_sds = jax.ShapeDtypeStruct((b, l, GLA_VAL), BF16)
    dir_args = [qc, kc, vc, ac, ql, kl, vl, al]
    return pl.pallas_call(
        functools.partial(_gla_kernel, n_ctx_blocks=n_ctx, chunk=c, n_batch=nb, n_levels=n_levels,
                          safe_exponent=safe_exponent),
        out_shape=[o_sds, o_sds],
        grid=(b // nb, n_ctx + n_lat),
        in_specs=dir_specs(0) + dir_specs(1) + [full(arr) for arr in consts],
        out_specs=[pl.BlockSpec((nb, blk, GLA_VAL), lat_block(0)), pl.BlockSpec((nb, blk, GLA_VAL), lat_block(1))],
        scratch_shapes=[pltpu.VMEM((2, nb, GLA_VAL, GLA_KEY), F32)],
        compiler_params=_cparams(("parallel", "arbitrary")),
        name="gla_scan",
    )(*dir_args, *dir_args, *consts)


def _dft_constants(r1):
    n = r1 * LANES
    h = r1 // 2
    k1 = np.arange(r1)
    f1 = np.exp(-2j * np.pi * np.outer(k1, k1) / r1)
    f2 = np.exp(-2j * np.pi * np.outer(np.arange(LANES), np.arange(LANES)) / LANES)
    tw = np.exp(-2j * np.pi * np.outer(k1, np.arange(LANES)) / n)
    fa_c = np.block([[f1.real[:, :h], -f1.imag[:, :h]], [f1.imag[:, :h], f1.real[:, :h]]])
    fa_r = np.concatenate([f1.real, f1.imag], axis=0)
    gc = np.block([[f2.real, f2.imag], [-f2.imag, f2.real]])
    gci = np.block([[f2.real, -f2.imag], [f2.imag, f2.real]])
    fai = np.block([[f1.real[:h], f1.imag[:h]], [-f1.imag[:h], f1.real[:h]]]) / n
    tw_lane = np.tile(tw, (1, HY_GROUP))
    tw_row = np.tile(tw, (HY_GROUP, 1))
    f = lambda a: jnp.asarray(a, F32)
    return dict(fa_c=f(fa_c), fa_r=f(fa_r), gc=f(gc), gci=f(gci), fai=f(fai),
                twl_r=f(tw_lane.real), twl_i=f(tw_lane.imag), twr_r=f(tw_row.real), twr_i=f(tw_row.imag))


def _fwd_dft(rhs, fa, gc, twl_r, twl_i, r1):
    a = _dot(fa, rhs)
    ar, ai = a[:r1], a[r1:]
    br = (ar * twl_r - ai * twl_i).astype(BF16)
    bi = (ar * twl_i + ai * twl_r).astype(BF16)
    lhs = jnp.concatenate(
        [jnp.concatenate([br[:, c * LANES:(c + 1) * LANES], bi[:, c * LANES:(c + 1) * LANES]], axis=1)
         for c in range(HY_GROUP)], axis=0)
    return _dot(lhs, gc)


def _inv_dft(yr, yi, gci, fai, twr_r, twr_i, r1):
    lhs = jnp.concatenate([yr, yi], axis=1).astype(BF16)
    c = _dot(lhs, gci)
    cr, ci = c[:, :LANES], c[:, LANES:]
    dr = (cr * twr_r + ci * twr_i).astype(BF16)
    di = (ci * twr_r - cr * twr_i).astype(BF16)
    rhs = jnp.concatenate(
        [jnp.concatenate([dr[g * r1:(g + 1) * r1], di[g * r1:(g + 1) * r1]], axis=0)
         for g in range(HY_GROUP)], axis=1)
    return _dot(fai, rhs)


def _filter_mlp_kernel(zz_ref, tn_ref, rate_ref, w1_ref, b1_ref, w2_ref, b2_ref, fr_ref, wo_ref, o_ref):
    hp = lax.Precision.HIGHEST
    fr = fr_ref[...]
    hid = jnp.sin(fr * (jnp.dot(zz_ref[...], w1_ref[...], precision=hp, preferred_element_type=F32) + b1_ref[...]))
    hid = jnp.sin(fr * (jnp.dot(hid, w2_ref[...], precision=hp, preferred_element_type=F32) + b2_ref[...]))
    lt = tn_ref.shape[1]
    rate = jnp.concatenate([rate_ref[...]] * (lt // LANES), axis=1)
    window = jnp.exp(-tn_ref[0:1, :] * rate) * tn_ref[1:2, :]
    for o in range(HY_ORDER):
        ht = lax.dot_general(wo_ref[o, 0], hid, (((1,), (1,)), ((), ())), precision=hp,
                             preferred_element_type=F32)
        o_ref[o] = ht * window


def _filter_mlp(zz, tn_rows, rate, w1p, b1, w2, b2, freq, wo_t):
    n2l = zz.shape[0]
    l = n2l // 2
    lt = min(FILT_TILE, l)
    nt = l // lt
    ch = rate.shape[0]
    full = lambda shape: pl.BlockSpec(shape, lambda d, j: (0,) * len(shape))
    return pl.pallas_call(
        _filter_mlp_kernel,
        out_shape=jax.ShapeDtypeStruct((HY_ORDER, ch, n2l), F32),
        grid=(2, nt),
        in_specs=[
            pl.BlockSpec((lt, LANES), lambda d, j: (d * nt + j, 0)),
            pl.BlockSpec((8, lt), lambda d, j: (0, d * nt + j)),
            full((ch, LANES)),
            full((LANES, HY_FH)), full((1, HY_FH)), full((HY_FH, HY_FH)), full((1, HY_FH)), full((1, HY_FH)),
            pl.BlockSpec((HY_ORDER, 1, ch, HY_FH), lambda d, j: (0, d, 0, 0)),
        ],
        out_specs=pl.BlockSpec((HY_ORDER, ch, lt), lambda d, j: (0, 0, d * nt + j)),
        compiler_params=_cparams(("arbitrary", "arbitrary")),
        name="hyena_filter_mlp",
    )(zz, tn_rows, rate, w1p, b1, w2, b2, freq, wo_t)


def _filter_fft_kernel(f_ref, fa_ref, gc_ref, twl_r_ref, twl_i_ref, h_ref, *, r1):
    fa = fa_ref[...].astype(BF16)
    gc = gc_ref[...].astype(BF16)
    nc = f_ref.shape[1]
    for g0 in range(0, nc, HY_GROUP):
        rhs = jnp.concatenate([f_ref[0, g0 + c].astype(BF16) for c in range(HY_GROUP)], axis=1)
        x = _fwd_dft(rhs, fa, gc, twl_r_ref[...], twl_i_ref[...], r1)
        for c in range(HY_GROUP):
            h_ref[0, g0 + c] = x[c * r1:(c + 1) * r1].astype(h_ref.dtype)


def _filter_spectra(filt, consts, r1):
    order, ch = filt.shape[:2]
    nc = HY_CH_BLOCK
    full = lambda a: pl.BlockSpec(a.shape, lambda o, j: (0,) * a.ndim)
    cs = [consts["fa_r"], consts["gc"], consts["twl_r"], consts["twl_i"]]
    return pl.pallas_call(
        functools.partial(_filter_fft_kernel, r1=r1),
        out_shape=jax.ShapeDtypeStruct((order, ch, r1, 2 * LANES), BF16),
        grid=(order, ch // nc),
        in_specs=[pl.BlockSpec((1, nc, r1, LANES), lambda o, j: (o, j, 0, 0))] + [full(a) for a in cs],
        out_specs=pl.BlockSpec((1, nc, r1, 2 * LANES), lambda o, j: (o, j, 0, 0)),
        compiler_params=_cparams(("arbitrary", "arbitrary")),
        name="hyena_filter_fft",
    )(filt, *cs)


def _hyena_kernel(dbias_ref, v_ref, x1_ref, x2_ref, h_ref, fa_ref, gc_ref, gci_ref, fai_ref,
                  twl_r_ref, twl_i_ref, twr_r_ref, twr_i_ref, y_ref, *, r1):
    fa = fa_ref[...].astype(BF16)
    gc = gc_ref[...].astype(BF16)
    gci = gci_ref[...].astype(BF16)
    fai = fai_ref[...].astype(BF16)
    twl_r, twl_i = twl_r_ref[...], twl_i_ref[...]
    twr_r, twr_i = twr_r_ref[...], twr_i_ref[...]
    nc = v_ref.shape[2]
    half = r1 // 2
    c_base = pl.program_id(0) * nc

    def conv(sig, order, g0):
        rhs = jnp.concatenate(
            [jnp.concatenate([sig[c][0].astype(BF16), sig[c][1].astype(BF16)], axis=0)
             for c in range(HY_GROUP)], axis=1)
        x = _fwd_dft(rhs, fa, gc, twl_r, twl_i, r1)
        xr, xi = x[:, :LANES], x[:, LANES:]
        hh = jnp.concatenate([h_ref[order, g0 + c] for c in range(HY_GROUP)], axis=0).astype(F32)
        hr, hi = hh[:, :LANES], hh[:, LANES:]
        y = _inv_dft(xr * hr - xi * hi, xr * hi + xi * hr, gci, fai, twr_r, twr_i, r1)
        out = []
        for c in range(HY_GROUP):
            dcoef = dbias_ref[order, c_base + g0 + c]
            yc = y[:, c * LANES:(c + 1) * LANES]
            out.append([yc[:half] + dcoef * sig[c][0], yc[half:] + dcoef * sig[c][1]])
        return out

    for g0 in range(0, nc, HY_GROUP):
        v = [[v_ref[0, b, g0 + c].astype(F32) for b in range(2)] for c in range(HY_GROUP)]
        y1 = conv(v, 0, g0)
        z = [[x1_ref[0, b, g0 + c].astype(F32) * y1[c][b] for b in range(2)] for c in range(HY_GROUP)]
        y2 = conv(z, 1, g0)
        for c in range(HY_GROUP):
            for b in range(2):
                y_ref[0, b, g0 + c] = (x2_ref[0, b, g0 + c].astype(F32) * y2[c][b]).astype(y_ref.dtype)


def _hyena_conv(u_t, spectra, d_bias, consts, r1):
    bp, _, ch3, half, _ = u_t.shape
    ch = ch3 // 3
    nc = HY_CH_BLOCK
    nblk = ch // nc
    names = ["fa_c", "gc", "gci", "fai", "twl_r", "twl_i", "twr_r", "twr_i"]
    cs = [consts[k] for k in names]
    full = lambda a: pl.BlockSpec(a.shape, lambda j, p: (0,) * a.ndim)
    part = lambda k: pl.BlockSpec((1, 2, nc, half, LANES), lambda j, p: (p, 0, k * nblk + j, 0, 0))
    return pl.pallas_call(
        functools.partial(_hyena_kernel, r1=r1),
        out_shape=jax.ShapeDtypeStruct((bp, 2, ch, half, LANES), BF16),
        grid=(nblk, bp),
        in_specs=[pl.BlockSpec(memory_space=pltpu.SMEM), part(0), part(1), part(2),
                  pl.BlockSpec((HY_ORDER, nc, r1, 2 * LANES), lambda j, p: (0, j, 0, 0))] + [full(a) for a in cs],
        out_specs=pl.BlockSpec((1, 2, nc, half, LANES), lambda j, p: (p, 0, j, 0, 0)),
        compiler_params=_cparams(("arbitrary", "arbitrary")),
        name="hyena_conv",
    )(d_bias, u_t, u_t, u_t, spectra, *cs)


def _filter_inputs(l):
    n = jnp.arange(2 * l, dtype=jnp.int32)
    t = jnp.where(n < l, n, 2 * l - n).astype(F32)
    valid = (n != l).astype(F32)
    t_norm = t / (l - 1)
    bands = (HY_EMB - 1) // 2
    f = jnp.linspace(1e-4, bands - 1, bands, dtype=F32)
    ang = (2.0 * math.pi * t / l)[:, None] * f[None, :]
    z = jnp.concatenate([t_norm[:, None], jnp.cos(ang), -jnp.sin(ang)], -1)
    zz = jnp.pad(z, ((0, 0), (0, LANES - HY_EMB)))
    tn_rows = jnp.zeros((8, 2 * l), F32).at[0].set(t_norm).at[1].set(valid)
    deltas = jnp.linspace(math.log(HY_TARGET) / HY_SLOW, math.log(HY_TARGET) / HY_FAST, HY_WIDTH, dtype=F32)
    rate = jnp.broadcast_to(jnp.abs(deltas)[:, None], (HY_WIDTH, LANES))
    return zz, tn_rows, rate


def _outproj_kernel(of_ref, ob_ref, g_ref, yh_ref, x_ref, gate_ref, sh_ref, sc_ref,
                    wg_ref, wh_ref, ng_ref, l1g_ref, l1b_ref, wr_ref, rb_ref, ltri_ref,
                    xm_ref, h2_ref, se_ref, sr_ref, sw_ref, cnt_ref, carry_ref):
    @pl.when((pl.program_id(0) == 0) & (pl.program_id(1) == 0))
    def _():
        carry_ref[...] = jnp.zeros_like(carry_ref)

    o = of_ref[0].astype(F32) + ob_ref[0].astype(F32)
    g = g_ref[0].astype(F32)
    parts = []
    for h in range(GLA_HEADS):
        oh = o[:, h * GLA_DV:(h + 1) * GLA_DV]
        parts.append(oh * lax.rsqrt(jnp.mean(oh * oh, axis=-1, keepdims=True) + LN_EPS))
    y_gla = jnp.concatenate(parts, axis=1) * ng_ref[...] * (g * jax.nn.sigmoid(g))
    f = _dot(y_gla.astype(BF16), wg_ref[...]) + _dot_tn(yh_ref[0], wh_ref[...])
    x_mid = _layer_norm(DEEPNORM_ALPHA * x_ref[0] + gate_ref[0] * f) * l1g_ref[...] + l1b_ref[...]
    xm_ref[0] = x_mid
    h2 = _layer_norm(x_mid) * (1.0 + sc_ref[0]) + sh_ref[0]
    for j in range(ROW_TILES):
        h2_ref[0, _row_slab(h2.shape[0], j), :] = h2[:, j * LANES:(j + 1) * LANES]

    logits = _dot(h2.astype(BF16), wr_ref[...]) + rb_ref[...]
    lane = lax.broadcasted_iota(jnp.int32, logits.shape, 1).astype(F32)
    hits, idxs, exps = [], [], []
    m0 = None
    for _ in range(TOP_K):
        m = jnp.max(logits, axis=-1, keepdims=True)
        idx = jnp.min(jnp.where(logits == m, lane, float(LANES)), axis=-1, keepdims=True)
        hit = lane == idx
        m0 = m if m0 is None else m0
        hits.append(hit)
        idxs.append(idx)
        exps.append(jnp.exp(m - m0))
        logits = jnp.where(hit, NEG_BIG, logits)
    denom = exps[0]
    sel = jnp.where(hits[0], 1.0, 0.0)
    for kk in range(1, TOP_K):
        denom = denom + exps[kk]
        sel = sel + jnp.where(hits[kk], 1.0, 0.0)
    rank_all = _dot(ltri_ref[...], sel.astype(BF16)) + carry_ref[0:1, :]
    carry_ref[0:1, :] = carry_ref[0:1, :] + jnp.sum(sel, axis=0, keepdims=True)
    cnt_ref[...] = jnp.broadcast_to(carry_ref[0:1, :], cnt_ref.shape).astype(jnp.int32)
    se = jnp.zeros(logits.shape, F32)
    sr = jnp.zeros(logits.shape, F32)
    sw = jnp.zeros(logits.shape, F32)
    for kk in range(TOP_K):
        rk = jnp.sum(jnp.where(hits[kk], rank_all, 0.0), axis=-1, keepdims=True)
        col = lane == float(kk)
        se = jnp.where(col, idxs[kk], se)
        sr = jnp.where(col, rk, sr)
        sw = jnp.where(col, exps[kk] / denom, sw)
    se_ref[0] = se.astype(jnp.int32)
    sr_ref[0] = sr.astype(jnp.int32)
    sw_ref[0] = sw


def _output_projection(o_f, o_b, g, y_hy_t, x, gate1, shift2, scale2, w_gla, w_hy, norm_g, ln1_g, ln1_b, wr, rb):
    b, l, d = x.shape
    tt = TOK_TILE
    tok = lambda w: pl.BlockSpec((1, tt, w), lambda i, j: (i, j, 0))
    row = lambda: pl.BlockSpec((1, 1, d), lambda i, j: (i, 0, 0))
    full = lambda a: pl.BlockSpec(a.shape, lambda i, j: (0,) * a.ndim)
    ltri = jnp.asarray(np.tril(np.ones((tt, tt), np.float32), -1), BF16)
    consts = [w_gla, w_hy, norm_g, ln1_g, ln1_b, wr, rb, ltri]
    lane_i = jax.ShapeDtypeStruct((b, l, LANES), jnp.int32)
    return pl.pallas_call(
        _outproj_kernel,
        out_shape=[jax.ShapeDtypeStruct((b, l, d), F32), jax.ShapeDtypeStruct((b, l * ROW_TILES, LANES), F32),
                   lane_i, lane_i, jax.ShapeDtypeStruct((b, l, LANES), F32),
                   jax.ShapeDtypeStruct((8, LANES), jnp.int32)],
        grid=(b, l // tt),
        in_specs=[
            tok(GLA_VAL), tok(GLA_VAL), tok(GLA_VAL),
            pl.BlockSpec((1, HY_WIDTH, tt), lambda i, j: (i, 0, j)),
            tok(d), row(), row(), row(),
        ] + [full(a) for a in consts],
        out_specs=[tok(d), pl.BlockSpec((1, tt * ROW_TILES, LANES), lambda i, j: (i, j, 0)),
                   tok(LANES), tok(LANES), tok(LANES),
                   pl.BlockSpec((8, LANES), lambda i, j: (0, 0))],
        scratch_shapes=[pltpu.VMEM((8, LANES), F32)],
        compiler_params=_cparams(("arbitrary", "arbitrary")),
        name="out_proj_router",
    )(o_f, o_b, g, y_hy_t, x, gate1, shift2, scale2, *consts)


def _expert_weight_layout(w1_ref, w2_ref, p_ref, w1o_ref, w2o_ref):
    p = p_ref[...]
    for j in range(w1_ref.shape[2] // SWIGLU_BLOCK):
        cols = slice(j * SWIGLU_BLOCK, (j + 1) * SWIGLU_BLOCK)
        w1o_ref[0, :, cols] = _dot(w1_ref[0, :, cols].astype(BF16), p).astype(BF16)
    w2o_ref[0] = w2_ref[0].astype(BF16)


def _row_copy_groups(n_tokens, make_copy):
    def group(gidx, carry):
        for u in range(DMA_UNROLL):
            r = gidx * DMA_UNROLL + u
            for kk in range(TOP_K):
                make_copy(r, kk, r * TOP_K + kk).start(priority=kk % 2)
        return carry
    lax.fori_loop(0, n_tokens // DMA_UNROLL, group, 0)


def _dispatch_kernel(base_ref, cnt_ref, nt_ref, slot_ref, h2_ref, w1_ref, w2_ref, p_ref,
                     xs_hbm, w1o_ref, w2o_ref, zrow_ref, zblk_ref, sem, zsem):
    tt = h2_ref.shape[0] // ROW_TILES

    def row_copy(r, kk, a):
        return pltpu.make_async_copy(h2_ref.at[_row_tile(r)], xs_hbm.at[_row_tile(slot_ref[a])], sem)

    _row_copy_groups(tt, row_copy)
    _expert_weight_layout(w1_ref, w2_ref, p_ref, w1o_ref, w2o_ref)
    for _ in range(TOP_K):
        pltpu.make_async_copy(h2_ref, xs_hbm.at[pl.ds(0, tt * ROW_TILES)], sem).wait()

    @pl.when(pl.program_id(0) == pl.num_programs(0) - 1)
    def _():
        zrow_ref[...] = jnp.zeros_like(zrow_ref)

        def per_expert(e, carry):
            n = cnt_ref[e]
            end = ((n + (MOE_TILE - 1)) // MOE_TILE) * MOE_TILE

            def fill(r, c):
                pltpu.make_async_copy(zrow_ref, xs_hbm.at[_row_tile(base_ref[e] + r)], zsem).start()
                return c

            def drain(r, c):
                pltpu.make_async_copy(zrow_ref, xs_hbm.at[_row_tile(0)], zsem).wait()
                return c

            lax.fori_loop(n, end, fill, 0)
            lax.fori_loop(n, end, drain, 0)
            return carry

        lax.fori_loop(0, N_EXPERTS, per_expert, 0)

        zblk_ref[...] = jnp.zeros_like(zblk_ref)
        tile_rows = MOE_TILE * ROW_TILES
        n_all = xs_hbm.shape[0] // tile_rows

        def tile_copy(ti):
            row0 = pl.multiple_of(ti * tile_rows, tile_rows)
            return pltpu.make_async_copy(zblk_ref, xs_hbm.at[pl.ds(row0, tile_rows)], zsem)

        def fill_tile(ti, c):
            tile_copy(ti).start()
            return c

        def drain_tile(ti, c):
            tile_copy(ti).wait()
            return c

        lax.fori_loop(nt_ref[0], n_all, fill_tile, 0)
        lax.fori_loop(nt_ref[0], n_all, drain_tile, 0)


def _dispatch(h2, slot_flat, base, counts, n_tiles, n_slots, w1, w2):
    t = h2.shape[0] // ROW_TILES
    ne, d, f2 = w1.shape
    assert t % (ne * DMA_UNROLL) == 0, "one dispatch step per expert"
    tt = t // ne
    src = np.concatenate([np.arange(0, SWIGLU_BLOCK, 2), np.arange(1, SWIGLU_BLOCK, 2)])
    perm = np.zeros((SWIGLU_BLOCK, SWIGLU_BLOCK), np.float32)
    perm[src, np.arange(SWIGLU_BLOCK)] = 1.0
    exp_blk = lambda shape: pl.BlockSpec((1,) + shape, lambda i, *_: (i, 0, 0))
    return pl.pallas_call(
        _dispatch_kernel,
        out_shape=[jax.ShapeDtypeStruct((n_slots * ROW_TILES, LANES), F32),
                   jax.ShapeDtypeStruct(w1.shape, BF16), jax.ShapeDtypeStruct(w2.shape, BF16)],
        grid_spec=pltpu.PrefetchScalarGridSpec(
            num_scalar_prefetch=3,
            grid=(ne,),
            in_specs=[pl.BlockSpec((tt * TOP_K,), lambda i, *_: (i,), memory_space=pltpu.SMEM),
                      pl.BlockSpec((tt * ROW_TILES, LANES), lambda i, *_: (i, 0)),
                      exp_blk((d, f2)), exp_blk(w2.shape[1:]),
                      pl.BlockSpec((SWIGLU_BLOCK, SWIGLU_BLOCK), lambda i, *_: (0, 0))],
            out_specs=[pl.BlockSpec(memory_space=pl.ANY), exp_blk((d, f2)), exp_blk(w2.shape[1:])],
            scratch_shapes=[pltpu.VMEM((ROW_TILES, LANES), F32), pltpu.VMEM((MOE_TILE * ROW_TILES, LANES), F32),
                            pltpu.SemaphoreType.DMA, pltpu.SemaphoreType.DMA],
        ),
        compiler_params=_cparams(("arbitrary",)),
        name="moe_dispatch",
    )(base, counts, n_tiles, slot_flat, h2, w1, w2, jnp.asarray(perm, BF16))


def _ffn_kernel(te_ref, nt_ref, xs_ref, *refs):
    ys_ref = refs[-1]
    tm = MOE_TILE
    rows = tm * ROW_TILES
    n_here = jnp.clip(nt_ref[0] - pl.program_id(0) * FFN_TILES_PER_STEP, 0, FFN_TILES_PER_STEP)

    def tile(t):
        w1_ref, b1_ref, w2_ref, b2_ref = refs[4 * t:4 * t + 4]
        x = jnp.concatenate([xs_ref[pl.ds(t * rows + j, tm, stride=ROW_TILES), :] for j in range(ROW_TILES)], axis=1)
        hid = _dot(x.astype(BF16), w1_ref[0]) + b1_ref[0]
        acts = []
        for j in range(hid.shape[1] // SWIGLU_BLOCK):
            glu = jnp.minimum(hid[:, j * SWIGLU_BLOCK:j * SWIGLU_BLOCK + LANES], SWIGLU_LIMIT)
            lin = jnp.clip(hid[:, j * SWIGLU_BLOCK + LANES:(j + 1) * SWIGLU_BLOCK], -SWIGLU_LIMIT, SWIGLU_LIMIT)
            acts.append((glu * jax.nn.sigmoid(SWIGLU_ALPHA * glu) * (lin + 1.0)).astype(BF16))
        y = _dot(jnp.concatenate(acts, axis=1), w2_ref[0]) + b2_ref[0]
        for j in range(ROW_TILES):
            ys_ref[pl.ds(t * rows + j, tm, stride=ROW_TILES), :] = y[:, j * LANES:(j + 1) * LANES]

    for n_valid in range(FFN_TILES_PER_STEP + 1):
        @pl.when(n_here == n_valid)
        def _(n_valid=n_valid):
            for t in range(n_valid):
                tile(t)
            for t in range(n_valid, FFN_TILES_PER_STEP):
                ys_ref[t * rows:(t + 1) * rows, :] = jnp.zeros((rows, LANES), F32)


def _expert_ffn(xs, tile_expert, n_tiles, w1p, b1p, w2b, b2):
    n_slots = xs.shape[0] // ROW_TILES
    d = w1p.shape[1]
    tps = FFN_TILES_PER_STEP
    tm = MOE_TILE
    f2 = w1p.shape[2]
    assert n_slots % (tm * tps) == 0
    rows_blk = (tps * tm * ROW_TILES, LANES)
    weights = []
    for t in range(tps):
        exp_blk = lambda i, te, nt, t=t: (te[i * tps + t], 0, 0)
        weights += [pl.BlockSpec((1, d, f2), exp_blk), pl.BlockSpec((1, 1, f2), exp_blk),
                    pl.BlockSpec((1, f2 // 2, d), exp_blk), pl.BlockSpec((1, 1, d), exp_blk)]
    return pl.pallas_call(
        _ffn_kernel,
        out_shape=jax.ShapeDtypeStruct(xs.shape, F32),
        grid_spec=pltpu.PrefetchScalarGridSpec(
            num_scalar_prefetch=2,
            grid=(n_slots // (tm * tps),),
            in_specs=[
                pl.BlockSpec(rows_blk, lambda i, te, nt: (jnp.minimum(i, (nt[0] - 1) // tps), 0)),
            ] + weights,
            out_specs=pl.BlockSpec(rows_blk, lambda i, te, nt: (i, 0)),
        ),
        compiler_params=_cparams(("arbitrary",)),
        name="moe_expert_ffn",
    )(tile_expert, n_tiles, xs, *([w1p, b1p, w2b, b2] * tps))


def _combine_kernel(slot_ref, slot_next_ref, w_ref, xm_ref, gate_ref, l2g_ref, l2b_ref,
                    ys_hbm, o_ref, buf_ref, sem):
    i = pl.program_id(0)
    tt = xm_ref.shape[0]
    cur = lax.rem(i, 2)

    def issue(sref, sl):
        def row_copy(r, kk, a):
            return pltpu.make_async_copy(ys_hbm.at[_row_tile(sref[a])], buf_ref.at[sl, kk, _row_tile(r)],
                                         sem.at[sl])
        _row_copy_groups(tt, row_copy)

    @pl.when(i == 0)
    def _():
        issue(slot_ref, 0)

    @pl.when(i + 1 < pl.num_programs(0))
    def _():
        issue(slot_next_ref, 1 - cur)

    for kk in range(TOP_K):
        pltpu.make_async_copy(ys_hbm.at[pl.ds(0, tt * ROW_TILES)], buf_ref.at[cur, kk], sem.at[cur]).wait()
    w = w_ref[...]
    cols = []
    for j in range(ROW_TILES):
        acc = w[:, 0:1] * buf_ref[cur, 0, _row_slab(tt, j), :]
        for kk in range(1, TOP_K):
            acc = acc + w[:, kk:kk + 1] * buf_ref[cur, kk, _row_slab(tt, j), :]
        cols.append(acc)
    pre = DEEPNORM_ALPHA * xm_ref[...] + gate_ref[0] * jnp.concatenate(cols, axis=1)
    o_ref[...] = _layer_norm(pre) * l2g_ref[...] + l2b_ref[...]


def _combine(ys, slot_flat, sel_w, x_mid, gate2, ln2_g, ln2_b, tiles_per_batch):
    t, d = x_mid.shape
    tt = min(COMBINE_TILE, t)
    n = t // tt
    return pl.pallas_call(
        _combine_kernel,
        out_shape=jax.ShapeDtypeStruct((t, d), F32),
        grid=(n,),
        in_specs=[
            pl.BlockSpec((tt * TOP_K,), lambda i: (i,), memory_space=pltpu.SMEM),
            pl.BlockSpec((tt * TOP_K,), lambda i: (jnp.minimum(i + 1, n - 1),), memory_space=pltpu.SMEM),
            pl.BlockSpec((tt, LANES), lambda i: (i, 0)),
            pl.BlockSpec((tt, d), lambda i: (i, 0)),
            pl.BlockSpec((1, 1, d), lambda i: (i // tiles_per_batch, 0, 0)),
            pl.BlockSpec((1, d), lambda i: (0, 0)),
            pl.BlockSpec((1, d), lambda i: (0, 0)),
            pl.BlockSpec(memory_space=pl.ANY),
        ],
        out_specs=pl.BlockSpec((tt, d), lambda i: (i, 0)),
        scratch_shapes=[pltpu.VMEM((2, TOP_K, tt * ROW_TILES, LANES), F32), pltpu.SemaphoreType.DMA((2,))],
        compiler_params=_cparams(("arbitrary",)),
        name="moe_combine",
    )(slot_flat, slot_flat, sel_w, x_mid, gate2, ln2_g, ln2_b, ys)


def kernel(x, c, ctx, c_ctx, ada_w, ada_b, w_in, gla_wa_f, gla_ba_f, gla_wa_b, gla_ba_b, gla_norm_g,
           hy_conv_w, hy_conv_b, hy_flt_w1, hy_flt_b1, hy_flt_w2, hy_flt_b2, hy_flt_wout, hy_flt_freq,
           hy_bias_d, w_out, ln1_g, ln1_b, router_w, router_b, exp_w1, exp_b1, exp_w2, exp_b2, ln2_g, ln2_b):
    batch, seq_len, d = x.shape
    lyr = 0
    ch = HY_WIDTH

    n_rows = 8 * ((batch + 1 + 7) // 8)
    cvec = jnp.zeros((n_rows, d), F32).at[:batch].set(c).at[batch].set(c_ctx)
    mod = _modulation(cvec, ada_w[lyr], ada_b[lyr][None, :])
    part = lambda rows, i: rows[:, None, i * d:(i + 1) * d]
    mod_x = mod[:batch]
    mod_c = jnp.broadcast_to(mod[batch:batch + 1], (batch, 6 * d))

    w = w_in[lyr]
    o_q, o_k, o_v, o_g = 0, GLA_KEY, 2 * GLA_KEY, 2 * GLA_KEY + GLA_VAL
    o_a = o_g + GLA_VAL
    o_h = o_a + 2 * GLA_RANK
    a_cols = jnp.pad(w[:, o_a:o_h], ((0, 0), (0, A_PAD - 2 * GLA_RANK)))
    q_cols = w[:, o_q:o_k] * (GLA_DK ** -0.5)
    w_main = jnp.concatenate([q_cols, w[:, o_k:o_a], a_cols], axis=1).astype(BF16)
    w_ctx = jnp.concatenate([q_cols, w[:, o_k:o_g], a_cols], axis=1).astype(BF16)
    w_hy_t = w[:, o_h:].T.astype(BF16)
    conv_w = jnp.broadcast_to(hy_conv_w[lyr][:, :, None], (HY_CONV, 3 * ch, LANES))
    conv_b = jnp.broadcast_to(hy_conv_b[lyr][:, None], (3 * ch, LANES))

    ctx_qkva = _input_projection(ctx, part(mod_c, 0), part(mod_c, 1), w_ctx)
    q, k, v, g, a_low, u_t = _input_projection(x, part(mod_x, 0), part(mod_x, 1), w_main, w_hy_t, conv_w, conv_b)

    wa = jnp.zeros((2, A_PAD, GLA_KEY), F32)
    wa = wa.at[0, :GLA_RANK].set(gla_wa_f[lyr]).at[1, GLA_RANK:2 * GLA_RANK].set(gla_wa_b[lyr]).astype(BF16)
    ba = jnp.stack([gla_ba_f[lyr], gla_ba_b[lyr]])[:, None, :]
    o_f, o_b = _gla_scan(ctx_qkva, (q, k, v, a_low), wa, ba)

    r1 = 2 * seq_len // LANES
    consts = _dft_constants(r1)
    zz, tn_rows, rate = _filter_inputs(seq_len)
    w1p = jnp.pad(hy_flt_w1[lyr], ((0, LANES - HY_EMB), (0, 0)))
    wo_t = hy_flt_wout[lyr].reshape(HY_FH, HY_ORDER, 2, ch).transpose(1, 2, 3, 0)
    filt = _filter_mlp(zz, tn_rows, rate, w1p, hy_flt_b1[lyr][None], hy_flt_w2[lyr], hy_flt_b2[lyr][None],
                       hy_flt_freq[lyr][None], wo_t)
    spectra = _filter_spectra(filt.reshape(HY_ORDER, ch, r1, LANES), consts, r1)
    y_hy = _hyena_conv(u_t.reshape(batch // 2, 2, 3 * ch, r1 // 2, LANES), spectra, hy_bias_d[lyr], consts, r1)
    y_hy_t = y_hy.reshape(batch, ch, seq_len)

    wo = w_out[lyr].astype(BF16)
    wr = jnp.pad(router_w[lyr], ((0, 0), (0, LANES - N_EXPERTS))).astype(BF16)
    rb = jnp.pad(router_b[lyr], (0, LANES - N_EXPERTS), constant_values=NEG_BIG)[None, :]
    norm_g = jnp.tile(gla_norm_g[lyr], GLA_HEADS)[None, :]
    x_mid, h2, sel_e, sel_r, sel_w, cnt = _output_projection(
        o_f, o_b, g, y_hy_t, x, part(mod_x, 2), part(mod_x, 3), part(mod_x, 4),
        wo[:GLA_VAL], wo[GLA_VAL:], norm_g, ln1_g[lyr][None], ln1_b[lyr][None], wr, rb)

    t = batch * seq_len
    counts = cnt[0, :N_EXPERTS]
    tiles_e = (counts + (MOE_TILE - 1)) // MOE_TILE
    tile_end = jnp.cumsum(tiles_e)
    base = ((tile_end - tiles_e) * MOE_TILE).astype(jnp.int32)
    n_tiles = tile_end[-1:].astype(jnp.int32)
    max_tiles = t * TOP_K // MOE_TILE + N_EXPERTS
    tile_ids = jnp.minimum(jnp.arange(max_tiles, dtype=jnp.int32), n_tiles[0] - 1)
    tile_expert = jnp.sum(tile_ids[:, None] >= tile_end[None, :], axis=1).astype(jnp.int32)
    e_flat = sel_e.reshape(t, LANES)[:, :TOP_K].reshape(t * TOP_K)
    r_flat = sel_r.reshape(t, LANES)[:, :TOP_K].reshape(t * TOP_K)
    slot_flat = base[e_flat] + r_flat

    n_blk = 2 * D_EXPERT // SWIGLU_BLOCK
    b1p_e = exp_b1[lyr].reshape(N_EXPERTS, n_blk, LANES, 2).transpose(0, 1, 3, 2).reshape(N_EXPERTS, 1, 2 * D_EXPERT)
    xs, w1p_e, w2_e = _dispatch(h2.reshape(t * ROW_TILES, LANES), slot_flat, base, counts, n_tiles,
                                max_tiles * MOE_TILE, exp_w1[lyr], exp_w2[lyr])
    ys = _expert_ffn(xs, tile_expert, n_tiles, w1p_e, b1p_e, w2_e, exp_b2[lyr][:, None, :])
    out = _combine(ys, slot_flat, sel_w.reshape(t, LANES), x_mid.reshape(t, d), part(mod_x, 5),
                   ln2_g[lyr][None], ln2_b[lyr][None], seq_len // min(COMBINE_TILE, t))
    return out.reshape(batch, seq_len, d)
```

```python
import functools
import math

import numpy as np
import jax
import jax.numpy as jnp
from jax import lax
from jax.experimental import pallas as pl
from jax.experimental.pallas import tpu as pltpu

F32 = jnp.float32
BF16 = jnp.bfloat16

D_MODEL = 1024
DEPTH = 1
GRID_W = 64
LN_EPS = 1e-6
DEEPNORM_ALPHA = (2 * DEPTH) ** 0.25
GLA_HEADS = 4
GLA_DK = 64
GLA_DV = 128
GLA_KEY = GLA_HEADS * GLA_DK
GLA_VAL = GLA_HEADS * GLA_DV
GLA_RANK = 16
GLA_TAU = 16.0
HY_WIDTH = D_MODEL - GLA_VAL
HY_ORDER = 2
HY_CONV = 3
HY_EMB = 33
HY_FH = 64
HY_TARGET = 1e-2
HY_FAST = 0.3
HY_SLOW = 1.5
N_EXPERTS = 32
TOP_K = 4
D_EXPERT = D_MODEL
SWIGLU_ALPHA = 1.702
SWIGLU_LIMIT = 7.0

LANES = 128
VMEM_LIMIT = 56 * 1024 * 1024

TOK_TILE = 512
PROJ_SUB_TILE = 256
GLA_CHUNK = 128
GLA_BLOCK = 256
GLA_BATCH = 4
GLA_SAFE_EXPONENT = 80.0
A_PAD = LANES
HY_GROUP = 8
HY_CH_BLOCK = 16
FILT_TILE = 512
HY_EMB_PAD = 40
MOE_TILE = 512
FFN_TILES_PER_STEP = 2
COMBINE_TILE = 256
DMA_UNROLL = 8
ROW_TILES = D_MODEL // LANES


def _row_slab(n_rows, j):
    return pl.ds(j, n_rows, stride=ROW_TILES)


def _row_tile(r):
    return pl.ds(pl.multiple_of(r * ROW_TILES, ROW_TILES), ROW_TILES)
SWIGLU_BLOCK = 2 * LANES
NEG_BIG = -1e30


def _cparams(sem):
    return pltpu.CompilerParams(dimension_semantics=sem, vmem_limit_bytes=VMEM_LIMIT)


def _layer_norm(x):
    mu = jnp.mean(x, axis=-1, keepdims=True)
    xc = x - mu
    return xc * lax.rsqrt(jnp.mean(xc * xc, axis=-1, keepdims=True) + LN_EPS)


def _dot(a, b):
    return jnp.dot(a, b, preferred_element_type=F32)


def _dot_nt(a, b):
    return lax.dot_general(a, b, (((1,), (1,)), ((), ())), preferred_element_type=F32)


def _dot_tn(a, b):
    return lax.dot_general(a, b, (((0,), (0,)), ((), ())), preferred_element_type=F32)


def _mod_kernel(c_ref, w_ref, b_ref, o_ref):
    c = c_ref[...]
    s = c * jax.nn.sigmoid(c)
    o_ref[...] = _dot(s.astype(BF16), w_ref[...].astype(BF16)) + b_ref[...]


def _modulation(cvec, ada_w, ada_b):
    rows, d = cvec.shape
    n = ada_w.shape[1]
    tn = 1024
    return pl.pallas_call(
        _mod_kernel,
        out_shape=jax.ShapeDtypeStruct((rows, n), F32),
        grid=(n // tn,),
        in_specs=[
            pl.BlockSpec((rows, d), lambda j: (0, 0)),
            pl.BlockSpec((d, tn), lambda j: (0, j)),
            pl.BlockSpec((1, tn), lambda j: (0, j)),
        ],
        out_specs=pl.BlockSpec((rows, tn), lambda j: (0, j)),
        compiler_params=_cparams(("arbitrary",)),
        name="adaln_mod",
    )(cvec, ada_w, ada_b)


def _inproj_kernel(x_ref, sh_ref, sc_ref, wm_ref, *rest, with_hy):
    if with_hy:
        wh_ref, cw_ref, cb_ref, q_ref, k_ref, v_ref, g_ref, a_ref, u_ref = rest
    else:
        q_ref, k_ref, v_ref, a_ref = rest
    tt = x_ref.shape[1]
    n_sub = max(tt // PROJ_SUB_TILE, 1)
    sub = tt // n_sub
    for s in range(n_sub):
        rows = slice(s * sub, (s + 1) * sub)
        h = _layer_norm(x_ref[0, rows, :]) * (1.0 + sc_ref[0]) + sh_ref[0]
        hb = h.astype(BF16)
        u = _dot(hb, wm_ref[...])
        q_ref[0, rows, :] = u[:, :GLA_KEY].astype(BF16)
        k_ref[0, rows, :] = u[:, GLA_KEY:2 * GLA_KEY].astype(BF16)
        v_ref[0, rows, :] = u[:, 2 * GLA_KEY:2 * GLA_KEY + GLA_VAL].astype(BF16)
        off = 2 * GLA_KEY + GLA_VAL
        if with_hy:
            g_ref[0, rows, :] = u[:, off:off + GLA_VAL].astype(BF16)
            off += GLA_VAL
        a_ref[0, rows, :] = u[:, off:off + A_PAD]
        if with_hy:
            ut = _dot_nt(wh_ref[...], hb)
            lane = lax.broadcasted_iota(jnp.int32, (1, LANES), 1) % GRID_W
            not_first = (lane != 0).astype(F32)
            not_last = (lane != GRID_W - 1).astype(F32)
            for j in range(sub // LANES):
                c = ut[:, j * LANES:(j + 1) * LANES]
                left = pltpu.roll(c, 1, axis=1) * not_first
                right = pltpu.roll(c, LANES - 1, axis=1) * not_last
                y = cw_ref[0] * left + cw_ref[1] * c + cw_ref[2] * right + cb_ref[...]
                lanes = slice(s * sub + j * LANES, s * sub + (j + 1) * LANES)
                u_ref[0, :, lanes] = y.astype(BF16)


def _input_projection(x, shift, scale, w_main, w_hy_t=None, conv_w=None, conv_b=None):
    b, l, d = x.shape
    tt = min(TOK_TILE, l)
    with_hy = w_hy_t is not None
    nm = w_main.shape[1]
    tok = lambda width, dt: jax.ShapeDtypeStruct((b, l, width), dt)
    tok_spec = lambda width: pl.BlockSpec((1, tt, width), lambda i, j: (i, j, 0))
    in_specs = [
        pl.BlockSpec((1, tt, d), lambda i, j: (i, j, 0)),
        pl.BlockSpec((1, 1, d), lambda i, j: (i, 0, 0)),
        pl.BlockSpec((1, 1, d), lambda i, j: (i, 0, 0)),
        pl.BlockSpec((d, nm), lambda i, j: (0, 0)),
    ]
    args = [x, shift, scale, w_main]
    if with_hy:
        ch = w_hy_t.shape[0]
        in_specs += [
            pl.BlockSpec((ch, d), lambda i, j: (0, 0)),
            pl.BlockSpec((HY_CONV, ch, LANES), lambda i, j: (0, 0, 0)),
            pl.BlockSpec((ch, LANES), lambda i, j: (0, 0)),
        ]
        args += [w_hy_t, conv_w, conv_b]
        out_shape = [tok(GLA_KEY, BF16), tok(GLA_KEY, BF16), tok(GLA_VAL, BF16), tok(GLA_VAL, BF16),
                     tok(A_PAD, F32), jax.ShapeDtypeStruct((b, ch, l), BF16)]
        out_specs = [tok_spec(GLA_KEY), tok_spec(GLA_KEY), tok_spec(GLA_VAL), tok_spec(GLA_VAL),
                     tok_spec(A_PAD), pl.BlockSpec((1, ch, tt), lambda i, j: (i, 0, j))]
    else:
        out_shape = [tok(GLA_KEY, BF16), tok(GLA_KEY, BF16), tok(GLA_VAL, BF16), tok(A_PAD, F32)]
        out_specs = [tok_spec(GLA_KEY), tok_spec(GLA_KEY), tok_spec(GLA_VAL), tok_spec(A_PAD)]
    return pl.pallas_call(
        functools.partial(_inproj_kernel, with_hy=with_hy),
        out_shape=out_shape,
        grid=(b, l // tt),
        in_specs=in_specs,
        out_specs=out_specs,
        compiler_params=_cparams(("parallel", "arbitrary")),
        name="in_proj_hy" if with_hy else "in_proj_ctx",
    )(*args)


def _gla_kernel(*refs, n_ctx_blocks, chunk, n_batch, n_levels, safe_exponent):
    ins = [refs[0:8], refs[8:16]]
    wa_ref, ba_ref, cw_ref, cm_ref, hm_ref, bd_ref, dq_ref, dk_ref, lm_ref = refs[16:25]
    outs = refs[25:27]
    st_ref = refs[27]
    s = pl.program_id(1)
    blk = ins[0][4].shape[1]
    n_chunks = blk // chunk

    @pl.when(s == 0)
    def _():
        st_ref[...] = jnp.zeros_like(st_ref)

    def two_pass(m, g_hi, g_lo):
        return _dot(m, g_hi) + _dot(m, g_lo)

    is_ctx = s < n_ctx_blocks
    bdmask = bd_ref[...]
    chains = [(d, bi) for d in range(2) for bi in range(n_batch)]

    def pre_activation(d, bi):
        a = jnp.where(is_ctx, ins[d][3][bi], ins[d][7][bi])
        return _dot(a.astype(BF16), wa_ref[d]) + ba_ref[d]

    def advance(d, bi, z, single_reference):
        qc_ref, kc_ref, vc_ref, _, ql_ref, kl_ref, vl_ref, _ = ins[d]
        q = jnp.where(is_ctx, qc_ref[bi], ql_ref[bi]).astype(F32)
        k = jnp.where(is_ctx, kc_ref[bi], kl_ref[bi]).astype(F32)
        v = jnp.where(is_ctx, vc_ref[bi], vl_ref[bi])
        g = (jnp.minimum(z, 0.0) - jnp.log(1.0 + jnp.exp(-jnp.abs(z)))) * (1.0 / GLA_TAU)
        g_hi = g.astype(BF16)
        g_lo = (g - g_hi.astype(F32)).astype(BF16)
        cum = two_pass(cw_ref[d], g_hi, g_lo)
        edge = (lambda c: (c + 1) * chunk - 1) if d == 0 else (lambda c: c * chunk)
        tot = jnp.concatenate([jnp.broadcast_to(cum[edge(c):edge(c) + 1], (chunk, GLA_KEY))
                               for c in range(n_chunks)], axis=0)
        qs_all = (q * jnp.exp(cum)).astype(BF16)
        kst_all = (k * jnp.exp(tot - cum)).astype(BF16)
        dec_all = jnp.exp(tot)

        if single_reference:
            ks_all = (k * jnp.exp(-cum)).astype(BF16)
            cmask = cm_ref[d]
            rows_out = []
            for c in range(n_chunks):
                rows = slice(c * chunk, (c + 1) * chunk)
                parts = []
                for h in range(GLA_HEADS):
                    a_h = _dot_nt(qs_all[rows] * hm_ref[h], ks_all[rows]) * cmask
                    parts.append(_dot(a_h.astype(BF16), v[rows, h * GLA_DV:(h + 1) * GLA_DV]))
                rows_out.append(jnp.concatenate(parts, axis=1))
            o_intra = jnp.concatenate(rows_out, axis=0)
        else:
            def level(lv, acc):
                ql = (q * jnp.exp(two_pass(dq_ref[d, lv], g_hi, g_lo))).astype(BF16)
                kl = (k * jnp.exp(two_pass(dk_ref[d, lv], g_hi, g_lo))).astype(BF16)
                msk = lm_ref[d, lv]
                parts = []
                for h in range(GLA_HEADS):
                    a_h = _dot_nt(ql * hm_ref[h], kl) * msk
                    parts.append(_dot(a_h.astype(BF16), v[:, h * GLA_DV:(h + 1) * GLA_DV]))
                return acc + jnp.concatenate(parts, axis=1)
            o_intra = lax.fori_loop(0, n_levels, level, jnp.zeros((blk, GLA_VAL), F32))

        for n in range(n_chunks):
            ci = n if d == 0 else n_chunks - 1 - n
            rows = slice(ci * chunk, (ci + 1) * chunk)
            o = _dot_nt(qs_all[rows], st_ref[d, bi].astype(BF16)) + o_intra[rows]
            outs[d][bi, rows, :] = o.astype(outs[d].dtype)
            st_ref[d, bi] = (st_ref[d, bi] * dec_all[ci * chunk:ci * chunk + 1]
                             + _dot_tn(v[rows], kst_all[rows]) * bdmask)

    zs = [pre_activation(d, bi) for d, bi in chains]
    z_low = zs[0]
    for z in zs[1:]:
        z_low = jnp.minimum(z_low, z)
    bound = (jnp.maximum(-jnp.min(z_low), 0.0) + math.log(2.0)) * (chunk / GLA_TAU)

    def step(single_reference):
        def run():
            for (d, bi), z in zip(chains, zs):
                advance(d, bi, z, single_reference)
        return run

    lax.cond(bound < safe_exponent, step(True), step(False))


def _gla_level_tables(blk, chunk):
    sizes = []
    s = chunk // 2
    while s >= 1:
        sizes.append(s)
        s //= 2
    n_lv = len(sizes) + 1
    dq = np.zeros((2, n_lv, blk, blk), np.float32)
    dk = np.zeros((2, n_lv, blk, blk), np.float32)
    lm = np.zeros((2, n_lv, blk, blk), np.float32)
    idx = np.arange(blk)
    for lv, s in enumerate(sizes):
        blk_id = idx // (2 * s)
        right = (idx % (2 * s)) >= s
        b = blk_id * 2 * s + s
        same = blk_id[:, None] == blk_id[None, :]
        m = idx[None, :]
        dq[0, lv] = (right[:, None] & (m >= b[:, None]) & (m <= idx[:, None]))
        dk[0, lv] = (~right[:, None] & (m > idx[:, None]) & (m < b[:, None]))
        lm[0, lv] = same & right[:, None] & ~right[None, :]
        dq[1, lv] = (~right[:, None] & (m >= idx[:, None]) & (m < b[:, None]))
        dk[1, lv] = (right[:, None] & (m >= b[:, None]) & (m < idx[:, None]))
        lm[1, lv] = same & ~right[:, None] & right[None, :]
    lm[:, n_lv - 1] = np.eye(blk, dtype=np.float32)
    return dq, dk, lm, n_lv


def _gla_scan(ctx_qkva, lat_qkva, wa, ba, chunk=GLA_CHUNK, blk=GLA_BLOCK, safe_exponent=GLA_SAFE_EXPONENT):
    qc, kc, vc, ac = ctx_qkva
    ql, kl, vl, al = lat_qkva
    b, l, _ = ql.shape
    ctx_len = qc.shape[1]
    n_ctx = ctx_len // blk
    n_lat = l // blk
    c = chunk
    idx = np.arange(c)
    tri = np.stack([idx[:, None] >= idx[None, :], idx[:, None] <= idx[None, :]]).astype(np.float32)
    cw = np.stack([np.kron(np.eye(blk // c, dtype=np.float32), tri[dd]) for dd in range(2)])
    dq, dk, lm, n_levels = _gla_level_tables(blk, c)
    hmask = np.zeros((GLA_HEADS, 1, GLA_KEY), np.float32)
    for h in range(GLA_HEADS):
        hmask[h, 0, h * GLA_DK:(h + 1) * GLA_DK] = 1.0
    bd = (np.arange(GLA_VAL)[:, None] // GLA_DV == np.arange(GLA_KEY)[None, :] // GLA_DK).astype(np.float32)

    nb = GLA_BATCH if b % GLA_BATCH == 0 else 1

    def lat_block(d):
        def f(i, s):
            t = jnp.maximum(s - n_ctx, 0)
            return (i, t if d == 0 else n_lat - 1 - t, 0)
        return f

    def ctx_block(d):
        def f(i, s):
            t = jnp.minimum(s, n_ctx - 1)
            return (i, t if d == 0 else n_ctx - 1 - t, 0)
        return f

    widths = (GLA_KEY, GLA_KEY, GLA_VAL, A_PAD)
    dir_specs = lambda d: ([pl.BlockSpec((nb, blk, w), ctx_block(d)) for w in widths]
                           + [pl.BlockSpec((nb, blk, w), lat_block(d)) for w in widths])
    consts = [wa, ba, jnp.asarray(cw, BF16), jnp.asarray(tri, F32), jnp.asarray(hmask, BF16), jnp.asarray(bd, F32),
              jnp.asarray(dq, BF16), jnp.asarray(dk, BF16), jnp.asarray(lm, F32)]
    full = lambda arr: pl.BlockSpec(arr.shape, lambda i, s: (0,) * arr.ndim)
    o_sds = jax.ShapeDtypeStruct((b, l, GLA_VAL), BF16)
    dir_args = [qc, kc, vc, ac, ql, kl, vl, al]
    return pl.pallas_call(
        functools.partial(_gla_kernel, n_ctx_blocks=n_ctx, chunk=c, n_batch=nb, n_levels=n_levels,
                          safe_exponent=safe_exponent),
        out_shape=[o_sds, o_sds],
        grid=(b // nb, n_ctx + n_lat),
        in_specs=dir_specs(0) + dir_specs(1) + [full(arr) for arr in consts],
        out_specs=[pl.BlockSpec((nb, blk, GLA_VAL), lat_block(0)), pl.BlockSpec((nb, blk, GLA_VAL), lat_block(1))],
        scratch_shapes=[pltpu.VMEM((2, nb, GLA_VAL, GLA_KEY), F32)],
        compiler_params=_cparams(("parallel", "arbitrary")),
        name="gla_scan",
    )(*dir_args, *dir_args, *consts)


def _dft_constants(r1):
    n = r1 * LANES
    h = r1 // 2
    k1 = np.arange(r1)
    f1 = np.exp(-2j * np.pi * np.outer(k1, k1) / r1)
    f2 = np.exp(-2j * np.pi * np.outer(np.arange(LANES), np.arange(LANES)) / LANES)
    tw = np.exp(-2j * np.pi * np.outer(k1, np.arange(LANES)) / n)
    fa_c = np.block([[f1.real[:, :h], -f1.imag[:, :h]], [f1.imag[:, :h], f1.real[:, :h]]])
    fa_r = np.concatenate([f1.real, f1.imag], axis=0)
    gc = np.block([[f2.real, f2.imag], [-f2.imag, f2.real]])
    gci = np.block([[f2.real, -f2.imag], [f2.imag, f2.real]])
    fai = np.block([[f1.real[:h], f1.imag[:h]], [-f1.imag[:h], f1.real[:h]]]) / n
    tw_lane = np.tile(tw, (1, HY_GROUP))
    tw_row = np.tile(tw, (HY_GROUP, 1))
    f = lambda a: jnp.asarray(a, F32)
    return dict(fa_c=f(fa_c), fa_r=f(fa_r), gc=f(gc), gci=f(gci), fai=f(fai),
                twl_r=f(tw_lane.real), twl_i=f(tw_lane.imag), twr_r=f(tw_row.real), twr_i=f(tw_row.imag))


def _fwd_dft(rhs, fa, gc, twl_r, twl_i, r1):
    a = _dot(fa, rhs)
    ar, ai = a[:r1], a[r1:]
    br = (ar * twl_r - ai * twl_i).astype(BF16)
    bi = (ar * twl_i + ai * twl_r).astype(BF16)
    lhs = jnp.concatenate(
        [jnp.concatenate([br[:, c * LANES:(c + 1) * LANES], bi[:, c * LANES:(c + 1) * LANES]], axis=1)
         for c in range(HY_GROUP)], axis=0)
    return _dot(lhs, gc)


def _inv_dft(yr, yi, gci, fai, twr_r, twr_i, r1):
    lhs = jnp.concatenate([yr, yi], axis=1).astype(BF16)
    c = _dot(lhs, gci)
    cr, ci = c[:, :LANES], c[:, LANES:]
    dr = (cr * twr_r + ci * twr_i).astype(BF16)
    di = (ci * twr_r - cr * twr_i).astype(BF16)
    rhs = jnp.concatenate(
        [jnp.concatenate([dr[g * r1:(g + 1) * r1], di[g * r1:(g + 1) * r1]], axis=0)
         for g in range(HY_GROUP)], axis=1)
    return _dot(fai, rhs)


def _filter_mlp_kernel(zt_ref, tn_ref, rate_ref, w1_ref, b1_ref, w2_ref, b2_ref, fr_ref, wo_ref, o_ref):
    hp = lax.Precision.HIGHEST
    lt = tn_ref.shape[1]
    lanes = lambda ref: jnp.concatenate([ref[...]] * (lt // LANES), axis=1)
    fr = lanes(fr_ref)
    hid = jnp.sin(fr * (jnp.dot(w1_ref[...], zt_ref[...], precision=hp, preferred_element_type=F32) + lanes(b1_ref)))
    hid = jnp.sin(fr * (jnp.dot(w2_ref[...], hid, precision=hp, preferred_element_type=F32) + lanes(b2_ref)))
    window = jnp.exp(-tn_ref[0:1, :] * lanes(rate_ref)) * tn_ref[1:2, :]
    for o in range(HY_ORDER):
        o_ref[o] = jnp.dot(wo_ref[o, 0], hid, precision=hp, preferred_element_type=F32) * window


def _filter_mlp(z_t, tn_rows, rate, w1_t, b1, w2_t, b2, freq, wo_t):
    emb, n2l = z_t.shape
    l = n2l // 2
    lt = min(FILT_TILE, l)
    nt = l // lt
    ch = rate.shape[0]
    full = lambda shape: pl.BlockSpec(shape, lambda d, j: (0,) * len(shape))
    return pl.pallas_call(
        _filter_mlp_kernel,
        out_shape=jax.ShapeDtypeStruct((HY_ORDER, ch, n2l), F32),
        grid=(2, nt),
        in_specs=[
            pl.BlockSpec((emb, lt), lambda d, j: (0, d * nt + j)),
            pl.BlockSpec((8, lt), lambda d, j: (0, d * nt + j)),
            full((ch, LANES)),
            full((HY_FH, emb)), full((HY_FH, LANES)), full((HY_FH, HY_FH)), full((HY_FH, LANES)),
            full((HY_FH, LANES)),
            pl.BlockSpec((HY_ORDER, 1, ch, HY_FH), lambda d, j: (0, d, 0, 0)),
        ],
        out_specs=pl.BlockSpec((HY_ORDER, ch, lt), lambda d, j: (0, 0, d * nt + j)),
        compiler_params=_cparams(("arbitrary", "arbitrary")),
        name="hyena_filter_mlp",
    )(z_t, tn_rows, rate, w1_t, b1, w2_t, b2, freq, wo_t)


def _filter_fft_kernel(f_ref, fa_ref, gc_ref, twl_r_ref, twl_i_ref, h_ref, *, r1):
    fa = fa_ref[...].astype(BF16)
    gc = gc_ref[...].astype(BF16)
    nc = f_ref.shape[1]
    for g0 in range(0, nc, HY_GROUP):
        rhs = jnp.concatenate([f_ref[0, g0 + c].astype(BF16) for c in range(HY_GROUP)], axis=1)
        x = _fwd_dft(rhs, fa, gc, twl_r_ref[...], twl_i_ref[...], r1)
        for c in range(HY_GROUP):
            h_ref[0, g0 + c] = x[c * r1:(c + 1) * r1].astype(h_ref.dtype)


def _filter_spectra(filt, consts, r1):
    order, ch = filt.shape[:2]
    nc = HY_CH_BLOCK
    full = lambda a: pl.BlockSpec(a.shape, lambda o, j: (0,) * a.ndim)
    cs = [consts["fa_r"], consts["gc"], consts["twl_r"], consts["twl_i"]]
    return pl.pallas_call(
        functools.partial(_filter_fft_kernel, r1=r1),
        out_shape=jax.ShapeDtypeStruct((order, ch, r1, 2 * LANES), BF16),
        grid=(order, ch // nc),
        in_specs=[pl.BlockSpec((1, nc, r1, LANES), lambda o, j: (o, j, 0, 0))] + [full(a) for a in cs],
        out_specs=pl.BlockSpec((1, nc, r1, 2 * LANES), lambda o, j: (o, j, 0, 0)),
        compiler_params=_cparams(("arbitrary", "arbitrary")),
        name="hyena_filter_fft",
    )(filt, *cs)


def _hyena_kernel(dbias_ref, v_ref, x1_ref, x2_ref, h_ref, fa_ref, gc_ref, gci_ref, fai_ref,
                  twl_r_ref, twl_i_ref, twr_r_ref, twr_i_ref, y_ref, *, r1):
    fa = fa_ref[...].astype(BF16)
    gc = gc_ref[...].astype(BF16)
    gci = gci_ref[...].astype(BF16)
    fai = fai_ref[...].astype(BF16)
    twl_r, twl_i = twl_r_ref[...], twl_i_ref[...]
    twr_r, twr_i = twr_r_ref[...], twr_i_ref[...]
    nc = v_ref.shape[2]
    half = r1 // 2
    c_base = pl.program_id(0) * nc

    def conv(sig, order, g0):
        rhs = jnp.concatenate(
            [jnp.concatenate([sig[c][0].astype(BF16), sig[c][1].astype(BF16)], axis=0)
             for c in range(HY_GROUP)], axis=1)
        x = _fwd_dft(rhs, fa, gc, twl_r, twl_i, r1)
        xr, xi = x[:, :LANES], x[:, LANES:]
        hh = jnp.concatenate([h_ref[order, g0 + c] for c in range(HY_GROUP)], axis=0).astype(F32)
        hr, hi = hh[:, :LANES], hh[:, LANES:]
        y = _inv_dft(xr * hr - xi * hi, xr * hi + xi * hr, gci, fai, twr_r, twr_i, r1)
        out = []
        for c in range(HY_GROUP):
            dcoef = dbias_ref[order, c_base + g0 + c]
            yc = y[:, c * LANES:(c + 1) * LANES]
            out.append([yc[:half] + dcoef * sig[c][0], yc[half:] + dcoef * sig[c][1]])
        return out

    for g0 in range(0, nc, HY_GROUP):
        v = [[v_ref[0, b, g0 + c].astype(F32) for b in range(2)] for c in range(HY_GROUP)]
        y1 = conv(v, 0, g0)
        z = [[x1_ref[0, b, g0 + c].astype(F32) * y1[c][b] for b in range(2)] for c in range(HY_GROUP)]
        y2 = conv(z, 1, g0)
        for c in range(HY_GROUP):
            for b in range(2):
                y_ref[0, b, g0 + c] = (x2_ref[0, b, g0 + c].astype(F32) * y2[c][b]).astype(y_ref.dtype)


def _hyena_conv(u_t, spectra, d_bias, consts, r1):
    bp, _, ch3, half, _ = u_t.shape
    ch = ch3 // 3
    nc = HY_CH_BLOCK
    nblk = ch // nc
    names = ["fa_c", "gc", "gci", "fai", "twl_r", "twl_i", "twr_r", "twr_i"]
    cs = [consts[k] for k in names]
    full = lambda a: pl.BlockSpec(a.shape, lambda j, p: (0,) * a.ndim)
    part = lambda k: pl.BlockSpec((1, 2, nc, half, LANES), lambda j, p: (p, 0, k * nblk + j, 0, 0))
    return pl.pallas_call(
        functools.partial(_hyena_kernel, r1=r1),
        out_shape=jax.ShapeDtypeStruct((bp, 2, ch, half, LANES), BF16),
        grid=(nblk, bp),
        in_specs=[pl.BlockSpec(memory_space=pltpu.SMEM), part(0), part(1), part(2),
                  pl.BlockSpec((HY_ORDER, nc, r1, 2 * LANES), lambda j, p: (0, j, 0, 0))] + [full(a) for a in cs],
        out_specs=pl.BlockSpec((1, 2, nc, half, LANES), lambda j, p: (p, 0, j, 0, 0)),
        compiler_params=_cparams(("arbitrary", "arbitrary")),
        name="hyena_conv",
    )(d_bias, u_t, u_t, u_t, spectra, *cs)


def _filter_inputs(l):
    n = jnp.arange(2 * l, dtype=jnp.int32)
    t = jnp.where(n < l, n, 2 * l - n).astype(F32)
    valid = (n != l).astype(F32)
    t_norm = t / (l - 1)
    bands = (HY_EMB - 1) // 2
    f = jnp.linspace(1e-4, bands - 1, bands, dtype=F32)
    ang = (2.0 * math.pi * t / l)[:, None] * f[None, :]
    z = jnp.concatenate([t_norm[:, None], jnp.cos(ang), -jnp.sin(ang)], -1)
    zz = jnp.pad(z, ((0, 0), (0, HY_EMB_PAD - HY_EMB))).T
    tn_rows = jnp.zeros((8, 2 * l), F32).at[0].set(t_norm).at[1].set(valid)
    deltas = jnp.linspace(math.log(HY_TARGET) / HY_SLOW, math.log(HY_TARGET) / HY_FAST, HY_WIDTH, dtype=F32)
    rate = jnp.broadcast_to(jnp.abs(deltas)[:, None], (HY_WIDTH, LANES))
    return zz, tn_rows, rate


def _outproj_kernel(of_ref, ob_ref, g_ref, yh_ref, x_ref, gate_ref, sh_ref, sc_ref,
                    wg_ref, wh_ref, ng_ref, l1g_ref, l1b_ref, wr_ref, rb_ref, ltri_ref,
                    xm_ref, h2_ref, se_ref, sr_ref, sw_ref, cnt_ref, carry_ref):
    @pl.when((pl.program_id(0) == 0) & (pl.program_id(1) == 0))
    def _():
        carry_ref[...] = jnp.zeros_like(carry_ref)

    tt = x_ref.shape[1]
    sub = ltri_ref.shape[0]
    for s in range(tt // sub):
        rows = slice(s * sub, (s + 1) * sub)
        o = of_ref[0, rows, :].astype(F32) + ob_ref[0, rows, :].astype(F32)
        g = g_ref[0, rows, :].astype(F32)
        parts = []
        for h in range(GLA_HEADS):
            oh = o[:, h * GLA_DV:(h + 1) * GLA_DV]
            parts.append(oh * lax.rsqrt(jnp.mean(oh * oh, axis=-1, keepdims=True) + LN_EPS))
        y_gla = jnp.concatenate(parts, axis=1) * ng_ref[...] * (g * jax.nn.sigmoid(g))
        f = _dot(y_gla.astype(BF16), wg_ref[...]) + _dot_tn(yh_ref[0, :, rows], wh_ref[...])
        x_mid = (_layer_norm(DEEPNORM_ALPHA * x_ref[0, rows, :] + gate_ref[0] * f) * l1g_ref[...]
                 + l1b_ref[...])
        xm_ref[0, rows, :] = x_mid
        h2 = _layer_norm(x_mid) * (1.0 + sc_ref[0]) + sh_ref[0]
        for j in range(ROW_TILES):
            h2_ref[0, pl.ds(s * sub * ROW_TILES + j, sub, stride=ROW_TILES), :] = h2[:, j * LANES:(j + 1) * LANES]

        logits = _dot(h2.astype(BF16), wr_ref[...]) + rb_ref[...]
        lane = lax.broadcasted_iota(jnp.int32, logits.shape, 1).astype(F32)
        hits, idxs, exps = [], [], []
        m0 = None
        for _ in range(TOP_K):
            m = jnp.max(logits, axis=-1, keepdims=True)
            idx = jnp.min(jnp.where(logits == m, lane, float(LANES)), axis=-1, keepdims=True)
            hit = lane == idx
            m0 = m if m0 is None else m0
            hits.append(hit)
            idxs.append(idx)
            exps.append(jnp.exp(m - m0))
            logits = jnp.where(hit, NEG_BIG, logits)
        denom = exps[0]
        sel = jnp.where(hits[0], 1.0, 0.0)
        for kk in range(1, TOP_K):
            denom = denom + exps[kk]
            sel = sel + jnp.where(hits[kk], 1.0, 0.0)
        rank_all = _dot(ltri_ref[...], sel.astype(BF16)) + carry_ref[0:1, :]
        carry_ref[0:1, :] = carry_ref[0:1, :] + jnp.sum(sel, axis=0, keepdims=True)
        se = jnp.zeros(logits.shape, F32)
        sr = jnp.zeros(logits.shape, F32)
        sw = jnp.zeros(logits.shape, F32)
        for kk in range(TOP_K):
            rk = jnp.sum(jnp.where(hits[kk], rank_all, 0.0), axis=-1, keepdims=True)
            col = lane == float(kk)
            se = jnp.where(col, idxs[kk], se)
            sr = jnp.where(col, rk, sr)
            sw = jnp.where(col, exps[kk] / denom, sw)
        se_ref[0, rows, :] = se.astype(jnp.int32)
        sr_ref[0, rows, :] = sr.astype(jnp.int32)
        sw_ref[0, rows, :] = sw
    cnt_ref[...] = jnp.broadcast_to(carry_ref[0:1, :], cnt_ref.shape).astype(jnp.int32)


def _output_projection(o_f, o_b, g, y_hy_t, x, gate1, shift2, scale2, w_gla, w_hy, norm_g, ln1_g, ln1_b, wr, rb):
    b, l, d = x.shape
    tt = TOK_TILE
    tok = lambda w: pl.BlockSpec((1, tt, w), lambda i, j: (i, j, 0))
    row = lambda: pl.BlockSpec((1, 1, d), lambda i, j: (i, 0, 0))
    full = lambda a: pl.BlockSpec(a.shape, lambda i, j: (0,) * a.ndim)
    sub = tt
    ltri = jnp.asarray(np.tril(np.ones((sub, sub), np.float32), -1), BF16)
    consts = [w_gla, w_hy, norm_g, ln1_g, ln1_b, wr, rb, ltri]
    lane_i = jax.ShapeDtypeStruct((b, l, LANES), jnp.int32)
    return pl.pallas_call(
        _outproj_kernel,
        out_shape=[jax.ShapeDtypeStruct((b, l, d), F32), jax.ShapeDtypeStruct((b, l * ROW_TILES, LANES), F32),
                   lane_i, lane_i, jax.ShapeDtypeStruct((b, l, LANES), F32),
                   jax.ShapeDtypeStruct((8, LANES), jnp.int32)],
        grid=(b, l // tt),
        in_specs=[
            tok(GLA_VAL), tok(GLA_VAL), tok(GLA_VAL),
            pl.BlockSpec((1, HY_WIDTH, tt), lambda i, j: (i, 0, j)),
            tok(d), row(), row(), row(),
        ] + [full(a) for a in consts],
        out_specs=[tok(d), pl.BlockSpec((1, tt * ROW_TILES, LANES), lambda i, j: (i, j, 0)),
                   tok(LANES), tok(LANES), tok(LANES),
                   pl.BlockSpec((8, LANES), lambda i, j: (0, 0))],
        scratch_shapes=[pltpu.VMEM((8, LANES), F32)],
        compiler_params=_cparams(("arbitrary", "arbitrary")),
        name="out_proj_router",
    )(o_f, o_b, g, y_hy_t, x, gate1, shift2, scale2, *consts)


def _expert_weight_layout(w1_ref, w2_ref, p_ref, w1o_ref, w2o_ref):
    p = p_ref[...]
    for j in range(w1_ref.shape[2] // SWIGLU_BLOCK):
        cols = slice(j * SWIGLU_BLOCK, (j + 1) * SWIGLU_BLOCK)
        w1o_ref[0, :, cols] = _dot(w1_ref[0, :, cols].astype(BF16), p).astype(BF16)
    w2o_ref[0] = w2_ref[0].astype(BF16)


def _row_copy_groups(n_tokens, make_copy):
    def group(gidx, carry):
        for u in range(DMA_UNROLL):
            r = gidx * DMA_UNROLL + u
            for kk in range(TOP_K):
                make_copy(r, kk, r * TOP_K + kk).start(priority=kk % 2)
        return carry
    lax.fori_loop(0, n_tokens // DMA_UNROLL, group, 0)


def _dispatch_kernel(base_ref, cnt_ref, nt_ref, slot_ref, h2_ref, w1_ref, w2_ref, p_ref,
                     xs_hbm, w1o_ref, w2o_ref, zrow_ref, zblk_ref, sem, zsem):
    tt = h2_ref.shape[0] // ROW_TILES

    def row_copy(r, kk, a):
        return pltpu.make_async_copy(h2_ref.at[_row_tile(r)], xs_hbm.at[_row_tile(slot_ref[a])], sem)

    _row_copy_groups(tt, row_copy)
    _expert_weight_layout(w1_ref, w2_ref, p_ref, w1o_ref, w2o_ref)
    for _ in range(TOP_K):
        pltpu.make_async_copy(h2_ref, xs_hbm.at[pl.ds(0, tt * ROW_TILES)], sem).wait()

    @pl.when(pl.program_id(0) == pl.num_programs(0) - 1)
    def _():
        zrow_ref[...] = jnp.zeros_like(zrow_ref)

        def per_expert(e, carry):
            n = cnt_ref[e]
            end = ((n + (MOE_TILE - 1)) // MOE_TILE) * MOE_TILE

            def fill(r, c):
                pltpu.make_async_copy(zrow_ref, xs_hbm.at[_row_tile(base_ref[e] + r)], zsem).start()
                return c

            def drain(r, c):
                pltpu.make_async_copy(zrow_ref, xs_hbm.at[_row_tile(0)], zsem).wait()
                return c

            lax.fori_loop(n, end, fill, 0)
            lax.fori_loop(n, end, drain, 0)
            return carry

        lax.fori_loop(0, N_EXPERTS, per_expert, 0)

        zblk_ref[...] = jnp.zeros_like(zblk_ref)
        tile_rows = MOE_TILE * ROW_TILES
        n_all = xs_hbm.shape[0] // tile_rows

        def tile_copy(ti):
            row0 = pl.multiple_of(ti * tile_rows, tile_rows)
            return pltpu.make_async_copy(zblk_ref, xs_hbm.at[pl.ds(row0, tile_rows)], zsem)

        def fill_tile(ti, c):
            tile_copy(ti).start()
            return c

        def drain_tile(ti, c):
            tile_copy(ti).wait()
            return c

        lax.fori_loop(nt_ref[0], n_all, fill_tile, 0)
        lax.fori_loop(nt_ref[0], n_all, drain_tile, 0)


def _dispatch(h2, slot_flat, base, counts, n_tiles, n_slots, w1, w2):
    t = h2.shape[0] // ROW_TILES
    ne, d, f2 = w1.shape
    assert t % (ne * DMA_UNROLL) == 0, "one dispatch step per expert"
    tt = t // ne
    src = np.concatenate([np.arange(0, SWIGLU_BLOCK, 2), np.arange(1, SWIGLU_BLOCK, 2)])
    perm = np.zeros((SWIGLU_BLOCK, SWIGLU_BLOCK), np.float32)
    perm[src, np.arange(SWIGLU_BLOCK)] = 1.0
    exp_blk = lambda shape: pl.BlockSpec((1,) + shape, lambda i, *_: (i, 0, 0))
    return pl.pallas_call(
        _dispatch_kernel,
        out_shape=[jax.ShapeDtypeStruct((n_slots * ROW_TILES, LANES), F32),
                   jax.ShapeDtypeStruct(w1.shape, BF16), jax.ShapeDtypeStruct(w2.shape, BF16)],
        grid_spec=pltpu.PrefetchScalarGridSpec(
            num_scalar_prefetch=3,
            grid=(ne,),
            in_specs=[pl.BlockSpec((tt * TOP_K,), lambda i, *_: (i,), memory_space=pltpu.SMEM),
                      pl.BlockSpec((tt * ROW_TILES, LANES), lambda i, *_: (i, 0)),
                      exp_blk((d, f2)), exp_blk(w2.shape[1:]),
                      pl.BlockSpec((SWIGLU_BLOCK, SWIGLU_BLOCK), lambda i, *_: (0, 0))],
            out_specs=[pl.BlockSpec(memory_space=pl.ANY), exp_blk((d, f2)), exp_blk(w2.shape[1:])],
            scratch_shapes=[pltpu.VMEM((ROW_TILES, LANES), F32), pltpu.VMEM((MOE_TILE * ROW_TILES, LANES), F32),
                            pltpu.SemaphoreType.DMA, pltpu.SemaphoreType.DMA],
        ),
        compiler_params=_cparams(("arbitrary",)),
        name="moe_dispatch",
    )(base, counts, n_tiles, slot_flat, h2, w1, w2, jnp.asarray(perm, BF16))


def _ffn_kernel(te_ref, nt_ref, xs_ref, *refs):
    ys_ref = refs[-1]
    tm = MOE_TILE
    rows = tm * ROW_TILES
    n_here = jnp.clip(nt_ref[0] - pl.program_id(0) * FFN_TILES_PER_STEP, 0, FFN_TILES_PER_STEP)

    def tile(t):
        w1_ref, b1_ref, w2_ref, b2_ref = refs[4 * t:4 * t + 4]
        x = jnp.concatenate([xs_ref[pl.ds(t * rows + j, tm, stride=ROW_TILES), :] for j in range(ROW_TILES)], axis=1)
        hid = _dot(x.astype(BF16), w1_ref[0]) + b1_ref[0]
        acts = []
        for j in range(hid.shape[1] // SWIGLU_BLOCK):
            glu = jnp.minimum(hid[:, j * SWIGLU_BLOCK:j * SWIGLU_BLOCK + LANES], SWIGLU_LIMIT)
            lin = jnp.clip(hid[:, j * SWIGLU_BLOCK + LANES:(j + 1) * SWIGLU_BLOCK], -SWIGLU_LIMIT, SWIGLU_LIMIT)
            acts.append((glu * jax.nn.sigmoid(SWIGLU_ALPHA * glu) * (lin + 1.0)).astype(BF16))
        y = _dot(jnp.concatenate(acts, axis=1), w2_ref[0]) + b2_ref[0]
        for j in range(ROW_TILES):
            ys_ref[pl.ds(t * rows + j, tm, stride=ROW_TILES), :] = y[:, j * LANES:(j + 1) * LANES]

    for n_valid in range(FFN_TILES_PER_STEP + 1):
        @pl.when(n_here == n_valid)
        def _(n_valid=n_valid):
            for t in range(n_valid):
                tile(t)
            for t in range(n_valid, FFN_TILES_PER_STEP):
                ys_ref[t * rows:(t + 1) * rows, :] = jnp.zeros((rows, LANES), F32)


def _expert_ffn(xs, tile_expert, n_tiles, w1p, b1p, w2b, b2):
    n_slots = xs.shape[0] // ROW_TILES
    d = w1p.shape[1]
    tps = FFN_TILES_PER_STEP
    tm = MOE_TILE
    f2 = w1p.shape[2]
    assert n_slots % (tm * tps) == 0
    rows_blk = (tps * tm * ROW_TILES, LANES)
    weights = []
    for t in range(tps):
        exp_blk = lambda i, te, nt, t=t: (te[i * tps + t], 0, 0)
        weights += [pl.BlockSpec((1, d, f2), exp_blk), pl.BlockSpec((1, 1, f2), exp_blk),
                    pl.BlockSpec((1, f2 // 2, d), exp_blk), pl.BlockSpec((1, 1, d), exp_blk)]
    return pl.pallas_call(
        _ffn_kernel,
        out_shape=jax.ShapeDtypeStruct(xs.shape, F32),
        grid_spec=pltpu.PrefetchScalarGridSpec(
            num_scalar_prefetch=2,
            grid=(n_slots // (tm * tps),),
            in_specs=[
                pl.BlockSpec(rows_blk, lambda i, te, nt: (jnp.minimum(i, (nt[0] - 1) // tps), 0)),
            ] + weights,
            out_specs=pl.BlockSpec(rows_blk, lambda i, te, nt: (i, 0)),
        ),
        compiler_params=_cparams(("arbitrary",)),
        name="moe_expert_ffn",
    )(tile_expert, n_tiles, xs, *([w1p, b1p, w2b, b2] * tps))


def _combine_kernel(slot_ref, slot_next_ref, w_ref, xm_ref, gate_ref, l2g_ref, l2b_ref,
                    ys_hbm, o_ref, buf_ref, sem):
    i = pl.program_id(0)
    tt = xm_ref.shape[0]
    cur = lax.rem(i, 2)

    def issue(sref, sl):
        def row_copy(r, kk, a):
            return pltpu.make_async_copy(ys_hbm.at[_row_tile(sref[a])], buf_ref.at[sl, kk, _row_tile(r)],
                                         sem.at[sl])
        _row_copy_groups(tt, row_copy)

    @pl.when(i == 0)
    def _():
        issue(slot_ref, 0)

    @pl.when(i + 1 < pl.num_programs(0))
    def _():
        issue(slot_next_ref, 1 - cur)

    for kk in range(TOP_K):
        pltpu.make_async_copy(ys_hbm.at[pl.ds(0, tt * ROW_TILES)], buf_ref.at[cur, kk], sem.at[cur]).wait()
    w = w_ref[...]
    cols = []
    for j in range(ROW_TILES):
        acc = w[:, 0:1] * buf_ref[cur, 0, _row_slab(tt, j), :]
        for kk in range(1, TOP_K):
            acc = acc + w[:, kk:kk + 1] * buf_ref[cur, kk, _row_slab(tt, j), :]
        cols.append(acc)
    pre = DEEPNORM_ALPHA * xm_ref[...] + gate_ref[0] * jnp.concatenate(cols, axis=1)
    o_ref[...] = _layer_norm(pre) * l2g_ref[...] + l2b_ref[...]


def _combine(ys, slot_flat, sel_w, x_mid, gate2, ln2_g, ln2_b, tiles_per_batch):
    t, d = x_mid.shape
    tt = min(COMBINE_TILE, t)
    n = t // tt
    return pl.pallas_call(
        _combine_kernel,
        out_shape=jax.ShapeDtypeStruct((t, d), F32),
        grid=(n,),
        in_specs=[
            pl.BlockSpec((tt * TOP_K,), lambda i: (i,), memory_space=pltpu.SMEM),
            pl.BlockSpec((tt * TOP_K,), lambda i: (jnp.minimum(i + 1, n - 1),), memory_space=pltpu.SMEM),
            pl.BlockSpec((tt, LANES), lambda i: (i, 0)),
            pl.BlockSpec((tt, d), lambda i: (i, 0)),
            pl.BlockSpec((1, 1, d), lambda i: (i // tiles_per_batch, 0, 0)),
            pl.BlockSpec((1, d), lambda i: (0, 0)),
            pl.BlockSpec((1, d), lambda i: (0, 0)),
            pl.BlockSpec(memory_space=pl.ANY),
        ],
        out_specs=pl.BlockSpec((tt, d), lambda i: (i, 0)),
        scratch_shapes=[pltpu.VMEM((2, TOP_K, tt * ROW_TILES, LANES), F32), pltpu.SemaphoreType.DMA((2,))],
        compiler_params=_cparams(("arbitrary",)),
        name="moe_combine",
    )(slot_flat, slot_flat, sel_w, x_mid, gate2, ln2_g, ln2_b, ys)


def kernel(x, c, ctx, c_ctx, ada_w, ada_b, w_in, gla_wa_f, gla_ba_f, gla_wa_b, gla_ba_b, gla_norm_g,
           hy_conv_w, hy_conv_b, hy_flt_w1, hy_flt_b1, hy_flt_w2, hy_flt_b2, hy_flt_wout, hy_flt_freq,
           hy_bias_d, w_out, ln1_g, ln1_b, router_w, router_b, exp_w1, exp_b1, exp_w2, exp_b2, ln2_g, ln2_b):
    batch, seq_len, d = x.shape
    lyr = 0
    ch = HY_WIDTH

    n_rows = 8 * ((batch + 1 + 7) // 8)
    cvec = jnp.zeros((n_rows, d), F32).at[:batch].set(c).at[batch].set(c_ctx)
    mod = _modulation(cvec, ada_w[lyr], ada_b[lyr][None, :])
    part = lambda rows, i: rows[:, None, i * d:(i + 1) * d]
    mod_x = mod[:batch]
    mod_c = jnp.broadcast_to(mod[batch:batch + 1], (batch, 6 * d))

    w = w_in[lyr]
    o_q, o_k, o_v, o_g = 0, GLA_KEY, 2 * GLA_KEY, 2 * GLA_KEY + GLA_VAL
    o_a = o_g + GLA_VAL
    o_h = o_a + 2 * GLA_RANK
    a_cols = jnp.pad(w[:, o_a:o_h], ((0, 0), (0, A_PAD - 2 * GLA_RANK)))
    q_cols = w[:, o_q:o_k] * (GLA_DK ** -0.5)
    w_main = jnp.concatenate([q_cols, w[:, o_k:o_a], a_cols], axis=1).astype(BF16)
    w_ctx = jnp.concatenate([q_cols, w[:, o_k:o_g], a_cols], axis=1).astype(BF16)
    w_hy_t = w[:, o_h:].T.astype(BF16)
    conv_w = jnp.broadcast_to(hy_conv_w[lyr][:, :, None], (HY_CONV, 3 * ch, LANES))
    conv_b = jnp.broadcast_to(hy_conv_b[lyr][:, None], (3 * ch, LANES))

    ctx_qkva = _input_projection(ctx, part(mod_c, 0), part(mod_c, 1), w_ctx)
    q, k, v, g, a_low, u_t = _input_projection(x, part(mod_x, 0), part(mod_x, 1), w_main, w_hy_t, conv_w, conv_b)

    wa = jnp.zeros((2, A_PAD, GLA_KEY), F32)
    wa = wa.at[0, :GLA_RANK].set(gla_wa_f[lyr]).at[1, GLA_RANK:2 * GLA_RANK].set(gla_wa_b[lyr]).astype(BF16)
    ba = jnp.stack([gla_ba_f[lyr], gla_ba_b[lyr]])[:, None, :]
    o_f, o_b = _gla_scan(ctx_qkva, (q, k, v, a_low), wa, ba)

    r1 = 2 * seq_len // LANES
    consts = _dft_constants(r1)
    zz, tn_rows, rate = _filter_inputs(seq_len)
    w1_t = jnp.pad(hy_flt_w1[lyr], ((0, HY_EMB_PAD - HY_EMB), (0, 0))).T
    unit_tile = lambda vec: jnp.broadcast_to(vec[:, None], (HY_FH, LANES))
    wo_t = hy_flt_wout[lyr].reshape(HY_FH, HY_ORDER, 2, ch).transpose(1, 2, 3, 0)
    filt = _filter_mlp(zz, tn_rows, rate, w1_t, unit_tile(hy_flt_b1[lyr]), hy_flt_w2[lyr].T,
                       unit_tile(hy_flt_b2[lyr]), unit_tile(hy_flt_freq[lyr]), wo_t)
    spectra = _filter_spectra(filt.reshape(HY_ORDER, ch, r1, LANES), consts, r1)
    y_hy = _hyena_conv(u_t.reshape(batch // 2, 2, 3 * ch, r1 // 2, LANES), spectra, hy_bias_d[lyr], consts, r1)
    y_hy_t = y_hy.reshape(batch, ch, seq_len)

    wo = w_out[lyr].astype(BF16)
    wr = jnp.pad(router_w[lyr], ((0, 0), (0, LANES - N_EXPERTS))).astype(BF16)
    rb = jnp.pad(router_b[lyr], (0, LANES - N_EXPERTS), constant_values=NEG_BIG)[None, :]
    norm_g = jnp.tile(gla_norm_g[lyr], GLA_HEADS)[None, :]
    x_mid, h2, sel_e, sel_r, sel_w, cnt = _output_projection(
        o_f, o_b, g, y_hy_t, x, part(mod_x, 2), part(mod_x, 3), part(mod_x, 4),
        wo[:GLA_VAL], wo[GLA_VAL:], norm_g, ln1_g[lyr][None], ln1_b[lyr][None], wr, rb)

    t = batch * seq_len
    counts = cnt[0, :N_EXPERTS]
    tiles_e = (counts + (MOE_TILE - 1)) // MOE_TILE
    tile_end = jnp.cumsum(tiles_e)
    base = ((tile_end - tiles_e) * MOE_TILE).astype(jnp.int32)
    n_tiles = tile_end[-1:].astype(jnp.int32)
    max_tiles = t * TOP_K // MOE_TILE + N_EXPERTS
    tile_ids = jnp.minimum(jnp.arange(max_tiles, dtype=jnp.int32), n_tiles[0] - 1)
    tile_expert = jnp.sum(tile_ids[:, None] >= tile_end[None, :], axis=1).astype(jnp.int32)
    e_flat = sel_e.reshape(t, LANES)[:, :TOP_K].reshape(t * TOP_K)
    r_flat = sel_r.reshape(t, LANES)[:, :TOP_K].reshape(t * TOP_K)
    slot_flat = base[e_flat] + r_flat

    n_blk = 2 * D_EXPERT // SWIGLU_BLOCK
    b1p_e = exp_b1[lyr].reshape(N_EXPERTS, n_blk, LANES, 2).transpose(0, 1, 3, 2).reshape(N_EXPERTS, 1, 2 * D_EXPERT)
    xs, w1p_e, w2_e = _dispatch(h2.reshape(t * ROW_TILES, LANES), slot_flat, base, counts, n_tiles,
                                max_tiles * MOE_TILE, exp_w1[lyr], exp_w2[lyr])
    ys = _expert_ffn(xs, tile_expert, n_tiles, w1p_e, b1p_e, w2_e, exp_b2[lyr][:, None, :])
    out = _combine(ys, slot_flat, sel_w.reshape(t, LANES), x_mid.reshape(t, d), part(mod_x, 5),
                   ln2_g[lyr][None], ln2_b[lyr][None], seq_len // min(COMBINE_TILE, t))
    return out.reshape(batch, seq_len, d)
```

```python
import functools
import math

import numpy as np
import jax
import jax.numpy as jnp
from jax import lax
from jax.experimental import pallas as pl
from jax.experimental.pallas import tpu as pltpu

F32 = jnp.float32
BF16 = jnp.bfloat16

D_MODEL = 1024
DEPTH = 1
GRID_W = 64
LN_EPS = 1e-6
DEEPNORM_ALPHA = (2 * DEPTH) ** 0.25
GLA_HEADS = 4
GLA_DK = 64
GLA_DV = 128
GLA_KEY = GLA_HEADS * GLA_DK
GLA_VAL = GLA_HEADS * GLA_DV
GLA_RANK = 16
GLA_TAU = 16.0
HY_WIDTH = D_MODEL - GLA_VAL
HY_ORDER = 2
HY_CONV = 3
HY_EMB = 33
HY_FH = 64
HY_TARGET = 1e-2
HY_FAST = 0.3
HY_SLOW = 1.5
N_EXPERTS = 32
TOP_K = 4
D_EXPERT = D_MODEL
SWIGLU_ALPHA = 1.702
SWIGLU_LIMIT = 7.0

LANES = 128
VMEM_LIMIT = 56 * 1024 * 1024

TOK_TILE = 512
PROJ_SUB_TILE = 256
GLA_CHUNK = 128
GLA_BLOCK = 256
GLA_BATCH = 4
GLA_SAFE_EXPONENT = 80.0
A_PAD = LANES
HY_GROUP = 8
HY_CH_BLOCK = 16
FILT_TILE = 1024
HY_EMB_PAD = 40
MOE_TILE = 512
FFN_TILES_PER_STEP = 2
COMBINE_TILE = 512
DMA_UNROLL = 8
ROW_TILES = D_MODEL // LANES


def _row_slab(n_rows, j):
    return pl.ds(j, n_rows, stride=ROW_TILES)


def _row_tile(r):
    return pl.ds(pl.multiple_of(r * ROW_TILES, ROW_TILES), ROW_TILES)
SWIGLU_BLOCK = 2 * LANES
NEG_BIG = -1e30


def _cparams(sem):
    return pltpu.CompilerParams(dimension_semantics=sem, vmem_limit_bytes=VMEM_LIMIT)


def _layer_norm(x):
    mu = jnp.mean(x, axis=-1, keepdims=True)
    xc = x - mu
    return xc * lax.rsqrt(jnp.mean(xc * xc, axis=-1, keepdims=True) + LN_EPS)


def _dot(a, b):
    return jnp.dot(a, b, preferred_element_type=F32)


def _dot_nt(a, b):
    return lax.dot_general(a, b, (((1,), (1,)), ((), ())), preferred_element_type=F32)


def _dot_tn(a, b):
    return lax.dot_general(a, b, (((0,), (0,)), ((), ())), preferred_element_type=F32)


def _mod_kernel(c_ref, w_ref, b_ref, o_ref):
    c = c_ref[...]
    s = c * jax.nn.sigmoid(c)
    o_ref[...] = _dot(s.astype(BF16), w_ref[...].astype(BF16)) + b_ref[...]


def _modulation(cvec, ada_w, ada_b):
    rows, d = cvec.shape
    n = ada_w.shape[1]
    tn = 1024
    return pl.pallas_call(
        _mod_kernel,
        out_shape=jax.ShapeDtypeStruct((rows, n), F32),
        grid=(n // tn,),
        in_specs=[
            pl.BlockSpec((rows, d), lambda j: (0, 0)),
            pl.BlockSpec((d, tn), lambda j: (0, j)),
            pl.BlockSpec((1, tn), lambda j: (0, j)),
        ],
        out_specs=pl.BlockSpec((rows, tn), lambda j: (0, j)),
        compiler_params=_cparams(("arbitrary",)),
        name="adaln_mod",
    )(cvec, ada_w, ada_b)


def _inproj_kernel(x_ref, sh_ref, sc_ref, wm_ref, *rest, with_hy):
    if with_hy:
        wh_ref, cw_ref, cb_ref, q_ref, k_ref, v_ref, g_ref, a_ref, u_ref = rest
    else:
        q_ref, k_ref, v_ref, a_ref = rest
    tt = x_ref.shape[1]
    n_sub = max(tt // PROJ_SUB_TILE, 1)
    sub = tt // n_sub
    for s in range(n_sub):
        rows = slice(s * sub, (s + 1) * sub)
        h = _layer_norm(x_ref[0, rows, :]) * (1.0 + sc_ref[0]) + sh_ref[0]
        hb = h.astype(BF16)
        u = _dot(hb, wm_ref[...])
        q_ref[0, rows, :] = u[:, :GLA_KEY].astype(BF16)
        k_ref[0, rows, :] = u[:, GLA_KEY:2 * GLA_KEY].astype(BF16)
        v_ref[0, rows, :] = u[:, 2 * GLA_KEY:2 * GLA_KEY + GLA_VAL].astype(BF16)
        off = 2 * GLA_KEY + GLA_VAL
        if with_hy:
            g_ref[0, rows, :] = u[:, off:off + GLA_VAL].astype(BF16)
            off += GLA_VAL
        a_ref[0, rows, :] = u[:, off:off + A_PAD]
        if with_hy:
            ut = _dot_nt(wh_ref[...], hb)
            lane = lax.broadcasted_iota(jnp.int32, (1, LANES), 1) % GRID_W
            not_first = (lane != 0).astype(F32)
            not_last = (lane != GRID_W - 1).astype(F32)
            for j in range(sub // LANES):
                c = ut[:, j * LANES:(j + 1) * LANES]
                left = pltpu.roll(c, 1, axis=1) * not_first
                right = pltpu.roll(c, LANES - 1, axis=1) * not_last
                y = cw_ref[0] * left + cw_ref[1] * c + cw_ref[2] * right + cb_ref[...]
                lanes = slice(s * sub + j * LANES, s * sub + (j + 1) * LANES)
                u_ref[0, :, lanes] = y.astype(BF16)


def _input_projection(x, shift, scale, w_main, w_hy_t=None, conv_w=None, conv_b=None):
    b, l, d = x.shape
    tt = min(TOK_TILE, l)
    with_hy = w_hy_t is not None
    nm = w_main.shape[1]
    tok = lambda width, dt: jax.ShapeDtypeStruct((b, l, width), dt)
    tok_spec = lambda width: pl.BlockSpec((1, tt, width), lambda i, j: (i, j, 0))
    in_specs = [
        pl.BlockSpec((1, tt, d), lambda i, j: (i, j, 0)),
        pl.BlockSpec((1, 1, d), lambda i, j: (i, 0, 0)),
        pl.BlockSpec((1, 1, d), lambda i, j: (i, 0, 0)),
        pl.BlockSpec((d, nm), lambda i, j: (0, 0)),
    ]
    args = [x, shift, scale, w_main]
    if with_hy:
        ch = w_hy_t.shape[0]
        in_specs += [
            pl.BlockSpec((ch, d), lambda i, j: (0, 0)),
            pl.BlockSpec((HY_CONV, ch, LANES), lambda i, j: (0, 0, 0)),
            pl.BlockSpec((ch, LANES), lambda i, j: (0, 0)),
        ]
        args += [w_hy_t, conv_w, conv_b]
        out_shape = [tok(GLA_KEY, BF16), tok(GLA_KEY, BF16), tok(GLA_VAL, BF16), tok(GLA_VAL, BF16),
                     tok(A_PAD, F32), jax.ShapeDtypeStruct((b, ch, l), BF16)]
        out_specs = [tok_spec(GLA_KEY), tok_spec(GLA_KEY), tok_spec(GLA_VAL), tok_spec(GLA_VAL),
                     tok_spec(A_PAD), pl.BlockSpec((1, ch, tt), lambda i, j: (i, 0, j))]
    else:
        out_shape = [tok(GLA_KEY, BF16), tok(GLA_KEY, BF16), tok(GLA_VAL, BF16), tok(A_PAD, F32)]
        out_specs = [tok_spec(GLA_KEY), tok_spec(GLA_KEY), tok_spec(GLA_VAL), tok_spec(A_PAD)]
    return pl.pallas_call(
        functools.partial(_inproj_kernel, with_hy=with_hy),
        out_shape=out_shape,
        grid=(b, l // tt),
        in_specs=in_specs,
        out_specs=out_specs,
        compiler_params=_cparams(("parallel", "arbitrary")),
        name="in_proj_hy" if with_hy else "in_proj_ctx",
    )(*args)


def _gla_kernel(*refs, n_ctx_blocks, chunk, n_batch, n_levels, safe_exponent):
    ins = [refs[0:8], refs[8:16]]
    wa_ref, ba_ref, cw_ref, cm_ref, hm_ref, bd_ref, dq_ref, dk_ref, lm_ref = refs[16:25]
    outs = refs[25:27]
    st_ref = refs[27]
    s = pl.program_id(1)
    blk = ins[0][4].shape[1]
    n_chunks = blk // chunk

    @pl.when(s == 0)
    def _():
        st_ref[...] = jnp.zeros_like(st_ref)

    def two_pass(m, g_hi, g_lo):
        return _dot(m, g_hi) + _dot(m, g_lo)

    is_ctx = s < n_ctx_blocks
    bdmask = bd_ref[...]
    chains = [(d, bi) for d in range(2) for bi in range(n_batch)]

    def pre_activation(d, bi):
        a = jnp.where(is_ctx, ins[d][3][bi], ins[d][7][bi])
        return _dot(a.astype(BF16), wa_ref[d]) + ba_ref[d]

    def advance(d, bi, z, single_reference):
        qc_ref, kc_ref, vc_ref, _, ql_ref, kl_ref, vl_ref, _ = ins[d]
        q = jnp.where(is_ctx, qc_ref[bi], ql_ref[bi]).astype(F32)
        k = jnp.where(is_ctx, kc_ref[bi], kl_ref[bi]).astype(F32)
        v = jnp.where(is_ctx, vc_ref[bi], vl_ref[bi])
        g = (jnp.minimum(z, 0.0) - jnp.log(1.0 + jnp.exp(-jnp.abs(z)))) * (1.0 / GLA_TAU)
        g_hi = g.astype(BF16)
        g_lo = (g - g_hi.astype(F32)).astype(BF16)
        cum = two_pass(cw_ref[d], g_hi, g_lo)
        edge = (lambda c: (c + 1) * chunk - 1) if d == 0 else (lambda c: c * chunk)
        tot = jnp.concatenate([jnp.broadcast_to(cum[edge(c):edge(c) + 1], (chunk, GLA_KEY))
                               for c in range(n_chunks)], axis=0)
        qs_all = (q * jnp.exp(cum)).astype(BF16)
        kst_all = (k * jnp.exp(tot - cum)).astype(BF16)
        dec_all = jnp.exp(tot)

        if single_reference:
            ks_all = (k * jnp.exp(-cum)).astype(BF16)
            cmask = cm_ref[d]
            rows_out = []
            for c in range(n_chunks):
                rows = slice(c * chunk, (c + 1) * chunk)
                parts = []
                for h in range(GLA_HEADS):
                    a_h = _dot_nt(qs_all[rows] * hm_ref[h], ks_all[rows]) * cmask
                    parts.append(_dot(a_h.astype(BF16), v[rows, h * GLA_DV:(h + 1) * GLA_DV]))
                rows_out.append(jnp.concatenate(parts, axis=1))
            o_intra = jnp.concatenate(rows_out, axis=0)
        else:
            def level(lv, acc):
                ql = (q * jnp.exp(two_pass(dq_ref[d, lv], g_hi, g_lo))).astype(BF16)
                kl = (k * jnp.exp(two_pass(dk_ref[d, lv], g_hi, g_lo))).astype(BF16)
                msk = lm_ref[d, lv]
                parts = []
                for h in range(GLA_HEADS):
                    a_h = _dot_nt(ql * hm_ref[h], kl) * msk
                    parts.append(_dot(a_h.astype(BF16), v[:, h * GLA_DV:(h + 1) * GLA_DV]))
                return acc + jnp.concatenate(parts, axis=1)
            o_intra = lax.fori_loop(0, n_levels, level, jnp.zeros((blk, GLA_VAL), F32))

        for n in range(n_chunks):
            ci = n if d == 0 else n_chunks - 1 - n
            rows = slice(ci * chunk, (ci + 1) * chunk)
            o = _dot_nt(qs_all[rows], st_ref[d, bi].astype(BF16)) + o_intra[rows]
            outs[d][bi, rows, :] = o.astype(outs[d].dtype)
            st_ref[d, bi] = (st_ref[d, bi] * dec_all[ci * chunk:ci * chunk + 1]
                             + _dot_tn(v[rows], kst_all[rows]) * bdmask)

    zs = [pre_activation(d, bi) for d, bi in chains]
    z_low = zs[0]
    for z in zs[1:]:
        z_low = jnp.minimum(z_low, z)
    bound = (jnp.maximum(-jnp.min(z_low), 0.0) + math.log(2.0)) * (chunk / GLA_TAU)

    def step(single_reference):
        def run():
            for (d, bi), z in zip(chains, zs):
                advance(d, bi, z, single_reference)
        return run

    lax.cond(bound < safe_exponent, step(True), step(False))


def _gla_level_tables(blk, chunk):
    sizes = []
    s = chunk // 2
    while s >= 1:
        sizes.append(s)
        s //= 2
    n_lv = len(sizes) + 1
    dq = np.zeros((2, n_lv, blk, blk), np.float32)
    dk = np.zeros((2, n_lv, blk, blk), np.float32)
    lm = np.zeros((2, n_lv, blk, blk), np.float32)
    idx = np.arange(blk)
    for lv, s in enumerate(sizes):
        blk_id = idx // (2 * s)
        right = (idx % (2 * s)) >= s
        b = blk_id * 2 * s + s
        same = blk_id[:, None] == blk_id[None, :]
        m = idx[None, :]
        dq[0, lv] = (right[:, None] & (m >= b[:, None]) & (m <= idx[:, None]))
        dk[0, lv] = (~right[:, None] & (m > idx[:, None]) & (m < b[:, None]))
        lm[0, lv] = same & right[:, None] & ~right[None, :]
        dq[1, lv] = (~right[:, None] & (m >= idx[:, None]) & (m < b[:, None]))
        dk[1, lv] = (right[:, None] & (m >= b[:, None]) & (m < idx[:, None]))
        lm[1, lv] = same & ~right[:, None] & right[None, :]
    lm[:, n_lv - 1] = np.eye(blk, dtype=np.float32)
    return dq, dk, lm, n_lv


def _gla_scan(ctx_qkva, lat_qkva, wa, ba, chunk=GLA_CHUNK, blk=GLA_BLOCK, safe_exponent=GLA_SAFE_EXPONENT):
    qc, kc, vc, ac = ctx_qkva
    ql, kl, vl, al = lat_qkva
    b, l, _ = ql.shape
    ctx_len = qc.shape[1]
    n_ctx = ctx_len // blk
    n_lat = l // blk
    c = chunk
    idx = np.arange(c)
    tri = np.stack([idx[:, None] >= idx[None, :], idx[:, None] <= idx[None, :]]).astype(np.float32)
    cw = np.stack([np.kron(np.eye(blk // c, dtype=np.float32), tri[dd]) for dd in range(2)])
    dq, dk, lm, n_levels = _gla_level_tables(blk, c)
    hmask = np.zeros((GLA_HEADS, 1, GLA_KEY), np.float32)
    for h in range(GLA_HEADS):
        hmask[h, 0, h * GLA_DK:(h + 1) * GLA_DK] = 1.0
    bd = (np.arange(GLA_VAL)[:, None] // GLA_DV == np.arange(GLA_KEY)[None, :] // GLA_DK).astype(np.float32)

    nb = GLA_BATCH if b % GLA_BATCH == 0 else 1

    def lat_block(d):
        def f(i, s):
            t = jnp.maximum(s - n_ctx, 0)
            return (i, t if d == 0 else n_lat - 1 - t, 0)
        return f

    def ctx_block(d):
        def f(i, s):
            t = jnp.minimum(s, n_ctx - 1)
            return (i, t if d == 0 else n_ctx - 1 - t, 0)
        return f

    widths = (GLA_KEY, GLA_KEY, GLA_VAL, A_PAD)
    dir_specs = lambda d: ([pl.BlockSpec((nb, blk, w), ctx_block(d)) for w in widths]
                           + [pl.BlockSpec((nb, blk, w), lat_block(d)) for w in widths])
    consts = [wa, ba, jnp.asarray(cw, BF16), jnp.asarray(tri, F32), jnp.asarray(hmask, BF16), jnp.asarray(bd, F32),
              jnp.asarray(dq, BF16), jnp.asarray(dk, BF16), jnp.asarray(lm, F32)]
    full = lambda arr: pl.BlockSpec(arr.shape, lambda i, s: (0,) * arr.ndim)
    o_sds = jax.ShapeDtypeStruct((b, l, GLA_VAL), BF16)
    dir_args = [qc, kc, vc, ac, ql, kl, vl, al]
    return pl.pallas_call(
        functools.partial(_gla_kernel, n_ctx_blocks=n_ctx, chunk=c, n_batch=nb, n_levels=n_levels,
                          safe_exponent=safe_exponent),
        out_shape=[o_sds, o_sds],
        grid=(b // nb, n_ctx + n_lat),
        in_specs=dir_specs(0) + dir_specs(1) + [full(arr) for arr in consts],
        out_specs=[pl.BlockSpec((nb, blk, GLA_VAL), lat_block(0)), pl.BlockSpec((nb, blk, GLA_VAL), lat_block(1))],
        scratch_shapes=[pltpu.VMEM((2, nb, GLA_VAL, GLA_KEY), F32)],
        compiler_params=_cparams(("parallel", "arbitrary")),
        name="gla_scan",
    )(*dir_args, *dir_args, *consts)


def _dft_constants(r1):
    n = r1 * LANES
    h = r1 // 2
    k1 = np.arange(r1)
    f1 = np.exp(-2j * np.pi * np.outer(k1, k1) / r1)
    f2 = np.exp(-2j * np.pi * np.outer(np.arange(LANES), np.arange(LANES)) / LANES)
    tw = np.exp(-2j * np.pi * np.outer(k1, np.arange(LANES)) / n)
    fa_c = np.block([[f1.real[:, :h], -f1.imag[:, :h]], [f1.imag[:, :h], f1.real[:, :h]]])
    fa_r = np.concatenate([f1.real, f1.imag], axis=0)
    gc = np.block([[f2.real, f2.imag], [-f2.imag, f2.real]])
    gci = np.block([[f2.real, -f2.imag], [f2.imag, f2.real]])
    fai = np.block([[f1.real[:h], f1.imag[:h]], [-f1.imag[:h], f1.real[:h]]]) / n
    tw_lane = np.tile(tw, (1, HY_GROUP))
    tw_row = np.tile(tw, (HY_GROUP, 1))
    f = lambda a: jnp.asarray(a, F32)
    return dict(fa_c=f(fa_c), fa_r=f(fa_r), gc=f(gc), gci=f(gci), fai=f(fai),
                twl_r=f(tw_lane.real), twl_i=f(tw_lane.imag), twr_r=f(tw_row.real), twr_i=f(tw_row.imag))


def _fwd_dft(rhs, fa, gc, twl_r, twl_i, r1):
    a = _dot(fa, rhs)
    ar, ai = a[:r1], a[r1:]
    br = (ar * twl_r - ai * twl_i).astype(BF16)
    bi = (ar * twl_i + ai * twl_r).astype(BF16)
    lhs = jnp.concatenate(
        [jnp.concatenate([br[:, c * LANES:(c + 1) * LANES], bi[:, c * LANES:(c + 1) * LANES]], axis=1)
         for c in range(HY_GROUP)], axis=0)
    return _dot(lhs, gc)


def _inv_dft(yr, yi, gci, fai, twr_r, twr_i, r1):
    lhs = jnp.concatenate([yr, yi], axis=1).astype(BF16)
    c = _dot(lhs, gci)
    cr, ci = c[:, :LANES], c[:, LANES:]
    dr = (cr * twr_r + ci * twr_i).astype(BF16)
    di = (ci * twr_r - cr * twr_i).astype(BF16)
    rhs = jnp.concatenate(
        [jnp.concatenate([dr[g * r1:(g + 1) * r1], di[g * r1:(g + 1) * r1]], axis=0)
         for g in range(HY_GROUP)], axis=1)
    return _dot(fai, rhs)


def _filter_mlp_kernel(zt_ref, tn_ref, rate_ref, w1_ref, b1_ref, w2_ref, b2_ref, fr_ref, wo_ref, o_ref):
    hp = lax.Precision.HIGHEST
    lt = tn_ref.shape[1]
    lanes = lambda ref: jnp.concatenate([ref[...]] * (lt // LANES), axis=1)
    fr = lanes(fr_ref)
    hid = jnp.sin(fr * (jnp.dot(w1_ref[...], zt_ref[...], precision=hp, preferred_element_type=F32) + lanes(b1_ref)))
    hid = jnp.sin(fr * (jnp.dot(w2_ref[...], hid, precision=hp, preferred_element_type=F32) + lanes(b2_ref)))
    window = jnp.exp(-tn_ref[0:1, :] * lanes(rate_ref)) * tn_ref[1:2, :]
    for o in range(HY_ORDER):
        o_ref[o] = jnp.dot(wo_ref[o, 0], hid, precision=hp, preferred_element_type=F32) * window


def _filter_mlp(z_t, tn_rows, rate, w1_t, b1, w2_t, b2, freq, wo_t):
    emb, n2l = z_t.shape
    l = n2l // 2
    lt = min(FILT_TILE, l)
    nt = l // lt
    ch = rate.shape[0]
    full = lambda shape: pl.BlockSpec(shape, lambda d, j: (0,) * len(shape))
    return pl.pallas_call(
        _filter_mlp_kernel,
        out_shape=jax.ShapeDtypeStruct((HY_ORDER, ch, n2l), F32),
        grid=(2, nt),
        in_specs=[
            pl.BlockSpec((emb, lt), lambda d, j: (0, d * nt + j)),
            pl.BlockSpec((8, lt), lambda d, j: (0, d * nt + j)),
            full((ch, LANES)),
            full((HY_FH, emb)), full((HY_FH, LANES)), full((HY_FH, HY_FH)), full((HY_FH, LANES)),
            full((HY_FH, LANES)),
            pl.BlockSpec((HY_ORDER, 1, ch, HY_FH), lambda d, j: (0, d, 0, 0)),
        ],
        out_specs=pl.BlockSpec((HY_ORDER, ch, lt), lambda d, j: (0, 0, d * nt + j)),
        compiler_params=_cparams(("arbitrary", "arbitrary")),
        name="hyena_filter_mlp",
    )(z_t, tn_rows, rate, w1_t, b1, w2_t, b2, freq, wo_t)


def _filter_fft_kernel(f_ref, fa_ref, gc_ref, twl_r_ref, twl_i_ref, h_ref, *, r1):
    fa = fa_ref[...].astype(BF16)
    gc = gc_ref[...].astype(BF16)
    nc = f_ref.shape[1]
    for g0 in range(0, nc, HY_GROUP):
        rhs = jnp.concatenate([f_ref[0, g0 + c].astype(BF16) for c in range(HY_GROUP)], axis=1)
        x = _fwd_dft(rhs, fa, gc, twl_r_ref[...], twl_i_ref[...], r1)
        for c in range(HY_GROUP):
            h_ref[0, g0 + c] = x[c * r1:(c + 1) * r1].astype(h_ref.dtype)


def _filter_spectra(filt, consts, r1):
    order, ch = filt.shape[:2]
    nc = HY_CH_BLOCK
    full = lambda a: pl.BlockSpec(a.shape, lambda o, j: (0,) * a.ndim)
    cs = [consts["fa_r"], consts["gc"], consts["twl_r"], consts["twl_i"]]
    return pl.pallas_call(
        functools.partial(_filter_fft_kernel, r1=r1),
        out_shape=jax.ShapeDtypeStruct((order, ch, r1, 2 * LANES), BF16),
        grid=(order, ch // nc),
        in_specs=[pl.BlockSpec((1, nc, r1, LANES), lambda o, j: (o, j, 0, 0))] + [full(a) for a in cs],
        out_specs=pl.BlockSpec((1, nc, r1, 2 * LANES), lambda o, j: (o, j, 0, 0)),
        compiler_params=_cparams(("arbitrary", "arbitrary")),
        name="hyena_filter_fft",
    )(filt, *cs)


def _hyena_kernel(dbias_ref, v_ref, x1_ref, x2_ref, h_ref, fa_ref, gc_ref, gci_ref, fai_ref,
                  twl_r_ref, twl_i_ref, twr_r_ref, twr_i_ref, y_ref, *, r1):
    fa = fa_ref[...].astype(BF16)
    gc = gc_ref[...].astype(BF16)
    gci = gci_ref[...].astype(BF16)
    fai = fai_ref[...].astype(BF16)
    twl_r, twl_i = twl_r_ref[...], twl_i_ref[...]
    twr_r, twr_i = twr_r_ref[...], twr_i_ref[...]
    nc = v_ref.shape[2]
    half = r1 // 2
    c_base = pl.program_id(0) * nc

    def conv(sig, sig_b, order, g0):
        rhs = jnp.concatenate(
            [jnp.concatenate([sig_b[c][0], sig_b[c][1]], axis=0) for c in range(HY_GROUP)], axis=1)
        x = _fwd_dft(rhs, fa, gc, twl_r, twl_i, r1)
        xr, xi = x[:, :LANES], x[:, LANES:]
        hh = jnp.concatenate([h_ref[order, g0 + c] for c in range(HY_GROUP)], axis=0).astype(F32)
        hr, hi = hh[:, :LANES], hh[:, LANES:]
        y = _inv_dft(xr * hr - xi * hi, xr * hi + xi * hr, gci, fai, twr_r, twr_i, r1)
        out = []
        for c in range(HY_GROUP):
            dcoef = dbias_ref[order, c_base + g0 + c]
            yc = y[:, c * LANES:(c + 1) * LANES]
            out.append([yc[:half] + dcoef * sig[c][0], yc[half:] + dcoef * sig[c][1]])
        return out

    for g0 in range(0, nc, HY_GROUP):
        v_b = [[v_ref[0, b, g0 + c] for b in range(2)] for c in range(HY_GROUP)]
        v = [[t.astype(F32) for t in pair] for pair in v_b]
        y1 = conv(v, v_b, 0, g0)
        z = [[x1_ref[0, b, g0 + c].astype(F32) * y1[c][b] for b in range(2)] for c in range(HY_GROUP)]
        z_b = [[t.astype(BF16) for t in pair] for pair in z]
        y2 = conv(z, z_b, 1, g0)
        for c in range(HY_GROUP):
            for b in range(2):
                y_ref[0, b, g0 + c] = (x2_ref[0, b, g0 + c].astype(F32) * y2[c][b]).astype(y_ref.dtype)


def _hyena_conv(u_t, spectra, d_bias, consts, r1):
    bp, _, ch3, half, _ = u_t.shape
    ch = ch3 // 3
    nc = HY_CH_BLOCK
    nblk = ch // nc
    names = ["fa_c", "gc", "gci", "fai", "twl_r", "twl_i", "twr_r", "twr_i"]
    cs = [consts[k] for k in names]
    full = lambda a: pl.BlockSpec(a.shape, lambda j, p: (0,) * a.ndim)
    part = lambda k: pl.BlockSpec((1, 2, nc, half, LANES), lambda j, p: (p, 0, k * nblk + j, 0, 0))
    return pl.pallas_call(
        functools.partial(_hyena_kernel, r1=r1),
        out_shape=jax.ShapeDtypeStruct((bp, 2, ch, half, LANES), BF16),
        grid=(nblk, bp),
        in_specs=[pl.BlockSpec(memory_space=pltpu.SMEM), part(0), part(1), part(2),
                  pl.BlockSpec((HY_ORDER, nc, r1, 2 * LANES), lambda j, p: (0, j, 0, 0))] + [full(a) for a in cs],
        out_specs=pl.BlockSpec((1, 2, nc, half, LANES), lambda j, p: (p, 0, j, 0, 0)),
        compiler_params=_cparams(("arbitrary", "arbitrary")),
        name="hyena_conv",
    )(d_bias, u_t, u_t, u_t, spectra, *cs)


def _filter_inputs(l):
    n = jnp.arange(2 * l, dtype=jnp.int32)
    t = jnp.where(n < l, n, 2 * l - n).astype(F32)
    valid = (n != l).astype(F32)
    t_norm = t / (l - 1)
    bands = (HY_EMB - 1) // 2
    f = jnp.linspace(1e-4, bands - 1, bands, dtype=F32)
    ang = (2.0 * math.pi * t / l)[:, None] * f[None, :]
    z = jnp.concatenate([t_norm[:, None], jnp.cos(ang), -jnp.sin(ang)], -1)
    zz = jnp.pad(z, ((0, 0), (0, HY_EMB_PAD - HY_EMB))).T
    tn_rows = jnp.zeros((8, 2 * l), F32).at[0].set(t_norm).at[1].set(valid)
    deltas = jnp.linspace(math.log(HY_TARGET) / HY_SLOW, math.log(HY_TARGET) / HY_FAST, HY_WIDTH, dtype=F32)
    rate = jnp.broadcast_to(jnp.abs(deltas)[:, None], (HY_WIDTH, LANES))
    return zz, tn_rows, rate


def _outproj_kernel(of_ref, ob_ref, g_ref, yh_ref, x_ref, gate_ref, sh_ref, sc_ref,
                    wg_ref, wh_ref, ng_ref, l1g_ref, l1b_ref, wr_ref, rb_ref, ltri_ref,
                    xm_ref, h2_ref, se_ref, sr_ref, sw_ref, cnt_ref, carry_ref):
    @pl.when((pl.program_id(0) == 0) & (pl.program_id(1) == 0))
    def _():
        carry_ref[...] = jnp.zeros_like(carry_ref)

    tt = x_ref.shape[1]
    sub = ltri_ref.shape[0]
    for s in range(tt // sub):
        rows = slice(s * sub, (s + 1) * sub)
        o = of_ref[0, rows, :].astype(F32) + ob_ref[0, rows, :].astype(F32)
        g = g_ref[0, rows, :].astype(F32)
        parts = []
        for h in range(GLA_HEADS):
            oh = o[:, h * GLA_DV:(h + 1) * GLA_DV]
            parts.append(oh * lax.rsqrt(jnp.mean(oh * oh, axis=-1, keepdims=True) + LN_EPS))
        y_gla = jnp.concatenate(parts, axis=1) * ng_ref[...] * (g * jax.nn.sigmoid(g))
        f = _dot(y_gla.astype(BF16), wg_ref[...]) + _dot_tn(yh_ref[0, :, rows], wh_ref[...])
        x_mid = (_layer_norm(DEEPNORM_ALPHA * x_ref[0, rows, :] + gate_ref[0] * f) * l1g_ref[...]
                 + l1b_ref[...])
        xm_ref[0, rows, :] = x_mid
        h2 = _layer_norm(x_mid) * (1.0 + sc_ref[0]) + sh_ref[0]
        for j in range(ROW_TILES):
            h2_ref[0, pl.ds(s * sub * ROW_TILES + j, sub, stride=ROW_TILES), :] = h2[:, j * LANES:(j + 1) * LANES]

        logits = _dot(h2.astype(BF16), wr_ref[...]) + rb_ref[...]
        lane = lax.broadcasted_iota(jnp.int32, logits.shape, 1).astype(F32)
        hits, idxs, exps = [], [], []
        m0 = None
        for _ in range(TOP_K):
            m = jnp.max(logits, axis=-1, keepdims=True)
            idx = jnp.min(jnp.where(logits == m, lane, float(LANES)), axis=-1, keepdims=True)
            hit = lane == idx
            m0 = m if m0 is None else m0
            hits.append(hit)
            idxs.append(idx)
            exps.append(jnp.exp(m - m0))
            logits = jnp.where(hit, NEG_BIG, logits)
        denom = exps[0]
        sel = jnp.where(hits[0], 1.0, 0.0)
        for kk in range(1, TOP_K):
            denom = denom + exps[kk]
            sel = sel + jnp.where(hits[kk], 1.0, 0.0)
        rank_all = _dot(ltri_ref[...], sel.astype(BF16)) + carry_ref[0:1, :]
        carry_ref[0:1, :] = carry_ref[0:1, :] + jnp.sum(sel, axis=0, keepdims=True)
        se = jnp.zeros(logits.shape, F32)
        sr = jnp.zeros(logits.shape, F32)
        sw = jnp.zeros(logits.shape, F32)
        for kk in range(TOP_K):
            rk = jnp.sum(jnp.where(hits[kk], rank_all, 0.0), axis=-1, keepdims=True)
            col = lane == float(kk)
            se = jnp.where(col, idxs[kk], se)
            sr = jnp.where(col, rk, sr)
            sw = jnp.where(col, exps[kk] / denom, sw)
        se_ref[0, rows, :] = se.astype(jnp.int32)
        sr_ref[0, rows, :] = sr.astype(jnp.int32)
        sw_ref[0, rows, :] = sw
    cnt_ref[...] = jnp.broadcast_to(carry_ref[0:1, :], cnt_ref.shape).astype(jnp.int32)


def _output_projection(o_f, o_b, g, y_hy_t, x, gate1, shift2, scale2, w_gla, w_hy, norm_g, ln1_g, ln1_b, wr, rb):
    b, l, d = x.shape
    tt = TOK_TILE
    tok = lambda w: pl.BlockSpec((1, tt, w), lambda i, j: (i, j, 0))
    row = lambda: pl.BlockSpec((1, 1, d), lambda i, j: (i, 0, 0))
    full = lambda a: pl.BlockSpec(a.shape, lambda i, j: (0,) * a.ndim)
    sub = tt
    ltri = jnp.asarray(np.tril(np.ones((sub, sub), np.float32), -1), BF16)
    consts = [w_gla, w_hy, norm_g, ln1_g, ln1_b, wr, rb, ltri]
    lane_i = jax.ShapeDtypeStruct((b, l, LANES), jnp.int32)
    return pl.pallas_call(
        _outproj_kernel,
        out_shape=[jax.ShapeDtypeStruct((b, l, d), F32), jax.ShapeDtypeStruct((b, l * ROW_TILES, LANES), F32),
                   lane_i, lane_i, jax.ShapeDtypeStruct((b, l, LANES), F32),
                   jax.ShapeDtypeStruct((8, LANES), jnp.int32)],
        grid=(b, l // tt),
        in_specs=[
            tok(GLA_VAL), tok(GLA_VAL), tok(GLA_VAL),
            pl.BlockSpec((1, HY_WIDTH, tt), lambda i, j: (i, 0, j)),
            tok(d), row(), row(), row(),
        ] + [full(a) for a in consts],
        out_specs=[tok(d), pl.BlockSpec((1, tt * ROW_TILES, LANES), lambda i, j: (i, j, 0)),
                   tok(LANES), tok(LANES), tok(LANES),
                   pl.BlockSpec((8, LANES), lambda i, j: (0, 0))],
        scratch_shapes=[pltpu.VMEM((8, LANES), F32)],
        compiler_params=_cparams(("arbitrary", "arbitrary")),
        name="out_proj_router",
    )(o_f, o_b, g, y_hy_t, x, gate1, shift2, scale2, *consts)


def _expert_weight_layout(w1_ref, w2_ref, p_ref, w1o_ref, w2o_ref):
    p = p_ref[...]
    for j in range(w1_ref.shape[2] // SWIGLU_BLOCK):
        cols = slice(j * SWIGLU_BLOCK, (j + 1) * SWIGLU_BLOCK)
        w1o_ref[0, :, cols] = _dot(w1_ref[0, :, cols].astype(BF16), p).astype(BF16)
    w2o_ref[0] = w2_ref[0].astype(BF16)


def _row_copy_groups(n_tokens, make_copy):
    def group(gidx, carry):
        for u in range(DMA_UNROLL):
            r = gidx * DMA_UNROLL + u
            for kk in range(TOP_K):
                make_copy(r, kk, r * TOP_K + kk).start(priority=kk % 2)
        return carry
    lax.fori_loop(0, n_tokens // DMA_UNROLL, group, 0)


def _dispatch_kernel(base_ref, cnt_ref, nt_ref, slot_ref, h2_ref, w1_ref, w2_ref, p_ref,
                     xs_hbm, w1o_ref, w2o_ref, zrow_ref, zblk_ref, sem, zsem):
    tt = h2_ref.shape[0] // ROW_TILES

    def row_copy(r, kk, a):
        return pltpu.make_async_copy(h2_ref.at[_row_tile(r)], xs_hbm.at[_row_tile(slot_ref[a])], sem)

    _row_copy_groups(tt, row_copy)
    _expert_weight_layout(w1_ref, w2_ref, p_ref, w1o_ref, w2o_ref)
    for _ in range(TOP_K):
        pltpu.make_async_copy(h2_ref, xs_hbm.at[pl.ds(0, tt * ROW_TILES)], sem).wait()

    @pl.when(pl.program_id(0) == pl.num_programs(0) - 1)
    def _():
        zrow_ref[...] = jnp.zeros_like(zrow_ref)

        def per_expert(e, carry):
            n = cnt_ref[e]
            end = ((n + (MOE_TILE - 1)) // MOE_TILE) * MOE_TILE

            def fill(r, c):
                pltpu.make_async_copy(zrow_ref, xs_hbm.at[_row_tile(base_ref[e] + r)], zsem).start()
                return c

            def drain(r, c):
                pltpu.make_async_copy(zrow_ref, xs_hbm.at[_row_tile(0)], zsem).wait()
                return c

            lax.fori_loop(n, end, fill, 0)
            lax.fori_loop(n, end, drain, 0)
            return carry

        lax.fori_loop(0, N_EXPERTS, per_expert, 0)

        zblk_ref[...] = jnp.zeros_like(zblk_ref)
        tile_rows = MOE_TILE * ROW_TILES
        n_all = xs_hbm.shape[0] // tile_rows

        def tile_copy(ti):
            row0 = pl.multiple_of(ti * tile_rows, tile_rows)
            return pltpu.make_async_copy(zblk_ref, xs_hbm.at[pl.ds(row0, tile_rows)], zsem)

        def fill_tile(ti, c):
            tile_copy(ti).start()
            return c

        def drain_tile(ti, c):
            tile_copy(ti).wait()
            return c

        lax.fori_loop(nt_ref[0], n_all, fill_tile, 0)
        lax.fori_loop(nt_ref[0], n_all, drain_tile, 0)


def _dispatch(h2, slot_flat, base, counts, n_tiles, n_slots, w1, w2):
    t = h2.shape[0] // ROW_TILES
    ne, d, f2 = w1.shape
    assert t % (ne * DMA_UNROLL) == 0, "one dispatch step per expert"
    tt = t // ne
    src = np.concatenate([np.arange(0, SWIGLU_BLOCK, 2), np.arange(1, SWIGLU_BLOCK, 2)])
    perm = np.zeros((SWIGLU_BLOCK, SWIGLU_BLOCK), np.float32)
    perm[src, np.arange(SWIGLU_BLOCK)] = 1.0
    exp_blk = lambda shape: pl.BlockSpec((1,) + shape, lambda i, *_: (i, 0, 0))
    return pl.pallas_call(
        _dispatch_kernel,
        out_shape=[jax.ShapeDtypeStruct((n_slots * ROW_TILES, LANES), F32),
                   jax.ShapeDtypeStruct(w1.shape, BF16), jax.ShapeDtypeStruct(w2.shape, BF16)],
        grid_spec=pltpu.PrefetchScalarGridSpec(
            num_scalar_prefetch=3,
            grid=(ne,),
            in_specs=[pl.BlockSpec((tt * TOP_K,), lambda i, *_: (i,), memory_space=pltpu.SMEM),
                      pl.BlockSpec((tt * ROW_TILES, LANES), lambda i, *_: (i, 0)),
                      exp_blk((d, f2)), exp_blk(w2.shape[1:]),
                      pl.BlockSpec((SWIGLU_BLOCK, SWIGLU_BLOCK), lambda i, *_: (0, 0))],
            out_specs=[pl.BlockSpec(memory_space=pl.ANY), exp_blk((d, f2)), exp_blk(w2.shape[1:])],
            scratch_shapes=[pltpu.VMEM((ROW_TILES, LANES), F32), pltpu.VMEM((MOE_TILE * ROW_TILES, LANES), F32),
                            pltpu.SemaphoreType.DMA, pltpu.SemaphoreType.DMA],
        ),
        compiler_params=_cparams(("arbitrary",)),
        name="moe_dispatch",
    )(base, counts, n_tiles, slot_flat, h2, w1, w2, jnp.asarray(perm, BF16))


def _ffn_kernel(te_ref, nt_ref, xs_ref, *refs):
    ys_ref = refs[-1]
    tm = MOE_TILE
    rows = tm * ROW_TILES
    n_here = jnp.clip(nt_ref[0] - pl.program_id(0) * FFN_TILES_PER_STEP, 0, FFN_TILES_PER_STEP)

    def tile(t):
        w1_ref, b1_ref, w2_ref, b2_ref = refs[4 * t:4 * t + 4]
        x = jnp.concatenate([xs_ref[pl.ds(t * rows + j, tm, stride=ROW_TILES), :] for j in range(ROW_TILES)], axis=1)
        hid = _dot(x.astype(BF16), w1_ref[0]) + b1_ref[0]
        acts = []
        for j in range(hid.shape[1] // SWIGLU_BLOCK):
            glu = jnp.minimum(hid[:, j * SWIGLU_BLOCK:j * SWIGLU_BLOCK + LANES], SWIGLU_LIMIT)
            lin = jnp.clip(hid[:, j * SWIGLU_BLOCK + LANES:(j + 1) * SWIGLU_BLOCK], -SWIGLU_LIMIT, SWIGLU_LIMIT)
            acts.append((glu * jax.nn.sigmoid(SWIGLU_ALPHA * glu) * (lin + 1.0)).astype(BF16))
        y = _dot(jnp.concatenate(acts, axis=1), w2_ref[0]) + b2_ref[0]
        for j in range(ROW_TILES):
            ys_ref[pl.ds(t * rows + j, tm, stride=ROW_TILES), :] = y[:, j * LANES:(j + 1) * LANES]

    for n_valid in range(FFN_TILES_PER_STEP + 1):
        @pl.when(n_here == n_valid)
        def _(n_valid=n_valid):
            for t in range(n_valid):
                tile(t)
            for t in range(n_valid, FFN_TILES_PER_STEP):
                ys_ref[t * rows:(t + 1) * rows, :] = jnp.zeros((rows, LANES), F32)


def _expert_ffn(xs, tile_expert, n_tiles, w1p, b1p, w2b, b2):
    n_slots = xs.shape[0] // ROW_TILES
    d = w1p.shape[1]
    tps = FFN_TILES_PER_STEP
    tm = MOE_TILE
    f2 = w1p.shape[2]
    assert n_slots % (tm * tps) == 0
    rows_blk = (tps * tm * ROW_TILES, LANES)
    weights = []
    for t in range(tps):
        exp_blk = lambda i, te, nt, t=t: (te[i * tps + t], 0, 0)
        weights += [pl.BlockSpec((1, d, f2), exp_blk), pl.BlockSpec((1, 1, f2), exp_blk),
                    pl.BlockSpec((1, f2 // 2, d), exp_blk), pl.BlockSpec((1, 1, d), exp_blk)]
    return pl.pallas_call(
        _ffn_kernel,
        out_shape=jax.ShapeDtypeStruct(xs.shape, F32),
        grid_spec=pltpu.PrefetchScalarGridSpec(
            num_scalar_prefetch=2,
            grid=(n_slots // (tm * tps),),
            in_specs=[
                pl.BlockSpec(rows_blk, lambda i, te, nt: (jnp.minimum(i, (nt[0] - 1) // tps), 0)),
            ] + weights,
            out_specs=pl.BlockSpec(rows_blk, lambda i, te, nt: (i, 0)),
        ),
        compiler_params=_cparams(("arbitrary",)),
        name="moe_expert_ffn",
    )(tile_expert, n_tiles, xs, *([w1p, b1p, w2b, b2] * tps))


def _combine_kernel(slot_ref, slot_next_ref, w_ref, xm_ref, gate_ref, l2g_ref, l2b_ref,
                    ys_hbm, o_ref, buf_ref, sem):
    i = pl.program_id(0)
    tt = xm_ref.shape[0]
    cur = lax.rem(i, 2)

    def issue(sref, sl):
        def row_copy(r, kk, a):
            return pltpu.make_async_copy(ys_hbm.at[_row_tile(sref[a])], buf_ref.at[sl, kk, _row_tile(r)],
                                         sem.at[sl])
        _row_copy_groups(tt, row_copy)

    @pl.when(i == 0)
    def _():
        issue(slot_ref, 0)

    @pl.when(i + 1 < pl.num_programs(0))
    def _():
        issue(slot_next_ref, 1 - cur)

    for kk in range(TOP_K):
        pltpu.make_async_copy(ys_hbm.at[pl.ds(0, tt * ROW_TILES)], buf_ref.at[cur, kk], sem.at[cur]).wait()
    w = w_ref[...]
    cols = []
    for j in range(ROW_TILES):
        acc = w[:, 0:1] * buf_ref[cur, 0, _row_slab(tt, j), :]
        for kk in range(1, TOP_K):
            acc = acc + w[:, kk:kk + 1] * buf_ref[cur, kk, _row_slab(tt, j), :]
        cols.append(acc)
    pre = DEEPNORM_ALPHA * xm_ref[...] + gate_ref[0] * jnp.concatenate(cols, axis=1)
    o_ref[...] = _layer_norm(pre) * l2g_ref[...] + l2b_ref[...]


def _combine(ys, slot_flat, sel_w, x_mid, gate2, ln2_g, ln2_b, tiles_per_batch):
    t, d = x_mid.shape
    tt = min(COMBINE_TILE, t)
    n = t // tt
    return pl.pallas_call(
        _combine_kernel,
        out_shape=jax.ShapeDtypeStruct((t, d), F32),
        grid=(n,),
        in_specs=[
            pl.BlockSpec((tt * TOP_K,), lambda i: (i,), memory_space=pltpu.SMEM),
            pl.BlockSpec((tt * TOP_K,), lambda i: (jnp.minimum(i + 1, n - 1),), memory_space=pltpu.SMEM),
            pl.BlockSpec((tt, LANES), lambda i: (i, 0)),
            pl.BlockSpec((tt, d), lambda i: (i, 0)),
            pl.BlockSpec((1, 1, d), lambda i: (i // tiles_per_batch, 0, 0)),
            pl.BlockSpec((1, d), lambda i: (0, 0)),
            pl.BlockSpec((1, d), lambda i: (0, 0)),
            pl.BlockSpec(memory_space=pl.ANY),
        ],
        out_specs=pl.BlockSpec((tt, d), lambda i: (i, 0)),
        scratch_shapes=[pltpu.VMEM((2, TOP_K, tt * ROW_TILES, LANES), F32), pltpu.SemaphoreType.DMA((2,))],
        compiler_params=_cparams(("arbitrary",)),
        name="moe_combine",
    )(slot_flat, slot_flat, sel_w, x_mid, gate2, ln2_g, ln2_b, ys)


def kernel(x, c, ctx, c_ctx, ada_w, ada_b, w_in, gla_wa_f, gla_ba_f, gla_wa_b, gla_ba_b, gla_norm_g,
           hy_conv_w, hy_conv_b, hy_flt_w1, hy_flt_b1, hy_flt_w2, hy_flt_b2, hy_flt_wout, hy_flt_freq,
           hy_bias_d, w_out, ln1_g, ln1_b, router_w, router_b, exp_w1, exp_b1, exp_w2, exp_b2, ln2_g, ln2_b):
    batch, seq_len, d = x.shape
    lyr = 0
    ch = HY_WIDTH

    n_rows = 8 * ((batch + 1 + 7) // 8)
    cvec = jnp.zeros((n_rows, d), F32).at[:batch].set(c).at[batch].set(c_ctx)
    mod = _modulation(cvec, ada_w[lyr], ada_b[lyr][None, :])
    part = lambda rows, i: rows[:, None, i * d:(i + 1) * d]
    mod_x = mod[:batch]
    mod_c = jnp.broadcast_to(mod[batch:batch + 1], (batch, 6 * d))

    w = w_in[lyr]
    o_q, o_k, o_v, o_g = 0, GLA_KEY, 2 * GLA_KEY, 2 * GLA_KEY + GLA_VAL
    o_a = o_g + GLA_VAL
    o_h = o_a + 2 * GLA_RANK
    a_cols = jnp.pad(w[:, o_a:o_h], ((0, 0), (0, A_PAD - 2 * GLA_RANK)))
    q_cols = w[:, o_q:o_k] * (GLA_DK ** -0.5)
    w_main = jnp.concatenate([q_cols, w[:, o_k:o_a], a_cols], axis=1).astype(BF16)
    w_ctx = jnp.concatenate([q_cols, w[:, o_k:o_g], a_cols], axis=1).astype(BF16)
    w_hy_t = w[:, o_h:].T.astype(BF16)
    conv_w = jnp.broadcast_to(hy_conv_w[lyr][:, :, None], (HY_CONV, 3 * ch, LANES))
    conv_b = jnp.broadcast_to(hy_conv_b[lyr][:, None], (3 * ch, LANES))

    ctx_qkva = _input_projection(ctx, part(mod_c, 0), part(mod_c, 1), w_ctx)
    q, k, v, g, a_low, u_t = _input_projection(x, part(mod_x, 0), part(mod_x, 1), w_main, w_hy_t, conv_w, conv_b)

    wa = jnp.zeros((2, A_PAD, GLA_KEY), F32)
    wa = wa.at[0, :GLA_RANK].set(gla_wa_f[lyr]).at[1, GLA_RANK:2 * GLA_RANK].set(gla_wa_b[lyr]).astype(BF16)
    ba = jnp.stack([gla_ba_f[lyr], gla_ba_b[lyr]])[:, None, :]
    o_f, o_b = _gla_scan(ctx_qkva, (q, k, v, a_low), wa, ba)

    r1 = 2 * seq_len // LANES
    consts = _dft_constants(r1)
    zz, tn_rows, rate = _filter_inputs(seq_len)
    w1_t = jnp.pad(hy_flt_w1[lyr], ((0, HY_EMB_PAD - HY_EMB), (0, 0))).T
    unit_tile = lambda vec: jnp.broadcast_to(vec[:, None], (HY_FH, LANES))
    wo_t = hy_flt_wout[lyr].reshape(HY_FH, HY_ORDER, 2, ch).transpose(1, 2, 3, 0)
    filt = _filter_mlp(zz, tn_rows, rate, w1_t, unit_tile(hy_flt_b1[lyr]), hy_flt_w2[lyr].T,
                       unit_tile(hy_flt_b2[lyr]), unit_tile(hy_flt_freq[lyr]), wo_t)
    spectra = _filter_spectra(filt.reshape(HY_ORDER, ch, r1, LANES), consts, r1)
    y_hy = _hyena_conv(u_t.reshape(batch // 2, 2, 3 * ch, r1 // 2, LANES), spectra, hy_bias_d[lyr], consts, r1)
    y_hy_t = y_hy.reshape(batch, ch, seq_len)

    wo = w_out[lyr].astype(BF16)
    wr = jnp.pad(router_w[lyr], ((0, 0), (0, LANES - N_EXPERTS))).astype(BF16)
    rb = jnp.pad(router_b[lyr], (0, LANES - N_EXPERTS), constant_values=NEG_BIG)[None, :]
    norm_g = jnp.tile(gla_norm_g[lyr], GLA_HEADS)[None, :]
    x_mid, h2, sel_e, sel_r, sel_w, cnt = _output_projection(
        o_f, o_b, g, y_hy_t, x, part(mod_x, 2), part(mod_x, 3), part(mod_x, 4),
        wo[:GLA_VAL], wo[GLA_VAL:], norm_g, ln1_g[lyr][None], ln1_b[lyr][None], wr, rb)

    t = batch * seq_len
    counts = cnt[0, :N_EXPERTS]
    tiles_e = (counts + (MOE_TILE - 1)) // MOE_TILE
    tile_end = jnp.cumsum(tiles_e)
    base = ((tile_end - tiles_e) * MOE_TILE).astype(jnp.int32)
    n_tiles = tile_end[-1:].astype(jnp.int32)
    max_tiles = t * TOP_K // MOE_TILE + N_EXPERTS
    tile_ids = jnp.minimum(jnp.arange(max_tiles, dtype=jnp.int32), n_tiles[0] - 1)
    tile_expert = jnp.sum(tile_ids[:, None] >= tile_end[None, :], axis=1).astype(jnp.int32)
    e_flat = sel_e.reshape(t, LANES)[:, :TOP_K].reshape(t * TOP_K)
    r_flat = sel_r.reshape(t, LANES)[:, :TOP_K].reshape(t * TOP_K)
    slot_flat = base[e_flat] + r_flat

    n_blk = 2 * D_EXPERT // SWIGLU_BLOCK
    b1p_e = exp_b1[lyr].reshape(N_EXPERTS, n_blk, LANES, 2).transpose(0, 1, 3, 2).reshape(N_EXPERTS, 1, 2 * D_EXPERT)
    xs, w1p_e, w2_e = _dispatch(h2.reshape(t * ROW_TILES, LANES), slot_flat, base, counts, n_tiles,
                                max_tiles * MOE_TILE, exp_w1[lyr], exp_w2[lyr])
    ys = _expert_ffn(xs, tile_expert, n_tiles, w1p_e, b1p_e, w2_e, exp_b2[lyr][:, None, :])
    out = _combine(ys, slot_flat, sel_w.reshape(t, LANES), x_mid.reshape(t, d), part(mod_x, 5),
                   ln2_g[lyr][None], ln2_b[lyr][None], seq_len // min(COMBINE_TILE, t))
    return out.reshape(batch, seq_len, d)
```

```python
import functools
import math

import numpy as np
import jax
import jax.numpy as jnp
from jax import lax
from jax.experimental import pallas as pl
from jax.experimental.pallas import tpu as pltpu

F32 = jnp.float32
BF16 = jnp.bfloat16

D_MODEL = 1024
DEPTH = 1
GRID_W = 64
LN_EPS = 1e-6
DEEPNORM_ALPHA = (2 * DEPTH) ** 0.25
GLA_HEADS = 4
GLA_DK = 64
GLA_DV = 128
GLA_KEY = GLA_HEADS * GLA_DK
GLA_VAL = GLA_HEADS * GLA_DV
GLA_RANK = 16
GLA_TAU = 16.0
HY_WIDTH = D_MODEL - GLA_VAL
HY_ORDER = 2
HY_CONV = 3
HY_EMB = 33
HY_FH = 64
HY_TARGET = 1e-2
HY_FAST = 0.3
HY_SLOW = 1.5
N_EXPERTS = 32
TOP_K = 4
D_EXPERT = D_MODEL
SWIGLU_ALPHA = 1.702
SWIGLU_LIMIT = 7.0

LANES = 128
VMEM_LIMIT = 56 * 1024 * 1024

TOK_TILE = 512
PROJ_SUB_TILE = 256
GLA_CHUNK = 128
GLA_BLOCK = 256
GLA_BATCH = 4
GLA_SAFE_EXPONENT = 80.0
A_PAD = LANES
HY_GROUP = 8
HY_CH_BLOCK = 16
FILT_TILE = 1024
HY_EMB_PAD = 40
MOE_TILE = 512
FFN_TILES_PER_STEP = 2
COMBINE_CHUNK = 32
DMA_UNROLL = 8
ROW_TILES = D_MODEL // LANES


def _row_slab(n_rows, j):
    return pl.ds(j, n_rows, stride=ROW_TILES)


def _row_tile(r):
    return pl.ds(pl.multiple_of(r * ROW_TILES, ROW_TILES), ROW_TILES)
SWIGLU_BLOCK = 2 * LANES
NEG_BIG = -1e30


def _cparams(sem):
    return pltpu.CompilerParams(dimension_semantics=sem, vmem_limit_bytes=VMEM_LIMIT)


def _layer_norm(x):
    mu = jnp.mean(x, axis=-1, keepdims=True)
    xc = x - mu
    return xc * lax.rsqrt(jnp.mean(xc * xc, axis=-1, keepdims=True) + LN_EPS)


def _dot(a, b):
    return jnp.dot(a, b, preferred_element_type=F32)


def _dot_nt(a, b):
    return lax.dot_general(a, b, (((1,), (1,)), ((), ())), preferred_element_type=F32)


def _dot_tn(a, b):
    return lax.dot_general(a, b, (((0,), (0,)), ((), ())), preferred_element_type=F32)


def _mod_kernel(c_ref, w_ref, b_ref, o_ref):
    c = c_ref[...]
    s = c * jax.nn.sigmoid(c)
    o_ref[...] = _dot(s.astype(BF16), w_ref[...].astype(BF16)) + b_ref[...]


def _modulation(cvec, ada_w, ada_b):
    rows, d = cvec.shape
    n = ada_w.shape[1]
    tn = 1024
    return pl.pallas_call(
        _mod_kernel,
        out_shape=jax.ShapeDtypeStruct((rows, n), F32),
        grid=(n // tn,),
        in_specs=[
            pl.BlockSpec((rows, d), lambda j: (0, 0)),
            pl.BlockSpec((d, tn), lambda j: (0, j)),
            pl.BlockSpec((1, tn), lambda j: (0, j)),
        ],
        out_specs=pl.BlockSpec((rows, tn), lambda j: (0, j)),
        compiler_params=_cparams(("arbitrary",)),
        name="adaln_mod",
    )(cvec, ada_w, ada_b)


def _inproj_kernel(x_ref, sh_ref, sc_ref, wm_ref, *rest, with_hy):
    if with_hy:
        wh_ref, cw_ref, cb_ref, q_ref, k_ref, v_ref, g_ref, a_ref, u_ref = rest
    else:
        q_ref, k_ref, v_ref, a_ref = rest
    tt = x_ref.shape[1]
    n_sub = max(tt // PROJ_SUB_TILE, 1)
    sub = tt // n_sub
    for s in range(n_sub):
        rows = slice(s * sub, (s + 1) * sub)
        h = _layer_norm(x_ref[0, rows, :]) * (1.0 + sc_ref[0]) + sh_ref[0]
        hb = h.astype(BF16)
        u = _dot(hb, wm_ref[...])
        q_ref[0, rows, :] = u[:, :GLA_KEY].astype(BF16)
        k_ref[0, rows, :] = u[:, GLA_KEY:2 * GLA_KEY].astype(BF16)
        v_ref[0, rows, :] = u[:, 2 * GLA_KEY:2 * GLA_KEY + GLA_VAL].astype(BF16)
        off = 2 * GLA_KEY + GLA_VAL
        if with_hy:
            g_ref[0, rows, :] = u[:, off:off + GLA_VAL].astype(BF16)
            off += GLA_VAL
        a_ref[0, rows, :] = u[:, off:off + A_PAD]
        if with_hy:
            ut = _dot_nt(wh_ref[...], hb)
            lane = lax.broadcasted_iota(jnp.int32, (1, LANES), 1) % GRID_W
            not_first = (lane != 0).astype(F32)
            not_last = (lane != GRID_W - 1).astype(F32)
            for j in range(sub // LANES):
                c = ut[:, j * LANES:(j + 1) * LANES]
                left = pltpu.roll(c, 1, axis=1) * not_first
                right = pltpu.roll(c, LANES - 1, axis=1) * not_last
                y = cw_ref[0] * left + cw_ref[1] * c + cw_ref[2] * right + cb_ref[...]
                lanes = slice(s * sub + j * LANES, s * sub + (j + 1) * LANES)
                u_ref[0, :, lanes] = y.astype(BF16)


def _input_projection(x, shift, scale, w_main, w_hy_t=None, conv_w=None, conv_b=None):
    b, l, d = x.shape
    tt = min(TOK_TILE, l)
    with_hy = w_hy_t is not None
    nm = w_main.shape[1]
    tok = lambda width, dt: jax.ShapeDtypeStruct((b, l, width), dt)
    tok_spec = lambda width: pl.BlockSpec((1, tt, width), lambda i, j: (i, j, 0))
    in_specs = [
        pl.BlockSpec((1, tt, d), lambda i, j: (i, j, 0)),
        pl.BlockSpec((1, 1, d), lambda i, j: (i, 0, 0)),
        pl.BlockSpec((1, 1, d), lambda i, j: (i, 0, 0)),
        pl.BlockSpec((d, nm), lambda i, j: (0, 0)),
    ]
    args = [x, shift, scale, w_main]
    if with_hy:
        ch = w_hy_t.shape[0]
        in_specs += [
            pl.BlockSpec((ch, d), lambda i, j: (0, 0)),
            pl.BlockSpec((HY_CONV, ch, LANES), lambda i, j: (0, 0, 0)),
            pl.BlockSpec((ch, LANES), lambda i, j: (0, 0)),
        ]
        args += [w_hy_t, conv_w, conv_b]
        out_shape = [tok(GLA_KEY, BF16), tok(GLA_KEY, BF16), tok(GLA_VAL, BF16), tok(GLA_VAL, BF16),
                     tok(A_PAD, F32), jax.ShapeDtypeStruct((b, ch, l), BF16)]
        out_specs = [tok_spec(GLA_KEY), tok_spec(GLA_KEY), tok_spec(GLA_VAL), tok_spec(GLA_VAL),
                     tok_spec(A_PAD), pl.BlockSpec((1, ch, tt), lambda i, j: (i, 0, j))]
    else:
        out_shape = [tok(GLA_KEY, BF16), tok(GLA_KEY, BF16), tok(GLA_VAL, BF16), tok(A_PAD, F32)]
        out_specs = [tok_spec(GLA_KEY), tok_spec(GLA_KEY), tok_spec(GLA_VAL), tok_spec(A_PAD)]
    return pl.pallas_call(
        functools.partial(_inproj_kernel, with_hy=with_hy),
        out_shape=out_shape,
        grid=(b, l // tt),
        in_specs=in_specs,
        out_specs=out_specs,
        compiler_params=_cparams(("parallel", "arbitrary")),
        name="in_proj_hy" if with_hy else "in_proj_ctx",
    )(*args)


def _gla_kernel(*refs, n_ctx_blocks, chunk, n_batch, n_levels, safe_exponent):
    ins = [refs[0:8], refs[8:16]]
    wa_ref, ba_ref, cw_ref, cm_ref, hm_ref, bd_ref, dq_ref, dk_ref, lm_ref = refs[16:25]
    outs = refs[25:27]
    st_ref = refs[27]
    s = pl.program_id(1)
    blk = ins[0][4].shape[1]
    n_chunks = blk // chunk

    @pl.when(s == 0)
    def _():
        st_ref[...] = jnp.zeros_like(st_ref)

    def two_pass(m, g_hi, g_lo):
        return _dot(m, g_hi) + _dot(m, g_lo)

    is_ctx = s < n_ctx_blocks
    bdmask = bd_ref[...]
    chains = [(d, bi) for d in range(2) for bi in range(n_batch)]

    def pre_activation(d, bi):
        a = jnp.where(is_ctx, ins[d][3][bi], ins[d][7][bi])
        return _dot(a.astype(BF16), wa_ref[d]) + ba_ref[d]

    def advance(d, bi, z, single_reference):
        qc_ref, kc_ref, vc_ref, _, ql_ref, kl_ref, vl_ref, _ = ins[d]
        q = jnp.where(is_ctx, qc_ref[bi], ql_ref[bi]).astype(F32)
        k = jnp.where(is_ctx, kc_ref[bi], kl_ref[bi]).astype(F32)
        v = jnp.where(is_ctx, vc_ref[bi], vl_ref[bi])
        g = (jnp.minimum(z, 0.0) - jnp.log(1.0 + jnp.exp(-jnp.abs(z)))) * (1.0 / GLA_TAU)
        g_hi = g.astype(BF16)
        g_lo = (g - g_hi.astype(F32)).astype(BF16)
        cum = two_pass(cw_ref[d], g_hi, g_lo)
        edge = (lambda c: (c + 1) * chunk - 1) if d == 0 else (lambda c: c * chunk)
        tot = jnp.concatenate([jnp.broadcast_to(cum[edge(c):edge(c) + 1], (chunk, GLA_KEY))
                               for c in range(n_chunks)], axis=0)
        qs_all = (q * jnp.exp(cum)).astype(BF16)
        kst_all = (k * jnp.exp(tot - cum)).astype(BF16)
        dec_all = jnp.exp(tot)

        if single_reference:
            ks_all = (k * jnp.exp(-cum)).astype(BF16)
            cmask = cm_ref[d]
            rows_out = []
            for c in range(n_chunks):
                rows = slice(c * chunk, (c + 1) * chunk)
                parts = []
                for h in range(GLA_HEADS):
                    a_h = _dot_nt(qs_all[rows] * hm_ref[h], ks_all[rows]) * cmask
                    parts.append(_dot(a_h.astype(BF16), v[rows, h * GLA_DV:(h + 1) * GLA_DV]))
                rows_out.append(jnp.concatenate(parts, axis=1))
            o_intra = jnp.concatenate(rows_out, axis=0)
        else:
            def level(lv, acc):
                ql = (q * jnp.exp(two_pass(dq_ref[d, lv], g_hi, g_lo))).astype(BF16)
                kl = (k * jnp.exp(two_pass(dk_ref[d, lv], g_hi, g_lo))).astype(BF16)
                msk = lm_ref[d, lv]
                parts = []
                for h in range(GLA_HEADS):
                    a_h = _dot_nt(ql * hm_ref[h], kl) * msk
                    parts.append(_dot(a_h.astype(BF16), v[:, h * GLA_DV:(h + 1) * GLA_DV]))
                return acc + jnp.concatenate(parts, axis=1)
            o_intra = lax.fori_loop(0, n_levels, level, jnp.zeros((blk, GLA_VAL), F32))

        for n in range(n_chunks):
            ci = n if d == 0 else n_chunks - 1 - n
            rows = slice(ci * chunk, (ci + 1) * chunk)
            o = _dot_nt(qs_all[rows], st_ref[d, bi].astype(BF16)) + o_intra[rows]
            outs[d][bi, rows, :] = o.astype(outs[d].dtype)
            st_ref[d, bi] = (st_ref[d, bi] * dec_all[ci * chunk:ci * chunk + 1]
                             + _dot_tn(v[rows], kst_all[rows]) * bdmask)

    zs = [pre_activation(d, bi) for d, bi in chains]
    z_low = zs[0]
    for z in zs[1:]:
        z_low = jnp.minimum(z_low, z)
    bound = (jnp.maximum(-jnp.min(z_low), 0.0) + math.log(2.0)) * (chunk / GLA_TAU)

    def step(single_reference):
        def run():
            for (d, bi), z in zip(chains, zs):
                advance(d, bi, z, single_reference)
        return run

    lax.cond(bound < safe_exponent, step(True), step(False))


def _gla_level_tables(blk, chunk):
    sizes = []
    s = chunk // 2
    while s >= 1:
        sizes.append(s)
        s //= 2
    n_lv = len(sizes) + 1
    dq = np.zeros((2, n_lv, blk, blk), np.float32)
    dk = np.zeros((2, n_lv, blk, blk), np.float32)
    lm = np.zeros((2, n_lv, blk, blk), np.float32)
    idx = np.arange(blk)
    for lv, s in enumerate(sizes):
        blk_id = idx // (2 * s)
        right = (idx % (2 * s)) >= s
        b = blk_id * 2 * s + s
        same = blk_id[:, None] == blk_id[None, :]
        m = idx[None, :]
        dq[0, lv] = (right[:, None] & (m >= b[:, None]) & (m <= idx[:, None]))
        dk[0, lv] = (~right[:, None] & (m > idx[:, None]) & (m < b[:, None]))
        lm[0, lv] = same & right[:, None] & ~right[None, :]
        dq[1, lv] = (~right[:, None] & (m >= idx[:, None]) & (m < b[:, None]))
        dk[1, lv] = (right[:, None] & (m >= b[:, None]) & (m < idx[:, None]))
        lm[1, lv] = same & ~right[:, None] & right[None, :]
    lm[:, n_lv - 1] = np.eye(blk, dtype=np.float32)
    return dq, dk, lm, n_lv


def _gla_scan(ctx_qkva, lat_qkva, wa, ba, chunk=GLA_CHUNK, blk=GLA_BLOCK, safe_exponent=GLA_SAFE_EXPONENT):
    qc, kc, vc, ac = ctx_qkva
    ql, kl, vl, al = lat_qkva
    b, l, _ = ql.shape
    ctx_len = qc.shape[1]
    n_ctx = ctx_len // blk
    n_lat = l // blk
    c = chunk
    idx = np.arange(c)
    tri = np.stack([idx[:, None] >= idx[None, :], idx[:, None] <= idx[None, :]]).astype(np.float32)
    cw = np.stack([np.kron(np.eye(blk // c, dtype=np.float32), tri[dd]) for dd in range(2)])
    dq, dk, lm, n_levels = _gla_level_tables(blk, c)
    hmask = np.zeros((GLA_HEADS, 1, GLA_KEY), np.float32)
    for h in range(GLA_HEADS):
        hmask[h, 0, h * GLA_DK:(h + 1) * GLA_DK] = 1.0
    bd = (np.arange(GLA_VAL)[:, None] // GLA_DV == np.arange(GLA_KEY)[None, :] // GLA_DK).astype(np.float32)

    nb = GLA_BATCH if b % GLA_BATCH == 0 else 1

    def lat_block(d):
        def f(i, s):
            t = jnp.maximum(s - n_ctx, 0)
            return (i, t if d == 0 else n_lat - 1 - t, 0)
        return f

    def ctx_block(d):
        def f(i, s):
            t = jnp.minimum(s, n_ctx - 1)
            return (i, t if d == 0 else n_ctx - 1 - t, 0)
        return f

    widths = (GLA_KEY, GLA_KEY, GLA_VAL, A_PAD)
    dir_specs = lambda d: ([pl.BlockSpec((nb, blk, w), ctx_block(d)) for w in widths]
                           + [pl.BlockSpec((nb, blk, w), lat_block(d)) for w in widths])
    consts = [wa, ba, jnp.asarray(cw, BF16), jnp.asarray(tri, F32), jnp.asarray(hmask, BF16), jnp.asarray(bd, F32),
              jnp.asarray(dq, BF16), jnp.asarray(dk, BF16), jnp.asarray(lm, F32)]
    full = lambda arr: pl.BlockSpec(arr.shape, lambda i, s: (0,) * arr.ndim)
    o_sds = jax.ShapeDtypeStruct((b, l, GLA_VAL), BF16)
    dir_args = [qc, kc, vc, ac, ql, kl, vl, al]
    return pl.pallas_call(
        functools.partial(_gla_kernel, n_ctx_blocks=n_ctx, chunk=c, n_batch=nb, n_levels=n_levels,
                          safe_exponent=safe_exponent),
        out_shape=[o_sds, o_sds],
        grid=(b // nb, n_ctx + n_lat),
        in_specs=dir_specs(0) + dir_specs(1) + [full(arr) for arr in consts],
        out_specs=[pl.BlockSpec((nb, blk, GLA_VAL), lat_block(0)), pl.BlockSpec((nb, blk, GLA_VAL), lat_block(1))],
        scratch_shapes=[pltpu.VMEM((2, nb, GLA_VAL, GLA_KEY), F32)],
        compiler_params=_cparams(("parallel", "arbitrary")),
        name="gla_scan",
    )(*dir_args, *dir_args, *consts)


def _dft_constants(r1):
    n = r1 * LANES
    h = r1 // 2
    k1 = np.arange(r1)
    f1 = np.exp(-2j * np.pi * np.outer(k1, k1) / r1)
    f2 = np.exp(-2j * np.pi * np.outer(np.arange(LANES), np.arange(LANES)) / LANES)
    tw = np.exp(-2j * np.pi * np.outer(k1, np.arange(LANES)) / n)
    fa_c = np.block([[f1.real[:, :h], -f1.imag[:, :h]], [f1.imag[:, :h], f1.real[:, :h]]])
    fa_r = np.concatenate([f1.real, f1.imag], axis=0)
    gc = np.block([[f2.real, f2.imag], [-f2.imag, f2.real]])
    gci = np.block([[f2.real, -f2.imag], [f2.imag, f2.real]])
    fai = np.block([[f1.real[:h], f1.imag[:h]], [-f1.imag[:h], f1.real[:h]]]) / n
    tw_lane = np.tile(tw, (1, HY_GROUP))
    tw_row = np.tile(tw, (HY_GROUP, 1))
    f = lambda a: jnp.asarray(a, F32)
    return dict(fa_c=f(fa_c), fa_r=f(fa_r), gc=f(gc), gci=f(gci), fai=f(fai),
                twl_r=f(tw_lane.real), twl_i=f(tw_lane.imag), twr_r=f(tw_row.real), twr_i=f(tw_row.imag))


def _fwd_dft(rhs, fa, gc, twl_r, twl_i, r1):
    a = _dot(fa, rhs)
    ar, ai = a[:r1], a[r1:]
    br = (ar * twl_r - ai * twl_i).astype(BF16)
    bi = (ar * twl_i + ai * twl_r).astype(BF16)
    lhs = jnp.concatenate(
        [jnp.concatenate([br[:, c * LANES:(c + 1) * LANES], bi[:, c * LANES:(c + 1) * LANES]], axis=1)
         for c in range(HY_GROUP)], axis=0)
    return _dot(lhs, gc)


def _inv_dft(yr, yi, gci, fai, twr_r, twr_i, r1):
    lhs = jnp.concatenate([yr, yi], axis=1).astype(BF16)
    c = _dot(lhs, gci)
    cr, ci = c[:, :LANES], c[:, LANES:]
    dr = (cr * twr_r + ci * twr_i).astype(BF16)
    di = (ci * twr_r - cr * twr_i).astype(BF16)
    rhs = jnp.concatenate(
        [jnp.concatenate([dr[g * r1:(g + 1) * r1], di[g * r1:(g + 1) * r1]], axis=0)
         for g in range(HY_GROUP)], axis=1)
    return _dot(fai, rhs)


def _filter_mlp_kernel(zt_ref, tn_ref, rate_ref, w1_ref, b1_ref, w2_ref, b2_ref, fr_ref, wo_ref, o_ref):
    hp = lax.Precision.HIGHEST
    lt = tn_ref.shape[1]
    lanes = lambda ref: jnp.concatenate([ref[...]] * (lt // LANES), axis=1)
    fr = lanes(fr_ref)
    hid = jnp.sin(fr * (jnp.dot(w1_ref[...], zt_ref[...], precision=hp, preferred_element_type=F32) + lanes(b1_ref)))
    hid = jnp.sin(fr * (jnp.dot(w2_ref[...], hid, precision=hp, preferred_element_type=F32) + lanes(b2_ref)))
    window = jnp.exp(-tn_ref[0:1, :] * lanes(rate_ref)) * tn_ref[1:2, :]
    for o in range(HY_ORDER):
        o_ref[o] = jnp.dot(wo_ref[o, 0], hid, precision=hp, preferred_element_type=F32) * window


def _filter_mlp(z_t, tn_rows, rate, w1_t, b1, w2_t, b2, freq, wo_t):
    emb, n2l = z_t.shape
    l = n2l // 2
    lt = min(FILT_TILE, l)
    nt = l // lt
    ch = rate.shape[0]
    full = lambda shape: pl.BlockSpec(shape, lambda d, j: (0,) * len(shape))
    return pl.pallas_call(
        _filter_mlp_kernel,
        out_shape=jax.ShapeDtypeStruct((HY_ORDER, ch, n2l), F32),
        grid=(2, nt),
        in_specs=[
            pl.BlockSpec((emb, lt), lambda d, j: (0, d * nt + j)),
            pl.BlockSpec((8, lt), lambda d, j: (0, d * nt + j)),
            full((ch, LANES)),
            full((HY_FH, emb)), full((HY_FH, LANES)), full((HY_FH, HY_FH)), full((HY_FH, LANES)),
            full((HY_FH, LANES)),
            pl.BlockSpec((HY_ORDER, 1, ch, HY_FH), lambda d, j: (0, d, 0, 0)),
        ],
        out_specs=pl.BlockSpec((HY_ORDER, ch, lt), lambda d, j: (0, 0, d * nt + j)),
        compiler_params=_cparams(("arbitrary", "arbitrary")),
        name="hyena_filter_mlp",
    )(z_t, tn_rows, rate, w1_t, b1, w2_t, b2, freq, wo_t)


def _filter_fft_kernel(f_ref, fa_ref, gc_ref, twl_r_ref, twl_i_ref, h_ref, *, r1):
    fa = fa_ref[...].astype(BF16)
    gc = gc_ref[...].astype(BF16)
    nc = f_ref.shape[1]
    for g0 in range(0, nc, HY_GROUP):
        rhs = jnp.concatenate([f_ref[0, g0 + c].astype(BF16) for c in range(HY_GROUP)], axis=1)
        x = _fwd_dft(rhs, fa, gc, twl_r_ref[...], twl_i_ref[...], r1)
        for c in range(HY_GROUP):
            h_ref[0, g0 + c] = x[c * r1:(c + 1) * r1].astype(h_ref.dtype)


def _filter_spectra(filt, consts, r1):
    order, ch = filt.shape[:2]
    nc = HY_CH_BLOCK
    full = lambda a: pl.BlockSpec(a.shape, lambda o, j: (0,) * a.ndim)
    cs = [consts["fa_r"], consts["gc"], consts["twl_r"], consts["twl_i"]]
    return pl.pallas_call(
        functools.partial(_filter_fft_kernel, r1=r1),
        out_shape=jax.ShapeDtypeStruct((order, ch, r1, 2 * LANES), BF16),
        grid=(order, ch // nc),
        in_specs=[pl.BlockSpec((1, nc, r1, LANES), lambda o, j: (o, j, 0, 0))] + [full(a) for a in cs],
        out_specs=pl.BlockSpec((1, nc, r1, 2 * LANES), lambda o, j: (o, j, 0, 0)),
        compiler_params=_cparams(("arbitrary", "arbitrary")),
        name="hyena_filter_fft",
    )(filt, *cs)


def _hyena_kernel(dbias_ref, v_ref, x1_ref, x2_ref, h_ref, fa_ref, gc_ref, gci_ref, fai_ref,
                  twl_r_ref, twl_i_ref, twr_r_ref, twr_i_ref, y_ref, *, r1):
    fa = fa_ref[...].astype(BF16)
    gc = gc_ref[...].astype(BF16)
    gci = gci_ref[...].astype(BF16)
    fai = fai_ref[...].astype(BF16)
    twl_r, twl_i = twl_r_ref[...], twl_i_ref[...]
    twr_r, twr_i = twr_r_ref[...], twr_i_ref[...]
    nc = v_ref.shape[2]
    half = r1 // 2
    c_base = pl.program_id(0) * nc

    def conv(sig, sig_b, order, g0):
        rhs = jnp.concatenate(
            [jnp.concatenate([sig_b[c][0], sig_b[c][1]], axis=0) for c in range(HY_GROUP)], axis=1)
        x = _fwd_dft(rhs, fa, gc, twl_r, twl_i, r1)
        xr, xi = x[:, :LANES], x[:, LANES:]
        hh = jnp.concatenate([h_ref[order, g0 + c] for c in range(HY_GROUP)], axis=0).astype(F32)
        hr, hi = hh[:, :LANES], hh[:, LANES:]
        y = _inv_dft(xr * hr - xi * hi, xr * hi + xi * hr, gci, fai, twr_r, twr_i, r1)
        out = []
        for c in range(HY_GROUP):
            dcoef = dbias_ref[order, c_base + g0 + c]
            yc = y[:, c * LANES:(c + 1) * LANES]
            out.append([yc[:half] + dcoef * sig[c][0], yc[half:] + dcoef * sig[c][1]])
        return out

    for g0 in range(0, nc, HY_GROUP):
        v_b = [[v_ref[0, b, g0 + c] for b in range(2)] for c in range(HY_GROUP)]
        v = [[t.astype(F32) for t in pair] for pair in v_b]
        y1 = conv(v, v_b, 0, g0)
        z = [[x1_ref[0, b, g0 + c].astype(F32) * y1[c][b] for b in range(2)] for c in range(HY_GROUP)]
        z_b = [[t.astype(BF16) for t in pair] for pair in z]
        y2 = conv(z, z_b, 1, g0)
        for c in range(HY_GROUP):
            for b in range(2):
                y_ref[0, b, g0 + c] = (x2_ref[0, b, g0 + c].astype(F32) * y2[c][b]).astype(y_ref.dtype)


def _hyena_conv(u_t, spectra, d_bias, consts, r1):
    bp, _, ch3, half, _ = u_t.shape
    ch = ch3 // 3
    nc = HY_CH_BLOCK
    nblk = ch // nc
    names = ["fa_c", "gc", "gci", "fai", "twl_r", "twl_i", "twr_r", "twr_i"]
    cs = [consts[k] for k in names]
    full = lambda a: pl.BlockSpec(a.shape, lambda j, p: (0,) * a.ndim)
    part = lambda k: pl.BlockSpec((1, 2, nc, half, LANES), lambda j, p: (p, 0, k * nblk + j, 0, 0))
    return pl.pallas_call(
        functools.partial(_hyena_kernel, r1=r1),
        out_shape=jax.ShapeDtypeStruct((bp, 2, ch, half, LANES), BF16),
        grid=(nblk, bp),
        in_specs=[pl.BlockSpec(memory_space=pltpu.SMEM), part(0), part(1), part(2),
                  pl.BlockSpec((HY_ORDER, nc, r1, 2 * LANES), lambda j, p: (0, j, 0, 0))] + [full(a) for a in cs],
        out_specs=pl.BlockSpec((1, 2, nc, half, LANES), lambda j, p: (p, 0, j, 0, 0)),
        compiler_params=_cparams(("arbitrary", "arbitrary")),
        name="hyena_conv",
    )(d_bias, u_t, u_t, u_t, spectra, *cs)


def _filter_inputs(l):
    n = jnp.arange(2 * l, dtype=jnp.int32)
    t = jnp.where(n < l, n, 2 * l - n).astype(F32)
    valid = (n != l).astype(F32)
    t_norm = t / (l - 1)
    bands = (HY_EMB - 1) // 2
    f = jnp.linspace(1e-4, bands - 1, bands, dtype=F32)
    ang = (2.0 * math.pi * t / l)[:, None] * f[None, :]
    z = jnp.concatenate([t_norm[:, None], jnp.cos(ang), -jnp.sin(ang)], -1)
    zz = jnp.pad(z, ((0, 0), (0, HY_EMB_PAD - HY_EMB))).T
    tn_rows = jnp.zeros((8, 2 * l), F32).at[0].set(t_norm).at[1].set(valid)
    deltas = jnp.linspace(math.log(HY_TARGET) / HY_SLOW, math.log(HY_TARGET) / HY_FAST, HY_WIDTH, dtype=F32)
    rate = jnp.broadcast_to(jnp.abs(deltas)[:, None], (HY_WIDTH, LANES))
    return zz, tn_rows, rate


def _outproj_kernel(of_ref, ob_ref, g_ref, yh_ref, x_ref, gate_ref, sh_ref, sc_ref,
                    wg_ref, wh_ref, ng_ref, l1g_ref, l1b_ref, wr_ref, rb_ref, ltri_ref,
                    xm_ref, h2_ref, se_ref, sr_ref, sw_ref, cnt_ref, cs_ref, carry_ref):
    @pl.when((pl.program_id(0) == 0) & (pl.program_id(1) == 0))
    def _():
        carry_ref[...] = jnp.zeros_like(carry_ref)

    cs_ref[0] = jnp.broadcast_to(carry_ref[0:1, :], cs_ref.shape[1:]).astype(jnp.int32)

    tt = x_ref.shape[1]
    sub = ltri_ref.shape[0]
    for s in range(tt // sub):
        rows = slice(s * sub, (s + 1) * sub)
        o = of_ref[0, rows, :].astype(F32) + ob_ref[0, rows, :].astype(F32)
        g = g_ref[0, rows, :].astype(F32)
        parts = []
        for h in range(GLA_HEADS):
            oh = o[:, h * GLA_DV:(h + 1) * GLA_DV]
            parts.append(oh * lax.rsqrt(jnp.mean(oh * oh, axis=-1, keepdims=True) + LN_EPS))
        y_gla = jnp.concatenate(parts, axis=1) * ng_ref[...] * (g * jax.nn.sigmoid(g))
        f = _dot(y_gla.astype(BF16), wg_ref[...]) + _dot_tn(yh_ref[0, :, rows], wh_ref[...])
        x_mid = (_layer_norm(DEEPNORM_ALPHA * x_ref[0, rows, :] + gate_ref[0] * f) * l1g_ref[...]
                 + l1b_ref[...])
        xm_ref[0, rows, :] = x_mid
        h2 = _layer_norm(x_mid) * (1.0 + sc_ref[0]) + sh_ref[0]
        for j in range(ROW_TILES):
            h2_ref[0, pl.ds(s * sub * ROW_TILES + j, sub, stride=ROW_TILES), :] = h2[:, j * LANES:(j + 1) * LANES]

        logits = _dot(h2.astype(BF16), wr_ref[...]) + rb_ref[...]
        lane = lax.broadcasted_iota(jnp.int32, logits.shape, 1).astype(F32)
        hits, idxs, exps = [], [], []
        m0 = None
        for _ in range(TOP_K):
            m = jnp.max(logits, axis=-1, keepdims=True)
            idx = jnp.min(jnp.where(logits == m, lane, float(LANES)), axis=-1, keepdims=True)
            hit = lane == idx
            m0 = m if m0 is None else m0
            hits.append(hit)
            idxs.append(idx)
            exps.append(jnp.exp(m - m0))
            logits = jnp.where(hit, NEG_BIG, logits)
        denom = exps[0]
        sel = jnp.where(hits[0], 1.0, 0.0)
        for kk in range(1, TOP_K):
            denom = denom + exps[kk]
            sel = sel + jnp.where(hits[kk], 1.0, 0.0)
        rank_all = _dot(ltri_ref[...], sel.astype(BF16)) + carry_ref[0:1, :]
        carry_ref[0:1, :] = carry_ref[0:1, :] + jnp.sum(sel, axis=0, keepdims=True)
        se = jnp.zeros(logits.shape, F32)
        sr = jnp.zeros(logits.shape, F32)
        sw = jnp.zeros(logits.shape, F32)
        for kk in range(TOP_K):
            rk = jnp.sum(jnp.where(hits[kk], rank_all, 0.0), axis=-1, keepdims=True)
            col = lane == float(kk)
            se = jnp.where(col, idxs[kk], se)
            sr = jnp.where(col, rk, sr)
            sw = jnp.where(col, exps[kk] / denom, sw)
        se_ref[0, rows, :] = se.astype(jnp.int32)
        sr_ref[0, rows, :] = sr.astype(jnp.int32)
        sw_ref[0, rows, :] = sw
    cnt_ref[...] = jnp.broadcast_to(carry_ref[0:1, :], cnt_ref.shape).astype(jnp.int32)


def _output_projection(o_f, o_b, g, y_hy_t, x, gate1, shift2, scale2, w_gla, w_hy, norm_g, ln1_g, ln1_b, wr, rb):
    b, l, d = x.shape
    tt = TOK_TILE
    tok = lambda w: pl.BlockSpec((1, tt, w), lambda i, j: (i, j, 0))
    row = lambda: pl.BlockSpec((1, 1, d), lambda i, j: (i, 0, 0))
    full = lambda a: pl.BlockSpec(a.shape, lambda i, j: (0,) * a.ndim)
    sub = tt
    ltri = jnp.asarray(np.tril(np.ones((sub, sub), np.float32), -1), BF16)
    consts = [w_gla, w_hy, norm_g, ln1_g, ln1_b, wr, rb, ltri]
    lane_i = jax.ShapeDtypeStruct((b, l, LANES), jnp.int32)
    return pl.pallas_call(
        _outproj_kernel,
        out_shape=[jax.ShapeDtypeStruct((b, l, d), F32), jax.ShapeDtypeStruct((b, l * ROW_TILES, LANES), F32),
                   lane_i, lane_i, jax.ShapeDtypeStruct((b, l, LANES), F32),
                   jax.ShapeDtypeStruct((8, LANES), jnp.int32),
                   jax.ShapeDtypeStruct((b * (l // tt), 8, LANES), jnp.int32)],
        grid=(b, l // tt),
        in_specs=[
            tok(GLA_VAL), tok(GLA_VAL), tok(GLA_VAL),
            pl.BlockSpec((1, HY_WIDTH, tt), lambda i, j: (i, 0, j)),
            tok(d), row(), row(), row(),
        ] + [full(a) for a in consts],
        out_specs=[tok(d), pl.BlockSpec((1, tt * ROW_TILES, LANES), lambda i, j: (i, j, 0)),
                   tok(LANES), tok(LANES), tok(LANES),
                   pl.BlockSpec((8, LANES), lambda i, j: (0, 0)),
                   pl.BlockSpec((1, 8, LANES), lambda i, j: (i * (l // tt) + j, 0, 0))],
        scratch_shapes=[pltpu.VMEM((8, LANES), F32)],
        compiler_params=_cparams(("arbitrary", "arbitrary")),
        name="out_proj_router",
    )(o_f, o_b, g, y_hy_t, x, gate1, shift2, scale2, *consts)


def _expert_weight_layout(w1_ref, w2_ref, p_ref, w1o_ref, w2o_ref):
    p = p_ref[...]
    for j in range(w1_ref.shape[2] // SWIGLU_BLOCK):
        cols = slice(j * SWIGLU_BLOCK, (j + 1) * SWIGLU_BLOCK)
        w1o_ref[0, :, cols] = _dot(w1_ref[0, :, cols].astype(BF16), p).astype(BF16)
    w2o_ref[0] = w2_ref[0].astype(BF16)


def _row_copy_groups(n_tokens, make_copy):
    def group(gidx, carry):
        for u in range(DMA_UNROLL):
            r = gidx * DMA_UNROLL + u
            for kk in range(TOP_K):
                make_copy(r, kk, r * TOP_K + kk).start(priority=kk % 2)
        return carry
    lax.fori_loop(0, n_tokens // DMA_UNROLL, group, 0)


def _dispatch_kernel(base_ref, cnt_ref, nt_ref, slot_ref, h2_ref, w1_ref, w2_ref, p_ref,
                     xs_hbm, w1o_ref, w2o_ref, zrow_ref, zblk_ref, sem, zsem):
    tt = h2_ref.shape[0] // ROW_TILES

    def row_copy(r, kk, a):
        return pltpu.make_async_copy(h2_ref.at[_row_tile(r)], xs_hbm.at[_row_tile(slot_ref[a])], sem)

    _row_copy_groups(tt, row_copy)
    _expert_weight_layout(w1_ref, w2_ref, p_ref, w1o_ref, w2o_ref)
    for _ in range(TOP_K):
        pltpu.make_async_copy(h2_ref, xs_hbm.at[pl.ds(0, tt * ROW_TILES)], sem).wait()

    @pl.when(pl.program_id(0) == pl.num_programs(0) - 1)
    def _():
        zrow_ref[...] = jnp.zeros_like(zrow_ref)

        def per_expert(e, carry):
            n = cnt_ref[e]
            end = ((n + (MOE_TILE - 1)) // MOE_TILE) * MOE_TILE

            def fill(r, c):
                pltpu.make_async_copy(zrow_ref, xs_hbm.at[_row_tile(base_ref[e] + r)], zsem).start()
                return c

            def drain(r, c):
                pltpu.make_async_copy(zrow_ref, xs_hbm.at[_row_tile(0)], zsem).wait()
                return c

            lax.fori_loop(n, end, fill, 0)
            lax.fori_loop(n, end, drain, 0)
            return carry

        lax.fori_loop(0, N_EXPERTS, per_expert, 0)

        zblk_ref[...] = jnp.zeros_like(zblk_ref)
        tile_rows = MOE_TILE * ROW_TILES
        n_all = xs_hbm.shape[0] // tile_rows

        def tile_copy(ti):
            row0 = pl.multiple_of(ti * tile_rows, tile_rows)
            return pltpu.make_async_copy(zblk_ref, xs_hbm.at[pl.ds(row0, tile_rows)], zsem)

        def fill_tile(ti, c):
            tile_copy(ti).start()
            return c

        def drain_tile(ti, c):
            tile_copy(ti).wait()
            return c

        lax.fori_loop(nt_ref[0], n_all, fill_tile, 0)
        lax.fori_loop(nt_ref[0], n_all, drain_tile, 0)


def _dispatch(h2, slot_flat, base, counts, n_tiles, n_slots, w1, w2):
    t = h2.shape[0] // ROW_TILES
    ne, d, f2 = w1.shape
    assert t % (ne * DMA_UNROLL) == 0, "one dispatch step per expert"
    tt = t // ne
    src = np.concatenate([np.arange(0, SWIGLU_BLOCK, 2), np.arange(1, SWIGLU_BLOCK, 2)])
    perm = np.zeros((SWIGLU_BLOCK, SWIGLU_BLOCK), np.float32)
    perm[src, np.arange(SWIGLU_BLOCK)] = 1.0
    exp_blk = lambda shape: pl.BlockSpec((1,) + shape, lambda i, *_: (i, 0, 0))
    return pl.pallas_call(
        _dispatch_kernel,
        out_shape=[jax.ShapeDtypeStruct((n_slots * ROW_TILES, LANES), F32),
                   jax.ShapeDtypeStruct(w1.shape, BF16), jax.ShapeDtypeStruct(w2.shape, BF16)],
        grid_spec=pltpu.PrefetchScalarGridSpec(
            num_scalar_prefetch=3,
            grid=(ne,),
            in_specs=[pl.BlockSpec((tt * TOP_K,), lambda i, *_: (i,), memory_space=pltpu.SMEM),
                      pl.BlockSpec((tt * ROW_TILES, LANES), lambda i, *_: (i, 0)),
                      exp_blk((d, f2)), exp_blk(w2.shape[1:]),
                      pl.BlockSpec((SWIGLU_BLOCK, SWIGLU_BLOCK), lambda i, *_: (0, 0))],
            out_specs=[pl.BlockSpec(memory_space=pl.ANY), exp_blk((d, f2)), exp_blk(w2.shape[1:])],
            scratch_shapes=[pltpu.VMEM((ROW_TILES, LANES), F32), pltpu.VMEM((MOE_TILE * ROW_TILES, LANES), F32),
                            pltpu.SemaphoreType.DMA, pltpu.SemaphoreType.DMA],
        ),
        compiler_params=_cparams(("arbitrary",)),
        name="moe_dispatch",
    )(base, counts, n_tiles, slot_flat, h2, w1, w2, jnp.asarray(perm, BF16))


def _ffn_kernel(te_ref, nt_ref, xs_ref, *refs):
    ys_ref = refs[-1]
    tm = MOE_TILE
    rows = tm * ROW_TILES
    n_here = jnp.clip(nt_ref[0] - pl.program_id(0) * FFN_TILES_PER_STEP, 0, FFN_TILES_PER_STEP)

    def tile(t):
        w1_ref, b1_ref, w2_ref, b2_ref = refs[4 * t:4 * t + 4]
        x = jnp.concatenate([xs_ref[pl.ds(t * rows + j, tm, stride=ROW_TILES), :] for j in range(ROW_TILES)], axis=1)
        hid = _dot(x.astype(BF16), w1_ref[0]) + b1_ref[0]
        acts = []
        for j in range(hid.shape[1] // SWIGLU_BLOCK):
            glu = jnp.minimum(hid[:, j * SWIGLU_BLOCK:j * SWIGLU_BLOCK + LANES], SWIGLU_LIMIT)
            lin = jnp.clip(hid[:, j * SWIGLU_BLOCK + LANES:(j + 1) * SWIGLU_BLOCK], -SWIGLU_LIMIT, SWIGLU_LIMIT)
            acts.append((glu * jax.nn.sigmoid(SWIGLU_ALPHA * glu) * (lin + 1.0)).astype(BF16))
        y = _dot(jnp.concatenate(acts, axis=1), w2_ref[0]) + b2_ref[0]
        for j in range(ROW_TILES):
            ys_ref[pl.ds(t * rows + j, tm, stride=ROW_TILES), :] = y[:, j * LANES:(j + 1) * LANES]

    for n_valid in range(FFN_TILES_PER_STEP + 1):
        @pl.when(n_here == n_valid)
        def _(n_valid=n_valid):
            for t in range(n_valid):
                tile(t)
            for t in range(n_valid, FFN_TILES_PER_STEP):
                ys_ref[t * rows:(t + 1) * rows, :] = jnp.zeros((rows, LANES), F32)


def _expert_ffn(xs, tile_expert, n_tiles, w1p, b1p, w2b, b2):
    n_slots = xs.shape[0] // ROW_TILES
    d = w1p.shape[1]
    tps = FFN_TILES_PER_STEP
    tm = MOE_TILE
    f2 = w1p.shape[2]
    assert n_slots % (tm * tps) == 0
    rows_blk = (tps * tm * ROW_TILES, LANES)
    weights = []
    for t in range(tps):
        exp_blk = lambda i, te, nt, t=t: (te[i * tps + t], 0, 0)
        weights += [pl.BlockSpec((1, d, f2), exp_blk), pl.BlockSpec((1, 1, f2), exp_blk),
                    pl.BlockSpec((1, f2 // 2, d), exp_blk), pl.BlockSpec((1, 1, d), exp_blk)]
    return pl.pallas_call(
        _ffn_kernel,
        out_shape=jax.ShapeDtypeStruct(xs.shape, F32),
        grid_spec=pltpu.PrefetchScalarGridSpec(
            num_scalar_prefetch=2,
            grid=(n_slots // (tm * tps),),
            in_specs=[
                pl.BlockSpec(rows_blk, lambda i, te, nt: (jnp.minimum(i, (nt[0] - 1) // tps), 0)),
            ] + weights,
            out_specs=pl.BlockSpec(rows_blk, lambda i, te, nt: (i, 0)),
        ),
        compiler_params=_cparams(("arbitrary",)),
        name="moe_expert_ffn",
    )(tile_expert, n_tiles, xs, *([w1p, b1p, w2b, b2] * tps))


def _combine_kernel(cs_ref, base_ref, w_ref, ev_ref, rv_ref, xm_ref, gate_ref, l2g_ref, l2b_ref,
                    ys_hbm, o_ref, buf_ref, acc_ref, rowv_ref, rows_ref, seg_ref, nch_ref, sem, rsem):
    i = pl.program_id(0)
    tt = xm_ref.shape[0]
    cur = lax.rem(i, 2)
    chunk_rows = COMBINE_CHUNK * ROW_TILES

    def chunk_copy(src_row, dst_row, sl):
        return pltpu.make_async_copy(ys_hbm.at[pl.ds(pl.multiple_of(src_row * ROW_TILES, ROW_TILES), chunk_rows)],
                                     buf_ref.at[sl, pl.ds(pl.multiple_of(dst_row * ROW_TILES, ROW_TILES), chunk_rows)],
                                     sem.at[sl])

    def issue(tile, sl):
        def per_expert(e, off):
            start = cs_ref[tile, e]
            n_chunks = (cs_ref[tile + 1, e] - start + (COMBINE_CHUNK - 1)) // COMBINE_CHUNK
            seg_ref[sl, e] = off - start

            def one(c, carry):
                chunk_copy(base_ref[e] + start + c * COMBINE_CHUNK, off + c * COMBINE_CHUNK, sl).start()
                return carry

            lax.fori_loop(0, n_chunks, one, 0)
            return off + n_chunks * COMBINE_CHUNK

        total = lax.fori_loop(0, N_EXPERTS, per_expert, 0)
        nch_ref[sl] = total // COMBINE_CHUNK

    @pl.when(i == 0)
    def _():
        issue(0, 0)

    @pl.when(i + 1 < pl.num_programs(0))
    def _():
        issue(i + 1, 1 - cur)

    def drain(c, carry):
        chunk_copy(0, 0, cur).wait()
        return carry

    ev = ev_ref[...]
    rows = rv_ref[...]
    for e in range(N_EXPERTS):
        rows = rows + jnp.where(ev == e, seg_ref[cur, e], 0)
    rowv_ref[...] = rows
    to_smem = pltpu.make_async_copy(rowv_ref, rows_ref, rsem)
    to_smem.start()
    lax.fori_loop(0, nch_ref[cur], drain, 0)
    to_smem.wait()

    per_line = LANES // TOP_K

    def line(li, carry):
        for u in range(per_line):
            acc = None
            for kk in range(TOP_K):
                a = u * TOP_K + kk
                term = w_ref[li * LANES + a] * buf_ref[cur, _row_tile(rows_ref[li, a]), :]
                acc = term if acc is None else acc + term
            acc_ref[_row_tile(li * per_line + u), :] = acc
        return carry

    lax.fori_loop(0, tt // per_line, line, 0)
    mixed = jnp.concatenate([acc_ref[_row_slab(tt, j), :] for j in range(ROW_TILES)], axis=1)
    pre = DEEPNORM_ALPHA * xm_ref[...] + gate_ref[0] * mixed
    o_ref[...] = _layer_norm(pre) * l2g_ref[...] + l2b_ref[...]


def _combine(ys, tile_starts, base, e_flat, r_flat, w_flat, x_mid, gate2, ln2_g, ln2_b, tiles_per_batch):
    t, d = x_mid.shape
    tt = TOK_TILE
    n = t // tt
    cap = tt * TOP_K + N_EXPERTS * COMBINE_CHUNK
    lines = tt * TOP_K // LANES
    line_blk = lambda: pl.BlockSpec((lines, LANES), lambda i, *_: (i, 0))
    as_lines = lambda flat: flat.reshape(t * TOP_K // LANES, LANES)
    return pl.pallas_call(
        _combine_kernel,
        out_shape=jax.ShapeDtypeStruct((t, d), F32),
        grid_spec=pltpu.PrefetchScalarGridSpec(
            num_scalar_prefetch=2,
            grid=(n,),
            in_specs=[
                pl.BlockSpec((tt * TOP_K,), lambda i, *_: (i,), memory_space=pltpu.SMEM),
                line_blk(), line_blk(),
                pl.BlockSpec((tt, d), lambda i, *_: (i, 0)),
                pl.BlockSpec((1, 1, d), lambda i, *_: (i // tiles_per_batch, 0, 0)),
                pl.BlockSpec((1, d), lambda i, *_: (0, 0)),
                pl.BlockSpec((1, d), lambda i, *_: (0, 0)),
                pl.BlockSpec(memory_space=pl.ANY),
            ],
            out_specs=pl.BlockSpec((tt, d), lambda i, *_: (i, 0)),
            scratch_shapes=[pltpu.VMEM((2, cap * ROW_TILES, LANES), F32), pltpu.VMEM((tt * ROW_TILES, LANES), F32),
                            pltpu.VMEM((lines, LANES), jnp.int32), pltpu.SMEM((lines, LANES), jnp.int32),
                            pltpu.SMEM((2, N_EXPERTS), jnp.int32), pltpu.SMEM((2,), jnp.int32),
                            pltpu.SemaphoreType.DMA((2,)), pltpu.SemaphoreType.DMA],
        ),
        compiler_params=_cparams(("arbitrary",)),
        name="moe_combine",
    )(tile_starts, base, w_flat, as_lines(e_flat), as_lines(r_flat), x_mid, gate2, ln2_g, ln2_b, ys)


def kernel(x, c, ctx, c_ctx, ada_w, ada_b, w_in, gla_wa_f, gla_ba_f, gla_wa_b, gla_ba_b, gla_norm_g,
           hy_conv_w, hy_conv_b, hy_flt_w1, hy_flt_b1, hy_flt_w2, hy_flt_b2, hy_flt_wout, hy_flt_freq,
           hy_bias_d, w_out, ln1_g, ln1_b, router_w, router_b, exp_w1, exp_b1, exp_w2, exp_b2, ln2_g, ln2_b):
    batch, seq_len, d = x.shape
    lyr = 0
    ch = HY_WIDTH

    n_rows = 8 * ((batch + 1 + 7) // 8)
    cvec = jnp.zeros((n_rows, d), F32).at[:batch].set(c).at[batch].set(c_ctx)
    mod = _modulation(cvec, ada_w[lyr], ada_b[lyr][None, :])
    part = lambda rows, i: rows[:, None, i * d:(i + 1) * d]
    mod_x = mod[:batch]
    mod_c = jnp.broadcast_to(mod[batch:batch + 1], (batch, 6 * d))

    w = w_in[lyr]
    o_q, o_k, o_v, o_g = 0, GLA_KEY, 2 * GLA_KEY, 2 * GLA_KEY + GLA_VAL
    o_a = o_g + GLA_VAL
    o_h = o_a + 2 * GLA_RANK
    a_cols = jnp.pad(w[:, o_a:o_h], ((0, 0), (0, A_PAD - 2 * GLA_RANK)))
    q_cols = w[:, o_q:o_k] * (GLA_DK ** -0.5)
    w_main = jnp.concatenate([q_cols, w[:, o_k:o_a], a_cols], axis=1).astype(BF16)
    w_ctx = jnp.concatenate([q_cols, w[:, o_k:o_g], a_cols], axis=1).astype(BF16)
    w_hy_t = w[:, o_h:].T.astype(BF16)
    conv_w = jnp.broadcast_to(hy_conv_w[lyr][:, :, None], (HY_CONV, 3 * ch, LANES))
    conv_b = jnp.broadcast_to(hy_conv_b[lyr][:, None], (3 * ch, LANES))

    ctx_qkva = _input_projection(ctx, part(mod_c, 0), part(mod_c, 1), w_ctx)
    q, k, v, g, a_low, u_t = _input_projection(x, part(mod_x, 0), part(mod_x, 1), w_main, w_hy_t, conv_w, conv_b)

    wa = jnp.zeros((2, A_PAD, GLA_KEY), F32)
    wa = wa.at[0, :GLA_RANK].set(gla_wa_f[lyr]).at[1, GLA_RANK:2 * GLA_RANK].set(gla_wa_b[lyr]).astype(BF16)
    ba = jnp.stack([gla_ba_f[lyr], gla_ba_b[lyr]])[:, None, :]
    o_f, o_b = _gla_scan(ctx_qkva, (q, k, v, a_low), wa, ba)

    r1 = 2 * seq_len // LANES
    consts = _dft_constants(r1)
    zz, tn_rows, rate = _filter_inputs(seq_len)
    w1_t = jnp.pad(hy_flt_w1[lyr], ((0, HY_EMB_PAD - HY_EMB), (0, 0))).T
    unit_tile = lambda vec: jnp.broadcast_to(vec[:, None], (HY_FH, LANES))
    wo_t = hy_flt_wout[lyr].reshape(HY_FH, HY_ORDER, 2, ch).transpose(1, 2, 3, 0)
    filt = _filter_mlp(zz, tn_rows, rate, w1_t, unit_tile(hy_flt_b1[lyr]), hy_flt_w2[lyr].T,
                       unit_tile(hy_flt_b2[lyr]), unit_tile(hy_flt_freq[lyr]), wo_t)
    spectra = _filter_spectra(filt.reshape(HY_ORDER, ch, r1, LANES), consts, r1)
    y_hy = _hyena_conv(u_t.reshape(batch // 2, 2, 3 * ch, r1 // 2, LANES), spectra, hy_bias_d[lyr], consts, r1)
    y_hy_t = y_hy.reshape(batch, ch, seq_len)

    wo = w_out[lyr].astype(BF16)
    wr = jnp.pad(router_w[lyr], ((0, 0), (0, LANES - N_EXPERTS))).astype(BF16)
    rb = jnp.pad(router_b[lyr], (0, LANES - N_EXPERTS), constant_values=NEG_BIG)[None, :]
    norm_g = jnp.tile(gla_norm_g[lyr], GLA_HEADS)[None, :]
    x_mid, h2, sel_e, sel_r, sel_w, cnt, tile_cs = _output_projection(
        o_f, o_b, g, y_hy_t, x, part(mod_x, 2), part(mod_x, 3), part(mod_x, 4),
        wo[:GLA_VAL], wo[GLA_VAL:], norm_g, ln1_g[lyr][None], ln1_b[lyr][None], wr, rb)

    t = batch * seq_len
    counts = cnt[0, :N_EXPERTS]
    tiles_e = (counts + (MOE_TILE - 1)) // MOE_TILE
    tile_end = jnp.cumsum(tiles_e)
    base = ((tile_end - tiles_e) * MOE_TILE).astype(jnp.int32)
    n_tiles = tile_end[-1:].astype(jnp.int32)
    max_tiles = t * TOP_K // MOE_TILE + N_EXPERTS + FFN_TILES_PER_STEP
    tile_ids = jnp.minimum(jnp.arange(max_tiles, dtype=jnp.int32), n_tiles[0] - 1)
    tile_expert = jnp.sum(tile_ids[:, None] >= tile_end[None, :], axis=1).astype(jnp.int32)
    e_flat = sel_e.reshape(t, LANES)[:, :TOP_K].reshape(t * TOP_K)
    r_flat = sel_r.reshape(t, LANES)[:, :TOP_K].reshape(t * TOP_K)
    slot_flat = base[e_flat] + r_flat
    w_flat = sel_w.reshape(t, LANES)[:, :TOP_K].reshape(t * TOP_K)
    tile_starts = jnp.concatenate([tile_cs[:, 0, :], cnt[0:1]], axis=0)

    n_blk = 2 * D_EXPERT // SWIGLU_BLOCK
    b1p_e = exp_b1[lyr].reshape(N_EXPERTS, n_blk, LANES, 2).transpose(0, 1, 3, 2).reshape(N_EXPERTS, 1, 2 * D_EXPERT)
    xs, w1p_e, w2_e = _dispatch(h2.reshape(t * ROW_TILES, LANES), slot_flat, base, counts, n_tiles,
                                max_tiles * MOE_TILE, exp_w1[lyr], exp_w2[lyr])
    ys = _expert_ffn(xs, tile_expert, n_tiles, w1p_e, b1p_e, w2_e, exp_b2[lyr][:, None, :])
    out = _combine(ys, tile_starts, base, e_flat, r_flat, w_flat, x_mid.reshape(t, d), part(mod_x, 5),
                   ln2_g[lyr][None], ln2_b[lyr][None], seq_len // TOK_TILE)
    return out.reshape(batch, seq_len, d)
```

```python
import functools
import math

import numpy as np
import jax
import jax.numpy as jnp
from jax import lax
from jax.experimental import pallas as pl
from jax.experimental.pallas import tpu as pltpu

F32 = jnp.float32
BF16 = jnp.bfloat16

D_MODEL = 1024
DEPTH = 1
GRID_W = 64
LN_EPS = 1e-6
DEEPNORM_ALPHA = (2 * DEPTH) ** 0.25
GLA_HEADS = 4
GLA_DK = 64
GLA_DV = 128
GLA_KEY = GLA_HEADS * GLA_DK
GLA_VAL = GLA_HEADS * GLA_DV
GLA_RANK = 16
GLA_TAU = 16.0
HY_WIDTH = D_MODEL - GLA_VAL
HY_ORDER = 2
HY_CONV = 3
HY_EMB = 33
HY_FH = 64
HY_TARGET = 1e-2
HY_FAST = 0.3
HY_SLOW = 1.5
N_EXPERTS = 32
TOP_K = 4
D_EXPERT = D_MODEL
SWIGLU_ALPHA = 1.702
SWIGLU_LIMIT = 7.0

LANES = 128
VMEM_LIMIT = 56 * 1024 * 1024

TOK_TILE = 512
PROJ_SUB_TILE = 256
GLA_CHUNK = 128
GLA_BLOCK = 256
GLA_BATCH = 4
GLA_SAFE_EXPONENT = 80.0
A_PAD = LANES
HY_GROUP = 8
HY_CH_BLOCK = 16
FILT_TILE = 1024
HY_EMB_PAD = 40
MOE_TILE = 512
FFN_TILES_PER_STEP = 2
COMBINE_CHUNK = 32
ROW_TILES = D_MODEL // LANES


def _row_slab(n_rows, j):
    return pl.ds(j, n_rows, stride=ROW_TILES)


def _row_tile(r):
    return pl.ds(pl.multiple_of(r * ROW_TILES, ROW_TILES), ROW_TILES)
SWIGLU_BLOCK = 2 * LANES
NEG_BIG = -1e30


def _cparams(sem):
    return pltpu.CompilerParams(dimension_semantics=sem, vmem_limit_bytes=VMEM_LIMIT)


def _layer_norm(x):
    mu = jnp.mean(x, axis=-1, keepdims=True)
    xc = x - mu
    return xc * lax.rsqrt(jnp.mean(xc * xc, axis=-1, keepdims=True) + LN_EPS)


def _dot(a, b):
    return jnp.dot(a, b, preferred_element_type=F32)


def _dot_nt(a, b):
    return lax.dot_general(a, b, (((1,), (1,)), ((), ())), preferred_element_type=F32)


def _dot_tn(a, b):
    return lax.dot_general(a, b, (((0,), (0,)), ((), ())), preferred_element_type=F32)


def _mod_kernel(c_ref, w_ref, b_ref, o_ref):
    c = c_ref[...]
    s = c * jax.nn.sigmoid(c)
    o_ref[...] = _dot(s.astype(BF16), w_ref[...].astype(BF16)) + b_ref[...]


def _modulation(cvec, ada_w, ada_b):
    rows, d = cvec.shape
    n = ada_w.shape[1]
    tn = 1024
    return pl.pallas_call(
        _mod_kernel,
        out_shape=jax.ShapeDtypeStruct((rows, n), F32),
        grid=(n // tn,),
        in_specs=[
            pl.BlockSpec((rows, d), lambda j: (0, 0)),
            pl.BlockSpec((d, tn), lambda j: (0, j)),
            pl.BlockSpec((1, tn), lambda j: (0, j)),
        ],
        out_specs=pl.BlockSpec((rows, tn), lambda j: (0, j)),
        compiler_params=_cparams(("arbitrary",)),
        name="adaln_mod",
    )(cvec, ada_w, ada_b)


def _inproj_kernel(x_ref, sh_ref, sc_ref, wm_ref, *rest, with_hy):
    if with_hy:
        wh_ref, cw_ref, cb_ref, q_ref, k_ref, v_ref, g_ref, a_ref, u_ref = rest
    else:
        q_ref, k_ref, v_ref, a_ref = rest
    tt = x_ref.shape[1]
    n_sub = max(tt // PROJ_SUB_TILE, 1)
    sub = tt // n_sub
    for s in range(n_sub):
        rows = slice(s * sub, (s + 1) * sub)
        h = _layer_norm(x_ref[0, rows, :]) * (1.0 + sc_ref[0]) + sh_ref[0]
        hb = h.astype(BF16)
        u = _dot(hb, wm_ref[...])
        q_ref[0, rows, :] = u[:, :GLA_KEY].astype(BF16)
        k_ref[0, rows, :] = u[:, GLA_KEY:2 * GLA_KEY].astype(BF16)
        v_ref[0, rows, :] = u[:, 2 * GLA_KEY:2 * GLA_KEY + GLA_VAL].astype(BF16)
        off = 2 * GLA_KEY + GLA_VAL
        if with_hy:
            g_ref[0, rows, :] = u[:, off:off + GLA_VAL].astype(BF16)
            off += GLA_VAL
        a_ref[0, rows, :] = u[:, off:off + A_PAD]
        if with_hy:
            ut = _dot_nt(wh_ref[...], hb)
            lane = lax.broadcasted_iota(jnp.int32, (1, LANES), 1) % GRID_W
            not_first = (lane != 0).astype(F32)
            not_last = (lane != GRID_W - 1).astype(F32)
            for j in range(sub // LANES):
                c = ut[:, j * LANES:(j + 1) * LANES]
                left = pltpu.roll(c, 1, axis=1) * not_first
                right = pltpu.roll(c, LANES - 1, axis=1) * not_last
                y = cw_ref[0] * left + cw_ref[1] * c + cw_ref[2] * right + cb_ref[...]
                lanes = slice(s * sub + j * LANES, s * sub + (j + 1) * LANES)
                u_ref[0, :, lanes] = y.astype(BF16)


def _input_projection(x, shift, scale, w_main, w_hy_t=None, conv_w=None, conv_b=None):
    b, l, d = x.shape
    tt = min(TOK_TILE, l)
    with_hy = w_hy_t is not None
    nm = w_main.shape[1]
    tok = lambda width, dt: jax.ShapeDtypeStruct((b, l, width), dt)
    tok_spec = lambda width: pl.BlockSpec((1, tt, width), lambda i, j: (i, j, 0))
    in_specs = [
        pl.BlockSpec((1, tt, d), lambda i, j: (i, j, 0)),
        pl.BlockSpec((1, 1, d), lambda i, j: (i, 0, 0)),
        pl.BlockSpec((1, 1, d), lambda i, j: (i, 0, 0)),
        pl.BlockSpec((d, nm), lambda i, j: (0, 0)),
    ]
    args = [x, shift, scale, w_main]
    if with_hy:
        ch = w_hy_t.shape[0]
        in_specs += [
            pl.BlockSpec((ch, d), lambda i, j: (0, 0)),
            pl.BlockSpec((HY_CONV, ch, LANES), lambda i, j: (0, 0, 0)),
            pl.BlockSpec((ch, LANES), lambda i, j: (0, 0)),
        ]
        args += [w_hy_t, conv_w, conv_b]
        out_shape = [tok(GLA_KEY, BF16), tok(GLA_KEY, BF16), tok(GLA_VAL, BF16), tok(GLA_VAL, BF16),
                     tok(A_PAD, F32), jax.ShapeDtypeStruct((b, ch, l), BF16)]
        out_specs = [tok_spec(GLA_KEY), tok_spec(GLA_KEY), tok_spec(GLA_VAL), tok_spec(GLA_VAL),
                     tok_spec(A_PAD), pl.BlockSpec((1, ch, tt), lambda i, j: (i, 0, j))]
    else:
        out_shape = [tok(GLA_KEY, BF16), tok(GLA_KEY, BF16), tok(GLA_VAL, BF16), tok(A_PAD, F32)]
        out_specs = [tok_spec(GLA_KEY), tok_spec(GLA_KEY), tok_spec(GLA_VAL), tok_spec(A_PAD)]
    return pl.pallas_call(
        functools.partial(_inproj_kernel, with_hy=with_hy),
        out_shape=out_shape,
        grid=(b, l // tt),
        in_specs=in_specs,
        out_specs=out_specs,
        compiler_params=_cparams(("parallel", "arbitrary")),
        name="in_proj_hy" if with_hy else "in_proj_ctx",
    )(*args)


def _gla_kernel(*refs, n_ctx_blocks, chunk, n_batch, n_levels, safe_exponent):
    ins = [refs[0:8], refs[8:16]]
    wa_ref, ba_ref, cw_ref, cm_ref, hm_ref, bd_ref, dq_ref, dk_ref, lm_ref = refs[16:25]
    outs = refs[25:27]
    st_ref = refs[27]
    s = pl.program_id(1)
    blk = ins[0][4].shape[1]
    n_chunks = blk // chunk

    @pl.when(s == 0)
    def _():
        st_ref[...] = jnp.zeros_like(st_ref)

    def two_pass(m, g_hi, g_lo):
        return _dot(m, g_hi) + _dot(m, g_lo)

    is_ctx = s < n_ctx_blocks
    bdmask = bd_ref[...]
    chains = [(d, bi) for d in range(2) for bi in range(n_batch)]

    def pre_activation(d, bi):
        a = jnp.where(is_ctx, ins[d][3][bi], ins[d][7][bi])
        return _dot(a.astype(BF16), wa_ref[d]) + ba_ref[d]

    def advance(d, bi, z, single_reference):
        qc_ref, kc_ref, vc_ref, _, ql_ref, kl_ref, vl_ref, _ = ins[d]
        q = jnp.where(is_ctx, qc_ref[bi], ql_ref[bi]).astype(F32)
        k = jnp.where(is_ctx, kc_ref[bi], kl_ref[bi]).astype(F32)
        v = jnp.where(is_ctx, vc_ref[bi], vl_ref[bi])
        g = (jnp.minimum(z, 0.0) - jnp.log(1.0 + jnp.exp(-jnp.abs(z)))) * (1.0 / GLA_TAU)
        g_hi = g.astype(BF16)
        g_lo = (g - g_hi.astype(F32)).astype(BF16)
        cum = two_pass(cw_ref[d], g_hi, g_lo)
        edge = (lambda c: (c + 1) * chunk - 1) if d == 0 else (lambda c: c * chunk)
        tot = jnp.concatenate([jnp.broadcast_to(cum[edge(c):edge(c) + 1], (chunk, GLA_KEY))
                               for c in range(n_chunks)], axis=0)
        qs_all = (q * jnp.exp(cum)).astype(BF16)
        kst_all = (k * jnp.exp(tot - cum)).astype(BF16)
        dec_all = jnp.exp(tot)

        if single_reference:
            ks_all = (k * jnp.exp(-cum)).astype(BF16)
            cmask = cm_ref[d]
            rows_out = []
            for c in range(n_chunks):
                rows = slice(c * chunk, (c + 1) * chunk)
                parts = []
                for h in range(GLA_HEADS):
                    a_h = _dot_nt(qs_all[rows] * hm_ref[h], ks_all[rows]) * cmask
                    parts.append(_dot(a_h.astype(BF16), v[rows, h * GLA_DV:(h + 1) * GLA_DV]))
                rows_out.append(jnp.concatenate(parts, axis=1))
            o_intra = jnp.concatenate(rows_out, axis=0)
        else:
            def level(lv, acc):
                ql = (q * jnp.exp(two_pass(dq_ref[d, lv], g_hi, g_lo))).astype(BF16)
                kl = (k * jnp.exp(two_pass(dk_ref[d, lv], g_hi, g_lo))).astype(BF16)
                msk = lm_ref[d, lv]
                parts = []
                for h in range(GLA_HEADS):
                    a_h = _dot_nt(ql * hm_ref[h], kl) * msk
                    parts.append(_dot(a_h.astype(BF16), v[:, h * GLA_DV:(h + 1) * GLA_DV]))
                return acc + jnp.concatenate(parts, axis=1)
            o_intra = lax.fori_loop(0, n_levels, level, jnp.zeros((blk, GLA_VAL), F32))

        for n in range(n_chunks):
            ci = n if d == 0 else n_chunks - 1 - n
            rows = slice(ci * chunk, (ci + 1) * chunk)
            o = _dot_nt(qs_all[rows], st_ref[d, bi].astype(BF16)) + o_intra[rows]
            outs[d][bi, rows, :] = o.astype(outs[d].dtype)
            st_ref[d, bi] = (st_ref[d, bi] * dec_all[ci * chunk:ci * chunk + 1]
                             + _dot_tn(v[rows], kst_all[rows]) * bdmask)

    zs = [pre_activation(d, bi) for d, bi in chains]
    z_low = zs[0]
    for z in zs[1:]:
        z_low = jnp.minimum(z_low, z)
    bound = (jnp.maximum(-jnp.min(z_low), 0.0) + math.log(2.0)) * (chunk / GLA_TAU)

    def step(single_reference):
        def run():
            for (d, bi), z in zip(chains, zs):
                advance(d, bi, z, single_reference)
        return run

    lax.cond(bound < safe_exponent, step(True), step(False))


def _gla_level_tables(blk, chunk):
    sizes = []
    s = chunk // 2
    while s >= 1:
        sizes.append(s)
        s //= 2
    n_lv = len(sizes) + 1
    dq = np.zeros((2, n_lv, blk, blk), np.float32)
    dk = np.zeros((2, n_lv, blk, blk), np.float32)
    lm = np.zeros((2, n_lv, blk, blk), np.float32)
    idx = np.arange(blk)
    for lv, s in enumerate(sizes):
        blk_id = idx // (2 * s)
        right = (idx % (2 * s)) >= s
        b = blk_id * 2 * s + s
        same = blk_id[:, None] == blk_id[None, :]
        m = idx[None, :]
        dq[0, lv] = (right[:, None] & (m >= b[:, None]) & (m <= idx[:, None]))
        dk[0, lv] = (~right[:, None] & (m > idx[:, None]) & (m < b[:, None]))
        lm[0, lv] = same & right[:, None] & ~right[None, :]
        dq[1, lv] = (~right[:, None] & (m >= idx[:, None]) & (m < b[:, None]))
        dk[1, lv] = (right[:, None] & (m >= b[:, None]) & (m < idx[:, None]))
        lm[1, lv] = same & ~right[:, None] & right[None, :]
    lm[:, n_lv - 1] = np.eye(blk, dtype=np.float32)
    return dq, dk, lm, n_lv


def _gla_scan(ctx_qkva, lat_qkva, wa, ba, chunk=GLA_CHUNK, blk=GLA_BLOCK, safe_exponent=GLA_SAFE_EXPONENT):
    qc, kc, vc, ac = ctx_qkva
    ql, kl, vl, al = lat_qkva
    b, l, _ = ql.shape
    ctx_len = qc.shape[1]
    n_ctx = ctx_len // blk
    n_lat = l // blk
    c = chunk
    idx = np.arange(c)
    tri = np.stack([idx[:, None] >= idx[None, :], idx[:, None] <= idx[None, :]]).astype(np.float32)
    cw = np.stack([np.kron(np.eye(blk // c, dtype=np.float32), tri[dd]) for dd in range(2)])
    dq, dk, lm, n_levels = _gla_level_tables(blk, c)
    hmask = np.zeros((GLA_HEADS, 1, GLA_KEY), np.float32)
    for h in range(GLA_HEADS):
        hmask[h, 0, h * GLA_DK:(h + 1) * GLA_DK] = 1.0
    bd = (np.arange(GLA_VAL)[:, None] // GLA_DV == np.arange(GLA_KEY)[None, :] // GLA_DK).astype(np.float32)

    nb = GLA_BATCH if b % GLA_BATCH == 0 else 1

    def lat_block(d):
        def f(i, s):
            t = jnp.maximum(s - n_ctx, 0)
            return (i, t if d == 0 else n_lat - 1 - t, 0)
        return f

    def ctx_block(d):
        def f(i, s):
            t = jnp.minimum(s, n_ctx - 1)
            return (i, t if d == 0 else n_ctx - 1 - t, 0)
        return f

    widths = (GLA_KEY, GLA_KEY, GLA_VAL, A_PAD)
    dir_specs = lambda d: ([pl.BlockSpec((nb, blk, w), ctx_block(d)) for w in widths]
                           + [pl.BlockSpec((nb, blk, w), lat_block(d)) for w in widths])
    consts = [wa, ba, jnp.asarray(cw, BF16), jnp.asarray(tri, F32), jnp.asarray(hmask, BF16), jnp.asarray(bd, F32),
              jnp.asarray(dq, BF16), jnp.asarray(dk, BF16), jnp.asarray(lm, F32)]
    full = lambda arr: pl.BlockSpec(arr.shape, lambda i, s: (0,) * arr.ndim)
    o_sds = jax.ShapeDtypeStruct((b, l, GLA_VAL), BF16)
    dir_args = [qc, kc, vc, ac, ql, kl, vl, al]
    return pl.pallas_call(
        functools.partial(_gla_kernel, n_ctx_blocks=n_ctx, chunk=c, n_batch=nb, n_levels=n_levels,
                          safe_exponent=safe_exponent),
        out_shape=[o_sds, o_sds],
        grid=(b // nb, n_ctx + n_lat),
        in_specs=dir_specs(0) + dir_specs(1) + [full(arr) for arr in consts],
        out_specs=[pl.BlockSpec((nb, blk, GLA_VAL), lat_block(0)), pl.BlockSpec((nb, blk, GLA_VAL), lat_block(1))],
        scratch_shapes=[pltpu.VMEM((2, nb, GLA_VAL, GLA_KEY), F32)],
        compiler_params=_cparams(("parallel", "arbitrary")),
        name="gla_scan",
    )(*dir_args, *dir_args, *consts)


def _dft_constants(r1):
    n = r1 * LANES
    h = r1 // 2
    k1 = np.arange(r1)
    f1 = np.exp(-2j * np.pi * np.outer(k1, k1) / r1)
    f2 = np.exp(-2j * np.pi * np.outer(np.arange(LANES), np.arange(LANES)) / LANES)
    tw = np.exp(-2j * np.pi * np.outer(k1, np.arange(LANES)) / n)
    fa_c = np.block([[f1.real[:, :h], -f1.imag[:, :h]], [f1.imag[:, :h], f1.real[:, :h]]])
    fa_r = np.concatenate([f1.real, f1.imag], axis=0)
    gc = np.block([[f2.real, f2.imag], [-f2.imag, f2.real]])
    gci = np.block([[f2.real, -f2.imag], [f2.imag, f2.real]])
    fai = np.block([[f1.real[:h], f1.imag[:h]], [-f1.imag[:h], f1.real[:h]]]) / n
    tw_lane = np.tile(tw, (1, HY_GROUP))
    tw_row = np.tile(tw, (HY_GROUP, 1))
    f = lambda a: jnp.asarray(a, F32)
    return dict(fa_c=f(fa_c), fa_r=f(fa_r), gc=f(gc), gci=f(gci), fai=f(fai),
                twl_r=f(tw_lane.real), twl_i=f(tw_lane.imag), twr_r=f(tw_row.real), twr_i=f(tw_row.imag))


def _fwd_dft(rhs, fa, gc, twl_r, twl_i, r1):
    a = _dot(fa, rhs)
    ar, ai = a[:r1], a[r1:]
    br = (ar * twl_r - ai * twl_i).astype(BF16)
    bi = (ar * twl_i + ai * twl_r).astype(BF16)
    lhs = jnp.concatenate(
        [jnp.concatenate([br[:, c * LANES:(c + 1) * LANES], bi[:, c * LANES:(c + 1) * LANES]], axis=1)
         for c in range(HY_GROUP)], axis=0)
    return _dot(lhs, gc)


def _inv_dft(yr, yi, gci, fai, twr_r, twr_i, r1):
    lhs = jnp.concatenate([yr, yi], axis=1).astype(BF16)
    c = _dot(lhs, gci)
    cr, ci = c[:, :LANES], c[:, LANES:]
    dr = (cr * twr_r + ci * twr_i).astype(BF16)
    di = (ci * twr_r - cr * twr_i).astype(BF16)
    rhs = jnp.concatenate(
        [jnp.concatenate([dr[g * r1:(g + 1) * r1], di[g * r1:(g + 1) * r1]], axis=0)
         for g in range(HY_GROUP)], axis=1)
    return _dot(fai, rhs)


def _filter_mlp_kernel(zt_ref, tn_ref, rate_ref, w1_ref, b1_ref, w2_ref, b2_ref, fr_ref, wo_ref, o_ref):
    hp = lax.Precision.HIGHEST
    lt = tn_ref.shape[1]
    lanes = lambda ref: jnp.concatenate([ref[...]] * (lt // LANES), axis=1)
    fr = lanes(fr_ref)
    hid = jnp.sin(fr * (jnp.dot(w1_ref[...], zt_ref[...], precision=hp, preferred_element_type=F32) + lanes(b1_ref)))
    hid = jnp.sin(fr * (jnp.dot(w2_ref[...], hid, precision=hp, preferred_element_type=F32) + lanes(b2_ref)))
    window = jnp.exp(-tn_ref[0:1, :] * lanes(rate_ref)) * tn_ref[1:2, :]
    for o in range(HY_ORDER):
        o_ref[o] = jnp.dot(wo_ref[o, 0], hid, precision=hp, preferred_element_type=F32) * window


def _filter_mlp(z_t, tn_rows, rate, w1_t, b1, w2_t, b2, freq, wo_t):
    emb, n2l = z_t.shape
    l = n2l // 2
    lt = min(FILT_TILE, l)
    nt = l // lt
    ch = rate.shape[0]
    full = lambda shape: pl.BlockSpec(shape, lambda d, j: (0,) * len(shape))
    return pl.pallas_call(
        _filter_mlp_kernel,
        out_shape=jax.ShapeDtypeStruct((HY_ORDER, ch, n2l), F32),
        grid=(2, nt),
        in_specs=[
            pl.BlockSpec((emb, lt), lambda d, j: (0, d * nt + j)),
            pl.BlockSpec((8, lt), lambda d, j: (0, d * nt + j)),
            full((ch, LANES)),
            full((HY_FH, emb)), full((HY_FH, LANES)), full((HY_FH, HY_FH)), full((HY_FH, LANES)),
            full((HY_FH, LANES)),
            pl.BlockSpec((HY_ORDER, 1, ch, HY_FH), lambda d, j: (0, d, 0, 0)),
        ],
        out_specs=pl.BlockSpec((HY_ORDER, ch, lt), lambda d, j: (0, 0, d * nt + j)),
        compiler_params=_cparams(("arbitrary", "arbitrary")),
        name="hyena_filter_mlp",
    )(z_t, tn_rows, rate, w1_t, b1, w2_t, b2, freq, wo_t)


def _filter_fft_kernel(f_ref, fa_ref, gc_ref, twl_r_ref, twl_i_ref, h_ref, *, r1):
    fa = fa_ref[...].astype(BF16)
    gc = gc_ref[...].astype(BF16)
    nc = f_ref.shape[1]
    for g0 in range(0, nc, HY_GROUP):
        rhs = jnp.concatenate([f_ref[0, g0 + c].astype(BF16) for c in range(HY_GROUP)], axis=1)
        x = _fwd_dft(rhs, fa, gc, twl_r_ref[...], twl_i_ref[...], r1)
        for c in range(HY_GROUP):
            h_ref[0, g0 + c] = x[c * r1:(c + 1) * r1].astype(h_ref.dtype)


def _filter_spectra(filt, consts, r1):
    order, ch = filt.shape[:2]
    nc = HY_CH_BLOCK
    full = lambda a: pl.BlockSpec(a.shape, lambda o, j: (0,) * a.ndim)
    cs = [consts["fa_r"], consts["gc"], consts["twl_r"], consts["twl_i"]]
    return pl.pallas_call(
        functools.partial(_filter_fft_kernel, r1=r1),
        out_shape=jax.ShapeDtypeStruct((order, ch, r1, 2 * LANES), BF16),
        grid=(order, ch // nc),
        in_specs=[pl.BlockSpec((1, nc, r1, LANES), lambda o, j: (o, j, 0, 0))] + [full(a) for a in cs],
        out_specs=pl.BlockSpec((1, nc, r1, 2 * LANES), lambda o, j: (o, j, 0, 0)),
        compiler_params=_cparams(("arbitrary", "arbitrary")),
        name="hyena_filter_fft",
    )(filt, *cs)


def _hyena_kernel(dbias_ref, v_ref, x1_ref, x2_ref, h_ref, fa_ref, gc_ref, gci_ref, fai_ref,
                  twl_r_ref, twl_i_ref, twr_r_ref, twr_i_ref, y_ref, *, r1):
    fa = fa_ref[...].astype(BF16)
    gc = gc_ref[...].astype(BF16)
    gci = gci_ref[...].astype(BF16)
    fai = fai_ref[...].astype(BF16)
    twl_r, twl_i = twl_r_ref[...], twl_i_ref[...]
    twr_r, twr_i = twr_r_ref[...], twr_i_ref[...]
    nc = v_ref.shape[2]
    half = r1 // 2
    c_base = pl.program_id(0) * nc

    def conv(sig, sig_b, order, g0):
        rhs = jnp.concatenate(
            [jnp.concatenate([sig_b[c][0], sig_b[c][1]], axis=0) for c in range(HY_GROUP)], axis=1)
        x = _fwd_dft(rhs, fa, gc, twl_r, twl_i, r1)
        xr, xi = x[:, :LANES], x[:, LANES:]
        hh = jnp.concatenate([h_ref[order, g0 + c] for c in range(HY_GROUP)], axis=0).astype(F32)
        hr, hi = hh[:, :LANES], hh[:, LANES:]
        y = _inv_dft(xr * hr - xi * hi, xr * hi + xi * hr, gci, fai, twr_r, twr_i, r1)
        out = []
        for c in range(HY_GROUP):
            dcoef = dbias_ref[order, c_base + g0 + c]
            yc = y[:, c * LANES:(c + 1) * LANES]
            out.append([yc[:half] + dcoef * sig[c][0], yc[half:] + dcoef * sig[c][1]])
        return out

    for g0 in range(0, nc, HY_GROUP):
        v_b = [[v_ref[0, b, g0 + c] for b in range(2)] for c in range(HY_GROUP)]
        v = [[t.astype(F32) for t in pair] for pair in v_b]
        y1 = conv(v, v_b, 0, g0)
        z = [[x1_ref[0, b, g0 + c].astype(F32) * y1[c][b] for b in range(2)] for c in range(HY_GROUP)]
        z_b = [[t.astype(BF16) for t in pair] for pair in z]
        y2 = conv(z, z_b, 1, g0)
        for c in range(HY_GROUP):
            for b in range(2):
                y_ref[0, b, g0 + c] = (x2_ref[0, b, g0 + c].astype(F32) * y2[c][b]).astype(y_ref.dtype)


def _hyena_conv(u_t, spectra, d_bias, consts, r1):
    bp, _, ch3, half, _ = u_t.shape
    ch = ch3 // 3
    nc = HY_CH_BLOCK
    nblk = ch // nc
    names = ["fa_c", "gc", "gci", "fai", "twl_r", "twl_i", "twr_r", "twr_i"]
    cs = [consts[k] for k in names]
    full = lambda a: pl.BlockSpec(a.shape, lambda j, p: (0,) * a.ndim)
    part = lambda k: pl.BlockSpec((1, 2, nc, half, LANES), lambda j, p: (p, 0, k * nblk + j, 0, 0))
    return pl.pallas_call(
        functools.partial(_hyena_kernel, r1=r1),
        out_shape=jax.ShapeDtypeStruct((bp, 2, ch, half, LANES), BF16),
        grid=(nblk, bp),
        in_specs=[pl.BlockSpec(memory_space=pltpu.SMEM), part(0), part(1), part(2),
                  pl.BlockSpec((HY_ORDER, nc, r1, 2 * LANES), lambda j, p: (0, j, 0, 0))] + [full(a) for a in cs],
        out_specs=pl.BlockSpec((1, 2, nc, half, LANES), lambda j, p: (p, 0, j, 0, 0)),
        compiler_params=_cparams(("arbitrary", "arbitrary")),
        name="hyena_conv",
    )(d_bias, u_t, u_t, u_t, spectra, *cs)


def _filter_inputs(l):
    n = jnp.arange(2 * l, dtype=jnp.int32)
    t = jnp.where(n < l, n, 2 * l - n).astype(F32)
    valid = (n != l).astype(F32)
    t_norm = t / (l - 1)
    bands = (HY_EMB - 1) // 2
    f = jnp.linspace(1e-4, bands - 1, bands, dtype=F32)
    ang = (2.0 * math.pi * t / l)[:, None] * f[None, :]
    z = jnp.concatenate([t_norm[:, None], jnp.cos(ang), -jnp.sin(ang)], -1)
    zz = jnp.pad(z, ((0, 0), (0, HY_EMB_PAD - HY_EMB))).T
    tn_rows = jnp.zeros((8, 2 * l), F32).at[0].set(t_norm).at[1].set(valid)
    deltas = jnp.linspace(math.log(HY_TARGET) / HY_SLOW, math.log(HY_TARGET) / HY_FAST, HY_WIDTH, dtype=F32)
    rate = jnp.broadcast_to(jnp.abs(deltas)[:, None], (HY_WIDTH, LANES))
    return zz, tn_rows, rate


def _outproj_kernel(of_ref, ob_ref, g_ref, yh_ref, x_ref, gate_ref, sh_ref, sc_ref,
                    wg_ref, wh_ref, ng_ref, l1g_ref, l1b_ref, wr_ref, rb_ref, ltri_ref,
                    xm_ref, h2_ref, se_ref, sr_ref, sw_ref, cnt_ref, cs_ref, carry_ref):
    @pl.when((pl.program_id(0) == 0) & (pl.program_id(1) == 0))
    def _():
        carry_ref[...] = jnp.zeros_like(carry_ref)

    cs_ref[0] = jnp.broadcast_to(carry_ref[0:1, :], cs_ref.shape[1:]).astype(jnp.int32)

    tt = x_ref.shape[1]
    sub = ltri_ref.shape[0]
    for s in range(tt // sub):
        rows = slice(s * sub, (s + 1) * sub)
        o = of_ref[0, rows, :].astype(F32) + ob_ref[0, rows, :].astype(F32)
        g = g_ref[0, rows, :].astype(F32)
        parts = []
        for h in range(GLA_HEADS):
            oh = o[:, h * GLA_DV:(h + 1) * GLA_DV]
            parts.append(oh * lax.rsqrt(jnp.mean(oh * oh, axis=-1, keepdims=True) + LN_EPS))
        y_gla = jnp.concatenate(parts, axis=1) * ng_ref[...] * (g * jax.nn.sigmoid(g))
        f = _dot(y_gla.astype(BF16), wg_ref[...]) + _dot_tn(yh_ref[0, :, rows], wh_ref[...])
        x_mid = (_layer_norm(DEEPNORM_ALPHA * x_ref[0, rows, :] + gate_ref[0] * f) * l1g_ref[...]
                 + l1b_ref[...])
        xm_ref[0, rows, :] = x_mid
        h2 = _layer_norm(x_mid) * (1.0 + sc_ref[0]) + sh_ref[0]
        for j in range(ROW_TILES):
            h2_ref[0, pl.ds(s * sub * ROW_TILES + j, sub, stride=ROW_TILES), :] = h2[:, j * LANES:(j + 1) * LANES]

        logits = _dot(h2.astype(BF16), wr_ref[...]) + rb_ref[...]
        lane = lax.broadcasted_iota(jnp.int32, logits.shape, 1).astype(F32)
        hits, idxs, exps = [], [], []
        m0 = None
        for _ in range(TOP_K):
            m = jnp.max(logits, axis=-1, keepdims=True)
            idx = jnp.min(jnp.where(logits == m, lane, float(LANES)), axis=-1, keepdims=True)
            hit = lane == idx
            m0 = m if m0 is None else m0
            hits.append(hit)
            idxs.append(idx)
            exps.append(jnp.exp(m - m0))
            logits = jnp.where(hit, NEG_BIG, logits)
        denom = exps[0]
        sel = jnp.where(hits[0], 1.0, 0.0)
        for kk in range(1, TOP_K):
            denom = denom + exps[kk]
            sel = sel + jnp.where(hits[kk], 1.0, 0.0)
        rank_all = _dot(ltri_ref[...], sel.astype(BF16)) + carry_ref[0:1, :]
        carry_ref[0:1, :] = carry_ref[0:1, :] + jnp.sum(sel, axis=0, keepdims=True)
        se = jnp.zeros(logits.shape, F32)
        sr = jnp.zeros(logits.shape, F32)
        sw = jnp.zeros(logits.shape, F32)
        for kk in range(TOP_K):
            rk = jnp.sum(jnp.where(hits[kk], rank_all, 0.0), axis=-1, keepdims=True)
            col = lane == float(kk)
            se = jnp.where(col, idxs[kk], se)
            sr = jnp.where(col, rk, sr)
            sw = jnp.where(col, exps[kk] / denom, sw)
        se_ref[0, rows, :] = se.astype(jnp.int32)
        sr_ref[0, rows, :] = sr.astype(jnp.int32)
        sw_ref[0, rows, :] = sw
    cnt_ref[...] = jnp.broadcast_to(carry_ref[0:1, :], cnt_ref.shape).astype(jnp.int32)


def _output_projection(o_f, o_b, g, y_hy_t, x, gate1, shift2, scale2, w_gla, w_hy, norm_g, ln1_g, ln1_b, wr, rb):
    b, l, d = x.shape
    tt = TOK_TILE
    tok = lambda w: pl.BlockSpec((1, tt, w), lambda i, j: (i, j, 0))
    row = lambda: pl.BlockSpec((1, 1, d), lambda i, j: (i, 0, 0))
    full = lambda a: pl.BlockSpec(a.shape, lambda i, j: (0,) * a.ndim)
    sub = tt
    ltri = jnp.asarray(np.tril(np.ones((sub, sub), np.float32), -1), BF16)
    consts = [w_gla, w_hy, norm_g, ln1_g, ln1_b, wr, rb, ltri]
    lane_i = jax.ShapeDtypeStruct((b, l, LANES), jnp.int32)
    return pl.pallas_call(
        _outproj_kernel,
        out_shape=[jax.ShapeDtypeStruct((b, l, d), F32), jax.ShapeDtypeStruct((b, l * ROW_TILES, LANES), F32),
                   lane_i, lane_i, jax.ShapeDtypeStruct((b, l, LANES), F32),
                   jax.ShapeDtypeStruct((8, LANES), jnp.int32),
                   jax.ShapeDtypeStruct((b * (l // tt), 8, LANES), jnp.int32)],
        grid=(b, l // tt),
        in_specs=[
            tok(GLA_VAL), tok(GLA_VAL), tok(GLA_VAL),
            pl.BlockSpec((1, HY_WIDTH, tt), lambda i, j: (i, 0, j)),
            tok(d), row(), row(), row(),
        ] + [full(a) for a in consts],
        out_specs=[tok(d), pl.BlockSpec((1, tt * ROW_TILES, LANES), lambda i, j: (i, j, 0)),
                   tok(LANES), tok(LANES), tok(LANES),
                   pl.BlockSpec((8, LANES), lambda i, j: (0, 0)),
                   pl.BlockSpec((1, 8, LANES), lambda i, j: (i * (l // tt) + j, 0, 0))],
        scratch_shapes=[pltpu.VMEM((8, LANES), F32)],
        compiler_params=_cparams(("arbitrary", "arbitrary")),
        name="out_proj_router",
    )(o_f, o_b, g, y_hy_t, x, gate1, shift2, scale2, *consts)


def _expert_weight_layout(w1_ref, w2_ref, p_ref, w1o_ref, w2o_ref):
    p = p_ref[...]
    for j in range(w1_ref.shape[2] // SWIGLU_BLOCK):
        cols = slice(j * SWIGLU_BLOCK, (j + 1) * SWIGLU_BLOCK)
        w1o_ref[0, :, cols] = _dot(w1_ref[0, :, cols].astype(BF16), p).astype(BF16)
    w2o_ref[0] = w2_ref[0].astype(BF16)


def _dispatch_kernel(base_ref, cnt_ref, nt_ref, ev_ref, rv_ref, h2_ref, w1_ref, w2_ref, p_ref,
                     xs_hbm, w1o_ref, w2o_ref, zrow_ref, zblk_ref, slotv_ref, slots_ref, sem, zsem, ssem):
    tt = h2_ref.shape[0] // ROW_TILES
    ev = ev_ref[...]
    slots = rv_ref[...]
    for e in range(N_EXPERTS):
        slots = slots + jnp.where(ev == e, base_ref[e], 0)
    slotv_ref[...] = slots
    to_smem = pltpu.make_async_copy(slotv_ref, slots_ref, ssem)
    to_smem.start()
    to_smem.wait()

    per_line = LANES // TOP_K

    def line(li, carry):
        for u in range(per_line):
            for kk in range(TOP_K):
                pltpu.make_async_copy(h2_ref.at[_row_tile(li * per_line + u)],
                                      xs_hbm.at[_row_tile(slots_ref[li, u * TOP_K + kk])],
                                      sem).start(priority=kk % 2)
        return carry

    lax.fori_loop(0, tt // per_line, line, 0)
    _expert_weight_layout(w1_ref, w2_ref, p_ref, w1o_ref, w2o_ref)
    for _ in range(TOP_K):
        pltpu.make_async_copy(h2_ref, xs_hbm.at[pl.ds(0, tt * ROW_TILES)], sem).wait()

    @pl.when(pl.program_id(0) == pl.num_programs(0) - 1)
    def _():
        zrow_ref[...] = jnp.zeros_like(zrow_ref)

        def per_expert(e, carry):
            n = cnt_ref[e]
            end = ((n + (MOE_TILE - 1)) // MOE_TILE) * MOE_TILE

            def fill(r, c):
                pltpu.make_async_copy(zrow_ref, xs_hbm.at[_row_tile(base_ref[e] + r)], zsem).start()
                return c

            def drain(r, c):
                pltpu.make_async_copy(zrow_ref, xs_hbm.at[_row_tile(0)], zsem).wait()
                return c

            lax.fori_loop(n, end, fill, 0)
            lax.fori_loop(n, end, drain, 0)
            return carry

        lax.fori_loop(0, N_EXPERTS, per_expert, 0)

        zblk_ref[...] = jnp.zeros_like(zblk_ref)
        tile_rows = MOE_TILE * ROW_TILES
        n_all = xs_hbm.shape[0] // tile_rows

        def tile_copy(ti):
            row0 = pl.multiple_of(ti * tile_rows, tile_rows)
            return pltpu.make_async_copy(zblk_ref, xs_hbm.at[pl.ds(row0, tile_rows)], zsem)

        def fill_tile(ti, c):
            tile_copy(ti).start()
            return c

        def drain_tile(ti, c):
            tile_copy(ti).wait()
            return c

        lax.fori_loop(nt_ref[0], n_all, fill_tile, 0)
        lax.fori_loop(nt_ref[0], n_all, drain_tile, 0)


def _dispatch(h2, e_lines, r_lines, base, counts, n_tiles, n_slots, w1, w2):
    t = h2.shape[0] // ROW_TILES
    ne, d, f2 = w1.shape
    assert t % (ne * 2 * LANES // TOP_K) == 0, "one dispatch step per expert, whole index lines per step"
    tt = t // ne
    lines = tt * TOP_K // LANES
    line_blk = lambda: pl.BlockSpec((lines, LANES), lambda i, *_: (i, 0))
    src = np.concatenate([np.arange(0, SWIGLU_BLOCK, 2), np.arange(1, SWIGLU_BLOCK, 2)])
    perm = np.zeros((SWIGLU_BLOCK, SWIGLU_BLOCK), np.float32)
    perm[src, np.arange(SWIGLU_BLOCK)] = 1.0
    exp_blk = lambda shape: pl.BlockSpec((1,) + shape, lambda i, *_: (i, 0, 0))
    return pl.pallas_call(
        _dispatch_kernel,
        out_shape=[jax.ShapeDtypeStruct((n_slots * ROW_TILES, LANES), F32),
                   jax.ShapeDtypeStruct(w1.shape, BF16), jax.ShapeDtypeStruct(w2.shape, BF16)],
        grid_spec=pltpu.PrefetchScalarGridSpec(
            num_scalar_prefetch=3,
            grid=(ne,),
            in_specs=[line_blk(), line_blk(),
                      pl.BlockSpec((tt * ROW_TILES, LANES), lambda i, *_: (i, 0)),
                      exp_blk((d, f2)), exp_blk(w2.shape[1:]),
                      pl.BlockSpec((SWIGLU_BLOCK, SWIGLU_BLOCK), lambda i, *_: (0, 0))],
            out_specs=[pl.BlockSpec(memory_space=pl.ANY), exp_blk((d, f2)), exp_blk(w2.shape[1:])],
            scratch_shapes=[pltpu.VMEM((ROW_TILES, LANES), F32), pltpu.VMEM((MOE_TILE * ROW_TILES, LANES), F32),
                            pltpu.VMEM((lines, LANES), jnp.int32), pltpu.SMEM((lines, LANES), jnp.int32),
                            pltpu.SemaphoreType.DMA, pltpu.SemaphoreType.DMA, pltpu.SemaphoreType.DMA],
        ),
        compiler_params=_cparams(("arbitrary",)),
        name="moe_dispatch",
    )(base, counts, n_tiles, e_lines, r_lines, h2, w1, w2, jnp.asarray(perm, BF16))


def _ffn_kernel(te_ref, nt_ref, xs_ref, *refs):
    ys_ref = refs[-1]
    tm = MOE_TILE
    rows = tm * ROW_TILES
    n_here = jnp.clip(nt_ref[0] - pl.program_id(0) * FFN_TILES_PER_STEP, 0, FFN_TILES_PER_STEP)

    def tile(t):
        w1_ref, b1_ref, w2_ref, b2_ref = refs[4 * t:4 * t + 4]
        x = jnp.concatenate([xs_ref[pl.ds(t * rows + j, tm, stride=ROW_TILES), :] for j in range(ROW_TILES)], axis=1)
        hid = _dot(x.astype(BF16), w1_ref[0]) + b1_ref[0]
        acts = []
        for j in range(hid.shape[1] // SWIGLU_BLOCK):
            glu = jnp.minimum(hid[:, j * SWIGLU_BLOCK:j * SWIGLU_BLOCK + LANES], SWIGLU_LIMIT)
            lin = jnp.clip(hid[:, j * SWIGLU_BLOCK + LANES:(j + 1) * SWIGLU_BLOCK], -SWIGLU_LIMIT, SWIGLU_LIMIT)
            acts.append((glu * jax.nn.sigmoid(SWIGLU_ALPHA * glu) * (lin + 1.0)).astype(BF16))
        y = _dot(jnp.concatenate(acts, axis=1), w2_ref[0]) + b2_ref[0]
        for j in range(ROW_TILES):
            ys_ref[pl.ds(t * rows + j, tm, stride=ROW_TILES), :] = y[:, j * LANES:(j + 1) * LANES]

    for n_valid in range(FFN_TILES_PER_STEP + 1):
        @pl.when(n_here == n_valid)
        def _(n_valid=n_valid):
            for t in range(n_valid):
                tile(t)
            for t in range(n_valid, FFN_TILES_PER_STEP):
                ys_ref[t * rows:(t + 1) * rows, :] = jnp.zeros((rows, LANES), F32)


def _expert_ffn(xs, tile_expert, n_tiles, w1p, b1p, w2b, b2):
    n_slots = xs.shape[0] // ROW_TILES
    d = w1p.shape[1]
    tps = FFN_TILES_PER_STEP
    tm = MOE_TILE
    f2 = w1p.shape[2]
    assert n_slots % (tm * tps) == 0
    rows_blk = (tps * tm * ROW_TILES, LANES)
    weights = []
    for t in range(tps):
        exp_blk = lambda i, te, nt, t=t: (te[i * tps + t], 0, 0)
        weights += [pl.BlockSpec((1, d, f2), exp_blk), pl.BlockSpec((1, 1, f2), exp_blk),
                    pl.BlockSpec((1, f2 // 2, d), exp_blk), pl.BlockSpec((1, 1, d), exp_blk)]
    return pl.pallas_call(
        _ffn_kernel,
        out_shape=jax.ShapeDtypeStruct(xs.shape, F32),
        grid_spec=pltpu.PrefetchScalarGridSpec(
            num_scalar_prefetch=2,
            grid=(n_slots // (tm * tps),),
            in_specs=[
                pl.BlockSpec(rows_blk, lambda i, te, nt: (jnp.minimum(i, (nt[0] - 1) // tps), 0)),
            ] + weights,
            out_specs=pl.BlockSpec(rows_blk, lambda i, te, nt: (i, 0)),
        ),
        compiler_params=_cparams(("arbitrary",)),
        name="moe_expert_ffn",
    )(tile_expert, n_tiles, xs, *([w1p, b1p, w2b, b2] * tps))


def _combine_kernel(cs_ref, base_ref, w_ref, ev_ref, rv_ref, xm_ref, gate_ref, l2g_ref, l2b_ref,
                    ys_hbm, o_ref, buf_ref, acc_ref, rowv_ref, rows_ref, seg_ref, nch_ref, sem, rsem):
    i = pl.program_id(0)
    tt = xm_ref.shape[0]
    cur = lax.rem(i, 2)
    chunk_rows = COMBINE_CHUNK * ROW_TILES

    def chunk_copy(src_row, dst_row, sl):
        return pltpu.make_async_copy(ys_hbm.at[pl.ds(pl.multiple_of(src_row * ROW_TILES, ROW_TILES), chunk_rows)],
                                     buf_ref.at[sl, pl.ds(pl.multiple_of(dst_row * ROW_TILES, ROW_TILES), chunk_rows)],
                                     sem.at[sl])

    def issue(tile, sl):
        def per_expert(e, off):
            start = cs_ref[tile, e]
            n_chunks = (cs_ref[tile + 1, e] - start + (COMBINE_CHUNK - 1)) // COMBINE_CHUNK
            seg_ref[sl, e] = off - start

            def one(c, carry):
                chunk_copy(base_ref[e] + start + c * COMBINE_CHUNK, off + c * COMBINE_CHUNK, sl).start()
                return carry

            lax.fori_loop(0, n_chunks, one, 0)
            return off + n_chunks * COMBINE_CHUNK

        total = lax.fori_loop(0, N_EXPERTS, per_expert, 0)
        nch_ref[sl] = total // COMBINE_CHUNK

    @pl.when(i == 0)
    def _():
        issue(0, 0)

    @pl.when(i + 1 < pl.num_programs(0))
    def _():
        issue(i + 1, 1 - cur)

    def drain(c, carry):
        chunk_copy(0, 0, cur).wait()
        return carry

    ev = ev_ref[...]
    rows = rv_ref[...]
    for e in range(N_EXPERTS):
        rows = rows + jnp.where(ev == e, seg_ref[cur, e], 0)
    rowv_ref[...] = rows
    to_smem = pltpu.make_async_copy(rowv_ref, rows_ref, rsem)
    to_smem.start()
    lax.fori_loop(0, nch_ref[cur], drain, 0)
    to_smem.wait()

    per_line = LANES // TOP_K

    def line(li, carry):
        for u in range(per_line):
            acc = None
            for kk in range(TOP_K):
                a = u * TOP_K + kk
                term = w_ref[li * LANES + a] * buf_ref[cur, _row_tile(rows_ref[li, a]), :]
                acc = term if acc is None else acc + term
            acc_ref[_row_tile(li * per_line + u), :] = acc
        return carry

    lax.fori_loop(0, tt // per_line, line, 0)
    mixed = jnp.concatenate([acc_ref[_row_slab(tt, j), :] for j in range(ROW_TILES)], axis=1)
    pre = DEEPNORM_ALPHA * xm_ref[...] + gate_ref[0] * mixed
    o_ref[...] = _layer_norm(pre) * l2g_ref[...] + l2b_ref[...]


def _combine(ys, tile_starts, base, e_lines, r_lines, w_flat, x_mid, gate2, ln2_g, ln2_b, tiles_per_batch):
    t, d = x_mid.shape
    tt = TOK_TILE
    n = t // tt
    cap = tt * TOP_K + N_EXPERTS * COMBINE_CHUNK
    lines = tt * TOP_K // LANES
    line_blk = lambda: pl.BlockSpec((lines, LANES), lambda i, *_: (i, 0))
    return pl.pallas_call(
        _combine_kernel,
        out_shape=jax.ShapeDtypeStruct((t, d), F32),
        grid_spec=pltpu.PrefetchScalarGridSpec(
            num_scalar_prefetch=2,
            grid=(n,),
            in_specs=[
                pl.BlockSpec((tt * TOP_K,), lambda i, *_: (i,), memory_space=pltpu.SMEM),
                line_blk(), line_blk(),
                pl.BlockSpec((tt, d), lambda i, *_: (i, 0)),
                pl.BlockSpec((1, 1, d), lambda i, *_: (i // tiles_per_batch, 0, 0)),
                pl.BlockSpec((1, d), lambda i, *_: (0, 0)),
                pl.BlockSpec((1, d), lambda i, *_: (0, 0)),
                pl.BlockSpec(memory_space=pl.ANY),
            ],
            out_specs=pl.BlockSpec((tt, d), lambda i, *_: (i, 0)),
            scratch_shapes=[pltpu.VMEM((2, cap * ROW_TILES, LANES), F32), pltpu.VMEM((tt * ROW_TILES, LANES), F32),
                            pltpu.VMEM((lines, LANES), jnp.int32), pltpu.SMEM((lines, LANES), jnp.int32),
                            pltpu.SMEM((2, N_EXPERTS), jnp.int32), pltpu.SMEM((2,), jnp.int32),
                            pltpu.SemaphoreType.DMA((2,)), pltpu.SemaphoreType.DMA],
        ),
        compiler_params=_cparams(("arbitrary",)),
        name="moe_combine",
    )(tile_starts, base, w_flat, e_lines, r_lines, x_mid, gate2, ln2_g, ln2_b, ys)


def kernel(x, c, ctx, c_ctx, ada_w, ada_b, w_in, gla_wa_f, gla_ba_f, gla_wa_b, gla_ba_b, gla_norm_g,
           hy_conv_w, hy_conv_b, hy_flt_w1, hy_flt_b1, hy_flt_w2, hy_flt_b2, hy_flt_wout, hy_flt_freq,
           hy_bias_d, w_out, ln1_g, ln1_b, router_w, router_b, exp_w1, exp_b1, exp_w2, exp_b2, ln2_g, ln2_b):
    batch, seq_len, d = x.shape
    lyr = 0
    ch = HY_WIDTH

    n_rows = 8 * ((batch + 1 + 7) // 8)
    cvec = jnp.zeros((n_rows, d), F32).at[:batch].set(c).at[batch].set(c_ctx)
    mod = _modulation(cvec, ada_w[lyr], ada_b[lyr][None, :])
    part = lambda rows, i: rows[:, None, i * d:(i + 1) * d]
    mod_x = mod[:batch]
    mod_c = jnp.broadcast_to(mod[batch:batch + 1], (batch, 6 * d))

    w = w_in[lyr]
    o_q, o_k, o_v, o_g = 0, GLA_KEY, 2 * GLA_KEY, 2 * GLA_KEY + GLA_VAL
    o_a = o_g + GLA_VAL
    o_h = o_a + 2 * GLA_RANK
    a_cols = jnp.pad(w[:, o_a:o_h], ((0, 0), (0, A_PAD - 2 * GLA_RANK)))
    q_cols = w[:, o_q:o_k] * (GLA_DK ** -0.5)
    w_main = jnp.concatenate([q_cols, w[:, o_k:o_a], a_cols], axis=1).astype(BF16)
    w_ctx = jnp.concatenate([q_cols, w[:, o_k:o_g], a_cols], axis=1).astype(BF16)
    w_hy_t = w[:, o_h:].T.astype(BF16)
    conv_w = jnp.broadcast_to(hy_conv_w[lyr][:, :, None], (HY_CONV, 3 * ch, LANES))
    conv_b = jnp.broadcast_to(hy_conv_b[lyr][:, None], (3 * ch, LANES))

    ctx_qkva = _input_projection(ctx, part(mod_c, 0), part(mod_c, 1), w_ctx)
    q, k, v, g, a_low, u_t = _input_projection(x, part(mod_x, 0), part(mod_x, 1), w_main, w_hy_t, conv_w, conv_b)

    wa = jnp.zeros((2, A_PAD, GLA_KEY), F32)
    wa = wa.at[0, :GLA_RANK].set(gla_wa_f[lyr]).at[1, GLA_RANK:2 * GLA_RANK].set(gla_wa_b[lyr]).astype(BF16)
    ba = jnp.stack([gla_ba_f[lyr], gla_ba_b[lyr]])[:, None, :]
    o_f, o_b = _gla_scan(ctx_qkva, (q, k, v, a_low), wa, ba)

    r1 = 2 * seq_len // LANES
    consts = _dft_constants(r1)
    zz, tn_rows, rate = _filter_inputs(seq_len)
    w1_t = jnp.pad(hy_flt_w1[lyr], ((0, HY_EMB_PAD - HY_EMB), (0, 0))).T
    unit_tile = lambda vec: jnp.broadcast_to(vec[:, None], (HY_FH, LANES))
    wo_t = hy_flt_wout[lyr].reshape(HY_FH, HY_ORDER, 2, ch).transpose(1, 2, 3, 0)
    filt = _filter_mlp(zz, tn_rows, rate, w1_t, unit_tile(hy_flt_b1[lyr]), hy_flt_w2[lyr].T,
                       unit_tile(hy_flt_b2[lyr]), unit_tile(hy_flt_freq[lyr]), wo_t)
    spectra = _filter_spectra(filt.reshape(HY_ORDER, ch, r1, LANES), consts, r1)
    y_hy = _hyena_conv(u_t.reshape(batch // 2, 2, 3 * ch, r1 // 2, LANES), spectra, hy_bias_d[lyr], consts, r1)
    y_hy_t = y_hy.reshape(batch, ch, seq_len)

    wo = w_out[lyr].astype(BF16)
    wr = jnp.pad(router_w[lyr], ((0, 0), (0, LANES - N_EXPERTS))).astype(BF16)
    rb = jnp.pad(router_b[lyr], (0, LANES - N_EXPERTS), constant_values=NEG_BIG)[None, :]
    norm_g = jnp.tile(gla_norm_g[lyr], GLA_HEADS)[None, :]
    x_mid, h2, sel_e, sel_r, sel_w, cnt, tile_cs = _output_projection(
        o_f, o_b, g, y_hy_t, x, part(mod_x, 2), part(mod_x, 3), part(mod_x, 4),
        wo[:GLA_VAL], wo[GLA_VAL:], norm_g, ln1_g[lyr][None], ln1_b[lyr][None], wr, rb)

    t = batch * seq_len
    counts = cnt[0, :N_EXPERTS]
    tiles_e = (counts + (MOE_TILE - 1)) // MOE_TILE
    tile_end = jnp.cumsum(tiles_e)
    base = ((tile_end - tiles_e) * MOE_TILE).astype(jnp.int32)
    n_tiles = tile_end[-1:].astype(jnp.int32)
    max_tiles = t * TOP_K // MOE_TILE + N_EXPERTS + FFN_TILES_PER_STEP
    tile_ids = jnp.minimum(jnp.arange(max_tiles, dtype=jnp.int32), n_tiles[0] - 1)
    tile_expert = jnp.sum(tile_ids[:, None] >= tile_end[None, :], axis=1).astype(jnp.int32)
    as_lines = lambda sel: sel.reshape(t, LANES)[:, :TOP_K].reshape(t * TOP_K // LANES, LANES)
    e_lines, r_lines = as_lines(sel_e), as_lines(sel_r)
    w_flat = as_lines(sel_w).reshape(t * TOP_K)
    tile_starts = jnp.concatenate([tile_cs[:, 0, :], cnt[0:1]], axis=0)

    n_blk = 2 * D_EXPERT // SWIGLU_BLOCK
    b1p_e = exp_b1[lyr].reshape(N_EXPERTS, n_blk, LANES, 2).transpose(0, 1, 3, 2).reshape(N_EXPERTS, 1, 2 * D_EXPERT)
    xs, w1p_e, w2_e = _dispatch(h2.reshape(t * ROW_TILES, LANES), e_lines, r_lines, base, counts, n_tiles,
                                max_tiles * MOE_TILE, exp_w1[lyr], exp_w2[lyr])
    ys = _expert_ffn(xs, tile_expert, n_tiles, w1p_e, b1p_e, w2_e, exp_b2[lyr][:, None, :])
    out = _combine(ys, tile_starts, base, e_lines, r_lines, w_flat, x_mid.reshape(t, d), part(mod_x, 5),
                   ln2_g[lyr][None], ln2_b[lyr][None], seq_len // TOK_TILE)
    return out.reshape(batch, seq_len, d)
```

```python
import functools
import math

import numpy as np
import jax
import jax.numpy as jnp
from jax import lax
from jax.experimental import pallas as pl
from jax.experimental.pallas import tpu as pltpu

F32 = jnp.float32
BF16 = jnp.bfloat16

D_MODEL = 1024
DEPTH = 1
GRID_W = 64
LN_EPS = 1e-6
DEEPNORM_ALPHA = (2 * DEPTH) ** 0.25
GLA_HEADS = 4
GLA_DK = 64
GLA_DV = 128
GLA_KEY = GLA_HEADS * GLA_DK
GLA_VAL = GLA_HEADS * GLA_DV
GLA_RANK = 16
GLA_TAU = 16.0
HY_WIDTH = D_MODEL - GLA_VAL
HY_ORDER = 2
HY_CONV = 3
HY_EMB = 33
HY_FH = 64
HY_TARGET = 1e-2
HY_FAST = 0.3
HY_SLOW = 1.5
N_EXPERTS = 32
TOP_K = 4
D_EXPERT = D_MODEL
SWIGLU_ALPHA = 1.702
SWIGLU_LIMIT = 7.0

LANES = 128
VMEM_LIMIT = 56 * 1024 * 1024

TOK_TILE = 512
PROJ_SUB_TILE = 256
GLA_CHUNK = 128
GLA_BLOCK = 256
GLA_BATCH = 4
GLA_SAFE_EXPONENT = 80.0
A_PAD = LANES
HY_GROUP = 8
HY_CH_BLOCK = 16
FILT_TILE = 1024
HY_EMB_PAD = 40
MOE_TILE = 512
FFN_TILES_PER_STEP = 2
COMBINE_CHUNK = 32
ROW_TILES = D_MODEL // LANES


def _row_slab(n_rows, j):
    return pl.ds(j, n_rows, stride=ROW_TILES)


def _row_tile(r):
    return pl.ds(pl.multiple_of(r * ROW_TILES, ROW_TILES), ROW_TILES)
SWIGLU_BLOCK = 2 * LANES
NEG_BIG = -1e30


def _cparams(sem):
    return pltpu.CompilerParams(dimension_semantics=sem, vmem_limit_bytes=VMEM_LIMIT)


def _layer_norm(x):
    mu = jnp.mean(x, axis=-1, keepdims=True)
    xc = x - mu
    return xc * lax.rsqrt(jnp.mean(xc * xc, axis=-1, keepdims=True) + LN_EPS)


def _dot(a, b):
    return jnp.dot(a, b, preferred_element_type=F32)


def _dot_nt(a, b):
    return lax.dot_general(a, b, (((1,), (1,)), ((), ())), preferred_element_type=F32)


def _dot_tn(a, b):
    return lax.dot_general(a, b, (((0,), (0,)), ((), ())), preferred_element_type=F32)


def _mod_kernel(c_ref, w_ref, b_ref, o_ref):
    c = c_ref[...]
    s = c * jax.nn.sigmoid(c)
    o_ref[...] = _dot(s.astype(BF16), w_ref[...].astype(BF16)) + b_ref[...]


def _modulation(cvec, ada_w, ada_b):
    rows, d = cvec.shape
    n = ada_w.shape[1]
    tn = 2048
    return pl.pallas_call(
        _mod_kernel,
        out_shape=jax.ShapeDtypeStruct((rows, n), F32),
        grid=(n // tn,),
        in_specs=[
            pl.BlockSpec((rows, d), lambda j: (0, 0)),
            pl.BlockSpec((d, tn), lambda j: (0, j)),
            pl.BlockSpec((1, tn), lambda j: (0, j)),
        ],
        out_specs=pl.BlockSpec((rows, tn), lambda j: (0, j)),
        compiler_params=_cparams(("arbitrary",)),
        name="adaln_mod",
    )(cvec, ada_w, ada_b)


def _inproj_kernel(x_ref, sh_ref, sc_ref, wm_ref, *rest, with_hy):
    if with_hy:
        wh_ref, cw_ref, cb_ref, q_ref, k_ref, v_ref, g_ref, a_ref, u_ref = rest
    else:
        q_ref, k_ref, v_ref, a_ref = rest
    tt = x_ref.shape[1]
    n_sub = max(tt // PROJ_SUB_TILE, 1)
    sub = tt // n_sub
    for s in range(n_sub):
        rows = slice(s * sub, (s + 1) * sub)
        h = _layer_norm(x_ref[0, rows, :]) * (1.0 + sc_ref[0]) + sh_ref[0]
        hb = h.astype(BF16)
        u = _dot(hb, wm_ref[...])
        q_ref[0, rows, :] = u[:, :GLA_KEY].astype(BF16)
        k_ref[0, rows, :] = u[:, GLA_KEY:2 * GLA_KEY].astype(BF16)
        v_ref[0, rows, :] = u[:, 2 * GLA_KEY:2 * GLA_KEY + GLA_VAL].astype(BF16)
        off = 2 * GLA_KEY + GLA_VAL
        if with_hy:
            g_ref[0, rows, :] = u[:, off:off + GLA_VAL].astype(BF16)
            off += GLA_VAL
        a_ref[0, rows, :] = u[:, off:off + A_PAD]
        if with_hy:
            ut = _dot_nt(wh_ref[...], hb)
            lane = lax.broadcasted_iota(jnp.int32, (1, LANES), 1) % GRID_W
            not_first = (lane != 0).astype(F32)
            not_last = (lane != GRID_W - 1).astype(F32)
            for j in range(sub // LANES):
                c = ut[:, j * LANES:(j + 1) * LANES]
                left = pltpu.roll(c, 1, axis=1) * not_first
                right = pltpu.roll(c, LANES - 1, axis=1) * not_last
                y = cw_ref[0] * left + cw_ref[1] * c + cw_ref[2] * right + cb_ref[...]
                lanes = slice(s * sub + j * LANES, s * sub + (j + 1) * LANES)
                u_ref[0, :, lanes] = y.astype(BF16)


def _input_projection(x, shift, scale, w_main, w_hy_t=None, conv_w=None, conv_b=None):
    b, l, d = x.shape
    tt = min(TOK_TILE, l)
    with_hy = w_hy_t is not None
    nm = w_main.shape[1]
    tok = lambda width, dt: jax.ShapeDtypeStruct((b, l, width), dt)
    tok_spec = lambda width: pl.BlockSpec((1, tt, width), lambda i, j: (i, j, 0))
    in_specs = [
        pl.BlockSpec((1, tt, d), lambda i, j: (i, j, 0)),
        pl.BlockSpec((1, 1, d), lambda i, j: (i, 0, 0)),
        pl.BlockSpec((1, 1, d), lambda i, j: (i, 0, 0)),
        pl.BlockSpec((d, nm), lambda i, j: (0, 0)),
    ]
    args = [x, shift, scale, w_main]
    if with_hy:
        ch = w_hy_t.shape[0]
        in_specs += [
            pl.BlockSpec((ch, d), lambda i, j: (0, 0)),
            pl.BlockSpec((HY_CONV, ch, LANES), lambda i, j: (0, 0, 0)),
            pl.BlockSpec((ch, LANES), lambda i, j: (0, 0)),
        ]
        args += [w_hy_t, conv_w, conv_b]
        out_shape = [tok(GLA_KEY, BF16), tok(GLA_KEY, BF16), tok(GLA_VAL, BF16), tok(GLA_VAL, BF16),
                     tok(A_PAD, F32), jax.ShapeDtypeStruct((b, ch, l), BF16)]
        out_specs = [tok_spec(GLA_KEY), tok_spec(GLA_KEY), tok_spec(GLA_VAL), tok_spec(GLA_VAL),
                     tok_spec(A_PAD), pl.BlockSpec((1, ch, tt), lambda i, j: (i, 0, j))]
    else:
        out_shape = [tok(GLA_KEY, BF16), tok(GLA_KEY, BF16), tok(GLA_VAL, BF16), tok(A_PAD, F32)]
        out_specs = [tok_spec(GLA_KEY), tok_spec(GLA_KEY), tok_spec(GLA_VAL), tok_spec(A_PAD)]
    return pl.pallas_call(
        functools.partial(_inproj_kernel, with_hy=with_hy),
        out_shape=out_shape,
        grid=(b, l // tt),
        in_specs=in_specs,
        out_specs=out_specs,
        compiler_params=_cparams(("parallel", "arbitrary")),
        name="in_proj_hy" if with_hy else "in_proj_ctx",
    )(*args)


def _gla_kernel(*refs, n_ctx_blocks, chunk, n_batch, n_levels, safe_exponent):
    ins = [refs[0:8], refs[8:16]]
    wa_ref, ba_ref, cw_ref, cm_ref, hm_ref, bd_ref, dq_ref, dk_ref, lm_ref = refs[16:25]
    outs = refs[25:27]
    st_ref = refs[27]
    s = pl.program_id(1)
    blk = ins[0][4].shape[1]
    n_chunks = blk // chunk

    @pl.when(s == 0)
    def _():
        st_ref[...] = jnp.zeros_like(st_ref)

    def two_pass(m, g_hi, g_lo):
        return _dot(m, g_hi) + _dot(m, g_lo)

    is_ctx = s < n_ctx_blocks
    bdmask = bd_ref[...]
    chains = [(d, bi) for d in range(2) for bi in range(n_batch)]

    def pre_activation(d, bi):
        a = jnp.where(is_ctx, ins[d][3][bi], ins[d][7][bi])
        return _dot(a.astype(BF16), wa_ref[d]) + ba_ref[d]

    def advance(d, bi, z, single_reference):
        qc_ref, kc_ref, vc_ref, _, ql_ref, kl_ref, vl_ref, _ = ins[d]
        q = jnp.where(is_ctx, qc_ref[bi], ql_ref[bi]).astype(F32)
        k = jnp.where(is_ctx, kc_ref[bi], kl_ref[bi]).astype(F32)
        v = jnp.where(is_ctx, vc_ref[bi], vl_ref[bi])
        g = (jnp.minimum(z, 0.0) - jnp.log(1.0 + jnp.exp(-jnp.abs(z)))) * (1.0 / GLA_TAU)
        g_hi = g.astype(BF16)
        g_lo = (g - g_hi.astype(F32)).astype(BF16)
        cum = two_pass(cw_ref[d], g_hi, g_lo)
        edge = (lambda c: (c + 1) * chunk - 1) if d == 0 else (lambda c: c * chunk)
        tot = jnp.concatenate([jnp.broadcast_to(cum[edge(c):edge(c) + 1], (chunk, GLA_KEY))
                               for c in range(n_chunks)], axis=0)
        qs_all = (q * jnp.exp(cum)).astype(BF16)
        kst_all = (k * jnp.exp(tot - cum)).astype(BF16)
        dec_all = jnp.exp(tot)

        if single_reference:
            ks_all = (k * jnp.exp(-cum)).astype(BF16)
            cmask = cm_ref[d]
            rows_out = []
            for c in range(n_chunks):
                rows = slice(c * chunk, (c + 1) * chunk)
                parts = []
                for h in range(GLA_HEADS):
                    a_h = _dot_nt(qs_all[rows] * hm_ref[h], ks_all[rows]) * cmask
                    parts.append(_dot(a_h.astype(BF16), v[rows, h * GLA_DV:(h + 1) * GLA_DV]))
                rows_out.append(jnp.concatenate(parts, axis=1))
            o_intra = jnp.concatenate(rows_out, axis=0)
        else:
            def level(lv, acc):
                ql = (q * jnp.exp(two_pass(dq_ref[d, lv], g_hi, g_lo))).astype(BF16)
                kl = (k * jnp.exp(two_pass(dk_ref[d, lv], g_hi, g_lo))).astype(BF16)
                msk = lm_ref[d, lv]
                parts = []
                for h in range(GLA_HEADS):
                    a_h = _dot_nt(ql * hm_ref[h], kl) * msk
                    parts.append(_dot(a_h.astype(BF16), v[:, h * GLA_DV:(h + 1) * GLA_DV]))
                return acc + jnp.concatenate(parts, axis=1)
            o_intra = lax.fori_loop(0, n_levels, level, jnp.zeros((blk, GLA_VAL), F32))

        for n in range(n_chunks):
            ci = n if d == 0 else n_chunks - 1 - n
            rows = slice(ci * chunk, (ci + 1) * chunk)
            o = _dot_nt(qs_all[rows], st_ref[d, bi].astype(BF16)) + o_intra[rows]
            outs[d][bi, rows, :] = o.astype(outs[d].dtype)
            st_ref[d, bi] = (st_ref[d, bi] * dec_all[ci * chunk:ci * chunk + 1]
                             + _dot_tn(v[rows], kst_all[rows]) * bdmask)

    zs = [pre_activation(d, bi) for d, bi in chains]
    z_low = zs[0]
    for z in zs[1:]:
        z_low = jnp.minimum(z_low, z)
    bound = (jnp.maximum(-jnp.min(z_low), 0.0) + math.log(2.0)) * (chunk / GLA_TAU)

    def step(single_reference):
        def run():
            for (d, bi), z in zip(chains, zs):
                advance(d, bi, z, single_reference)
        return run

    lax.cond(bound < safe_exponent, step(True), step(False))


def _gla_level_tables(blk, chunk):
    sizes = []
    s = chunk // 2
    while s >= 1:
        sizes.append(s)
        s //= 2
    n_lv = len(sizes) + 1
    dq = np.zeros((2, n_lv, blk, blk), np.float32)
    dk = np.zeros((2, n_lv, blk, blk), np.float32)
    lm = np.zeros((2, n_lv, blk, blk), np.float32)
    idx = np.arange(blk)
    for lv, s in enumerate(sizes):
        blk_id = idx // (2 * s)
        right = (idx % (2 * s)) >= s
        b = blk_id * 2 * s + s
        same = blk_id[:, None] == blk_id[None, :]
        m = idx[None, :]
        dq[0, lv] = (right[:, None] & (m >= b[:, None]) & (m <= idx[:, None]))
        dk[0, lv] = (~right[:, None] & (m > idx[:, None]) & (m < b[:, None]))
        lm[0, lv] = same & right[:, None] & ~right[None, :]
        dq[1, lv] = (~right[:, None] & (m >= idx[:, None]) & (m < b[:, None]))
        dk[1, lv] = (right[:, None] & (m >= b[:, None]) & (m < idx[:, None]))
        lm[1, lv] = same & ~right[:, None] & right[None, :]
    lm[:, n_lv - 1] = np.eye(blk, dtype=np.float32)
    return dq, dk, lm, n_lv


def _gla_scan(ctx_qkva, lat_qkva, wa, ba, chunk=GLA_CHUNK, blk=GLA_BLOCK, safe_exponent=GLA_SAFE_EXPONENT):
    qc, kc, vc, ac = ctx_qkva
    ql, kl, vl, al = lat_qkva
    b, l, _ = ql.shape
    ctx_len = qc.shape[1]
    n_ctx = ctx_len // blk
    n_lat = l // blk
    c = chunk
    idx = np.arange(c)
    tri = np.stack([idx[:, None] >= idx[None, :], idx[:, None] <= idx[None, :]]).astype(np.float32)
    cw = np.stack([np.kron(np.eye(blk // c, dtype=np.float32), tri[dd]) for dd in range(2)])
    dq, dk, lm, n_levels = _gla_level_tables(blk, c)
    hmask = np.zeros((GLA_HEADS, 1, GLA_KEY), np.float32)
    for h in range(GLA_HEADS):
        hmask[h, 0, h * GLA_DK:(h + 1) * GLA_DK] = 1.0
    bd = (np.arange(GLA_VAL)[:, None] // GLA_DV == np.arange(GLA_KEY)[None, :] // GLA_DK).astype(np.float32)

    nb = GLA_BATCH if b % GLA_BATCH == 0 else 1

    def lat_block(d):
        def f(i, s):
            t = jnp.maximum(s - n_ctx, 0)
            return (i, t if d == 0 else n_lat - 1 - t, 0)
        return f

    def ctx_block(d):
        def f(i, s):
            t = jnp.minimum(s, n_ctx - 1)
            return (i, t if d == 0 else n_ctx - 1 - t, 0)
        return f

    widths = (GLA_KEY, GLA_KEY, GLA_VAL, A_PAD)
    dir_specs = lambda d: ([pl.BlockSpec((nb, blk, w), ctx_block(d)) for w in widths]
                           + [pl.BlockSpec((nb, blk, w), lat_block(d)) for w in widths])
    consts = [wa, ba, jnp.asarray(cw, BF16), jnp.asarray(tri, F32), jnp.asarray(hmask, BF16), jnp.asarray(bd, F32),
              jnp.asarray(dq, BF16), jnp.asarray(dk, BF16), jnp.asarray(lm, F32)]
    full = lambda arr: pl.BlockSpec(arr.shape, lambda i, s: (0,) * arr.ndim)
    o_sds = jax.ShapeDtypeStruct((b, l, GLA_VAL), BF16)
    dir_args = [qc, kc, vc, ac, ql, kl, vl, al]
    return pl.pallas_call(
        functools.partial(_gla_kernel, n_ctx_blocks=n_ctx, chunk=c, n_batch=nb, n_levels=n_levels,
                          safe_exponent=safe_exponent),
        out_shape=[o_sds, o_sds],
        grid=(b // nb, n_ctx + n_lat),
        in_specs=dir_specs(0) + dir_specs(1) + [full(arr) for arr in consts],
        out_specs=[pl.BlockSpec((nb, blk, GLA_VAL), lat_block(0)), pl.BlockSpec((nb, blk, GLA_VAL), lat_block(1))],
        scratch_shapes=[pltpu.VMEM((2, nb, GLA_VAL, GLA_KEY), F32)],
        compiler_params=_cparams(("parallel", "arbitrary")),
        name="gla_scan",
    )(*dir_args, *dir_args, *consts)


def _dft_constants(r1):
    n = r1 * LANES
    h = r1 // 2
    k1 = np.arange(r1)
    f1 = np.exp(-2j * np.pi * np.outer(k1, k1) / r1)
    f2 = np.exp(-2j * np.pi * np.outer(np.arange(LANES), np.arange(LANES)) / LANES)
    tw = np.exp(-2j * np.pi * np.outer(k1, np.arange(LANES)) / n)
    fa_c = np.block([[f1.real[:, :h], -f1.imag[:, :h]], [f1.imag[:, :h], f1.real[:, :h]]])
    fa_r = np.concatenate([f1.real, f1.imag], axis=0)
    gc = np.block([[f2.real, f2.imag], [-f2.imag, f2.real]])
    gci = np.block([[f2.real, -f2.imag], [f2.imag, f2.real]])
    fai = np.block([[f1.real[:h], f1.imag[:h]], [-f1.imag[:h], f1.real[:h]]]) / n
    tw_lane = np.tile(tw, (1, HY_GROUP))
    tw_row = np.tile(tw, (HY_GROUP, 1))
    f = lambda a: jnp.asarray(a, F32)
    return dict(fa_c=f(fa_c), fa_r=f(fa_r), gc=f(gc), gci=f(gci), fai=f(fai),
                twl_r=f(tw_lane.real), twl_i=f(tw_lane.imag), twr_r=f(tw_row.real), twr_i=f(tw_row.imag))


def _fwd_dft(rhs, fa, gc, twl_r, twl_i, r1):
    a = _dot(fa, rhs)
    ar, ai = a[:r1], a[r1:]
    br = (ar * twl_r - ai * twl_i).astype(BF16)
    bi = (ar * twl_i + ai * twl_r).astype(BF16)
    lhs = jnp.concatenate(
        [jnp.concatenate([br[:, c * LANES:(c + 1) * LANES], bi[:, c * LANES:(c + 1) * LANES]], axis=1)
         for c in range(HY_GROUP)], axis=0)
    return _dot(lhs, gc)


def _inv_dft(yr, yi, gci, fai, twr_r, twr_i, r1):
    lhs = jnp.concatenate([yr, yi], axis=1).astype(BF16)
    c = _dot(lhs, gci)
    cr, ci = c[:, :LANES], c[:, LANES:]
    dr = (cr * twr_r + ci * twr_i).astype(BF16)
    di = (ci * twr_r - cr * twr_i).astype(BF16)
    rhs = jnp.concatenate(
        [jnp.concatenate([dr[g * r1:(g + 1) * r1], di[g * r1:(g + 1) * r1]], axis=0)
         for g in range(HY_GROUP)], axis=1)
    return _dot(fai, rhs)


def _filter_mlp_kernel(zt_ref, tn_ref, rate_ref, w1_ref, b1_ref, w2_ref, b2_ref, fr_ref, wo_ref, o_ref):
    hp = lax.Precision.HIGHEST
    lt = tn_ref.shape[1]
    lanes = lambda ref: jnp.concatenate([ref[...]] * (lt // LANES), axis=1)
    fr = lanes(fr_ref)
    hid = jnp.sin(fr * (jnp.dot(w1_ref[...], zt_ref[...], precision=hp, preferred_element_type=F32) + lanes(b1_ref)))
    hid = jnp.sin(fr * (jnp.dot(w2_ref[...], hid, precision=hp, preferred_element_type=F32) + lanes(b2_ref)))
    window = jnp.exp(-tn_ref[0:1, :] * lanes(rate_ref)) * tn_ref[1:2, :]
    for o in range(HY_ORDER):
        o_ref[o] = jnp.dot(wo_ref[o, 0], hid, precision=hp, preferred_element_type=F32) * window


def _filter_mlp(z_t, tn_rows, rate, w1_t, b1, w2_t, b2, freq, wo_t):
    emb, n2l = z_t.shape
    l = n2l // 2
    lt = min(FILT_TILE, l)
    nt = l // lt
    ch = rate.shape[0]
    full = lambda shape: pl.BlockSpec(shape, lambda d, j: (0,) * len(shape))
    return pl.pallas_call(
        _filter_mlp_kernel,
        out_shape=jax.ShapeDtypeStruct((HY_ORDER, ch, n2l), F32),
        grid=(2, nt),
        in_specs=[
            pl.BlockSpec((emb, lt), lambda d, j: (0, d * nt + j)),
            pl.BlockSpec((8, lt), lambda d, j: (0, d * nt + j)),
            full((ch, LANES)),
            full((HY_FH, emb)), full((HY_FH, LANES)), full((HY_FH, HY_FH)), full((HY_FH, LANES)),
            full((HY_FH, LANES)),
            pl.BlockSpec((HY_ORDER, 1, ch, HY_FH), lambda d, j: (0, d, 0, 0)),
        ],
        out_specs=pl.BlockSpec((HY_ORDER, ch, lt), lambda d, j: (0, 0, d * nt + j)),
        compiler_params=_cparams(("arbitrary", "arbitrary")),
        name="hyena_filter_mlp",
    )(z_t, tn_rows, rate, w1_t, b1, w2_t, b2, freq, wo_t)


def _filter_fft_kernel(f_ref, fa_ref, gc_ref, twl_r_ref, twl_i_ref, h_ref, *, r1):
    fa = fa_ref[...].astype(BF16)
    gc = gc_ref[...].astype(BF16)
    nc = f_ref.shape[1]
    for g0 in range(0, nc, HY_GROUP):
        rhs = jnp.concatenate([f_ref[0, g0 + c].astype(BF16) for c in range(HY_GROUP)], axis=1)
        x = _fwd_dft(rhs, fa, gc, twl_r_ref[...], twl_i_ref[...], r1)
        for c in range(HY_GROUP):
            h_ref[0, g0 + c] = x[c * r1:(c + 1) * r1].astype(h_ref.dtype)


def _filter_spectra(filt, consts, r1):
    order, ch = filt.shape[:2]
    nc = HY_CH_BLOCK
    full = lambda a: pl.BlockSpec(a.shape, lambda o, j: (0,) * a.ndim)
    cs = [consts["fa_r"], consts["gc"], consts["twl_r"], consts["twl_i"]]
    return pl.pallas_call(
        functools.partial(_filter_fft_kernel, r1=r1),
        out_shape=jax.ShapeDtypeStruct((order, ch, r1, 2 * LANES), BF16),
        grid=(order, ch // nc),
        in_specs=[pl.BlockSpec((1, nc, r1, LANES), lambda o, j: (o, j, 0, 0))] + [full(a) for a in cs],
        out_specs=pl.BlockSpec((1, nc, r1, 2 * LANES), lambda o, j: (o, j, 0, 0)),
        compiler_params=_cparams(("arbitrary", "arbitrary")),
        name="hyena_filter_fft",
    )(filt, *cs)


def _hyena_kernel(dbias_ref, v_ref, x1_ref, x2_ref, h_ref, fa_ref, gc_ref, gci_ref, fai_ref,
                  twl_r_ref, twl_i_ref, twr_r_ref, twr_i_ref, y_ref, *, r1):
    fa = fa_ref[...].astype(BF16)
    gc = gc_ref[...].astype(BF16)
    gci = gci_ref[...].astype(BF16)
    fai = fai_ref[...].astype(BF16)
    twl_r, twl_i = twl_r_ref[...], twl_i_ref[...]
    twr_r, twr_i = twr_r_ref[...], twr_i_ref[...]
    nc = v_ref.shape[2]
    half = r1 // 2
    c_base = pl.program_id(0) * nc

    def conv(sig, sig_b, order, g0):
        rhs = jnp.concatenate(
            [jnp.concatenate([sig_b[c][0], sig_b[c][1]], axis=0) for c in range(HY_GROUP)], axis=1)
        x = _fwd_dft(rhs, fa, gc, twl_r, twl_i, r1)
        xr, xi = x[:, :LANES], x[:, LANES:]
        hh = jnp.concatenate([h_ref[order, g0 + c] for c in range(HY_GROUP)], axis=0).astype(F32)
        hr, hi = hh[:, :LANES], hh[:, LANES:]
        y = _inv_dft(xr * hr - xi * hi, xr * hi + xi * hr, gci, fai, twr_r, twr_i, r1)
        out = []
        for c in range(HY_GROUP):
            dcoef = dbias_ref[order, c_base + g0 + c]
            yc = y[:, c * LANES:(c + 1) * LANES]
            out.append([yc[:half] + dcoef * sig[c][0], yc[half:] + dcoef * sig[c][1]])
        return out

    for g0 in range(0, nc, HY_GROUP):
        v_b = [[v_ref[0, b, g0 + c] for b in range(2)] for c in range(HY_GROUP)]
        v = [[t.astype(F32) for t in pair] for pair in v_b]
        y1 = conv(v, v_b, 0, g0)
        z = [[x1_ref[0, b, g0 + c].astype(F32) * y1[c][b] for b in range(2)] for c in range(HY_GROUP)]
        z_b = [[t.astype(BF16) for t in pair] for pair in z]
        y2 = conv(z, z_b, 1, g0)
        for c in range(HY_GROUP):
            for b in range(2):
                y_ref[0, b, g0 + c] = (x2_ref[0, b, g0 + c].astype(F32) * y2[c][b]).astype(y_ref.dtype)


def _hyena_conv(u_t, spectra, d_bias, consts, r1):
    bp, _, ch3, half, _ = u_t.shape
    ch = ch3 // 3
    nc = HY_CH_BLOCK
    nblk = ch // nc
    names = ["fa_c", "gc", "gci", "fai", "twl_r", "twl_i", "twr_r", "twr_i"]
    cs = [consts[k] for k in names]
    full = lambda a: pl.BlockSpec(a.shape, lambda j, p: (0,) * a.ndim)
    part = lambda k: pl.BlockSpec((1, 2, nc, half, LANES), lambda j, p: (p, 0, k * nblk + j, 0, 0))
    return pl.pallas_call(
        functools.partial(_hyena_kernel, r1=r1),
        out_shape=jax.ShapeDtypeStruct((bp, 2, ch, half, LANES), BF16),
        grid=(nblk, bp),
        in_specs=[pl.BlockSpec(memory_space=pltpu.SMEM), part(0), part(1), part(2),
                  pl.BlockSpec((HY_ORDER, nc, r1, 2 * LANES), lambda j, p: (0, j, 0, 0))] + [full(a) for a in cs],
        out_specs=pl.BlockSpec((1, 2, nc, half, LANES), lambda j, p: (p, 0, j, 0, 0)),
        compiler_params=_cparams(("arbitrary", "arbitrary")),
        name="hyena_conv",
    )(d_bias, u_t, u_t, u_t, spectra, *cs)


def _filter_inputs(l):
    n = jnp.arange(2 * l, dtype=jnp.int32)
    t = jnp.where(n < l, n, 2 * l - n).astype(F32)
    valid = (n != l).astype(F32)
    t_norm = t / (l - 1)
    bands = (HY_EMB - 1) // 2
    f = jnp.linspace(1e-4, bands - 1, bands, dtype=F32)
    ang = (2.0 * math.pi * t / l)[:, None] * f[None, :]
    z = jnp.concatenate([t_norm[:, None], jnp.cos(ang), -jnp.sin(ang)], -1)
    zz = jnp.pad(z, ((0, 0), (0, HY_EMB_PAD - HY_EMB))).T
    tn_rows = jnp.zeros((8, 2 * l), F32).at[0].set(t_norm).at[1].set(valid)
    deltas = jnp.linspace(math.log(HY_TARGET) / HY_SLOW, math.log(HY_TARGET) / HY_FAST, HY_WIDTH, dtype=F32)
    rate = jnp.broadcast_to(jnp.abs(deltas)[:, None], (HY_WIDTH, LANES))
    return zz, tn_rows, rate


def _outproj_kernel(of_ref, ob_ref, g_ref, yh_ref, x_ref, gate_ref, sh_ref, sc_ref,
                    wg_ref, wh_ref, ng_ref, l1g_ref, l1b_ref, wr_ref, rb_ref, ltri_ref,
                    xm_ref, h2_ref, se_ref, sr_ref, sw_ref, cnt_ref, cs_ref, carry_ref):
    @pl.when((pl.program_id(0) == 0) & (pl.program_id(1) == 0))
    def _():
        carry_ref[...] = jnp.zeros_like(carry_ref)

    cs_ref[0] = jnp.broadcast_to(carry_ref[0:1, :], cs_ref.shape[1:]).astype(jnp.int32)

    tt = x_ref.shape[1]
    sub = ltri_ref.shape[0]
    for s in range(tt // sub):
        rows = slice(s * sub, (s + 1) * sub)
        o = of_ref[0, rows, :].astype(F32) + ob_ref[0, rows, :].astype(F32)
        g = g_ref[0, rows, :].astype(F32)
        parts = []
        for h in range(GLA_HEADS):
            oh = o[:, h * GLA_DV:(h + 1) * GLA_DV]
            parts.append(oh * lax.rsqrt(jnp.mean(oh * oh, axis=-1, keepdims=True) + LN_EPS))
        y_gla = jnp.concatenate(parts, axis=1) * ng_ref[...] * (g * jax.nn.sigmoid(g))
        f = _dot(y_gla.astype(BF16), wg_ref[...]) + _dot_tn(yh_ref[0, :, rows], wh_ref[...])
        x_mid = (_layer_norm(DEEPNORM_ALPHA * x_ref[0, rows, :] + gate_ref[0] * f) * l1g_ref[...]
                 + l1b_ref[...])
        xm_ref[0, rows, :] = x_mid
        h2 = _layer_norm(x_mid) * (1.0 + sc_ref[0]) + sh_ref[0]
        for j in range(ROW_TILES):
            h2_ref[0, pl.ds(s * sub * ROW_TILES + j, sub, stride=ROW_TILES), :] = h2[:, j * LANES:(j + 1) * LANES]

        logits = _dot(h2.astype(BF16), wr_ref[...]) + rb_ref[...]
        lane = lax.broadcasted_iota(jnp.int32, logits.shape, 1).astype(F32)
        hits, idxs, exps = [], [], []
        m0 = None
        for _ in range(TOP_K):
            m = jnp.max(logits, axis=-1, keepdims=True)
            idx = jnp.min(jnp.where(logits == m, lane, float(LANES)), axis=-1, keepdims=True)
            hit = lane == idx
            m0 = m if m0 is None else m0
            hits.append(hit)
            idxs.append(idx)
            exps.append(jnp.exp(m - m0))
            logits = jnp.where(hit, NEG_BIG, logits)
        denom = exps[0]
        sel = jnp.where(hits[0], 1.0, 0.0)
        for kk in range(1, TOP_K):
            denom = denom + exps[kk]
            sel = sel + jnp.where(hits[kk], 1.0, 0.0)
        rank_all = _dot(ltri_ref[...], sel.astype(BF16)) + carry_ref[0:1, :]
        carry_ref[0:1, :] = carry_ref[0:1, :] + jnp.sum(sel, axis=0, keepdims=True)
        se = jnp.zeros(logits.shape, F32)
        sr = jnp.zeros(logits.shape, F32)
        sw = jnp.zeros(logits.shape, F32)
        for kk in range(TOP_K):
            rk = jnp.sum(jnp.where(hits[kk], rank_all, 0.0), axis=-1, keepdims=True)
            col = lane == float(kk)
            se = jnp.where(col, idxs[kk], se)
            sr = jnp.where(col, rk, sr)
            sw = jnp.where(col, exps[kk] / denom, sw)
        se_ref[0, rows, :] = se.astype(jnp.int32)
        sr_ref[0, rows, :] = sr.astype(jnp.int32)
        sw_ref[0, rows, :] = sw
    cnt_ref[...] = jnp.broadcast_to(carry_ref[0:1, :], cnt_ref.shape).astype(jnp.int32)


def _output_projection(o_f, o_b, g, y_hy_t, x, gate1, shift2, scale2, w_gla, w_hy, norm_g, ln1_g, ln1_b, wr, rb):
    b, l, d = x.shape
    tt = TOK_TILE
    tok = lambda w: pl.BlockSpec((1, tt, w), lambda i, j: (i, j, 0))
    row = lambda: pl.BlockSpec((1, 1, d), lambda i, j: (i, 0, 0))
    full = lambda a: pl.BlockSpec(a.shape, lambda i, j: (0,) * a.ndim)
    sub = tt
    ltri = jnp.asarray(np.tril(np.ones((sub, sub), np.float32), -1), BF16)
    consts = [w_gla, w_hy, norm_g, ln1_g, ln1_b, wr, rb, ltri]
    lane_i = jax.ShapeDtypeStruct((b, l, LANES), jnp.int32)
    return pl.pallas_call(
        _outproj_kernel,
        out_shape=[jax.ShapeDtypeStruct((b, l, d), F32), jax.ShapeDtypeStruct((b, l * ROW_TILES, LANES), F32),
                   lane_i, lane_i, jax.ShapeDtypeStruct((b, l, LANES), F32),
                   jax.ShapeDtypeStruct((8, LANES), jnp.int32),
                   jax.ShapeDtypeStruct((b * (l // tt), 8, LANES), jnp.int32)],
        grid=(b, l // tt),
        in_specs=[
            tok(GLA_VAL), tok(GLA_VAL), tok(GLA_VAL),
            pl.BlockSpec((1, HY_WIDTH, tt), lambda i, j: (i, 0, j)),
            tok(d), row(), row(), row(),
        ] + [full(a) for a in consts],
        out_specs=[tok(d), pl.BlockSpec((1, tt * ROW_TILES, LANES), lambda i, j: (i, j, 0)),
                   tok(LANES), tok(LANES), tok(LANES),
                   pl.BlockSpec((8, LANES), lambda i, j: (0, 0)),
                   pl.BlockSpec((1, 8, LANES), lambda i, j: (i * (l // tt) + j, 0, 0))],
        scratch_shapes=[pltpu.VMEM((8, LANES), F32)],
        compiler_params=_cparams(("arbitrary", "arbitrary")),
        name="out_proj_router",
    )(o_f, o_b, g, y_hy_t, x, gate1, shift2, scale2, *consts)


def _expert_weight_layout(w1_ref, w2_ref, p_ref, w1o_ref, w2o_ref):
    p = p_ref[...]
    for j in range(w1_ref.shape[2] // SWIGLU_BLOCK):
        cols = slice(j * SWIGLU_BLOCK, (j + 1) * SWIGLU_BLOCK)
        w1o_ref[0, :, cols] = _dot(w1_ref[0, :, cols].astype(BF16), p).astype(BF16)
    w2o_ref[0] = w2_ref[0].astype(BF16)


def _dispatch_kernel(base_ref, cnt_ref, nt_ref, ev_ref, rv_ref, h2_ref, w1_ref, w2_ref, p_ref,
                     xs_hbm, w1o_ref, w2o_ref, zrow_ref, zblk_ref, slotv_ref, slots_ref, sem, zsem, ssem):
    tt = h2_ref.shape[0] // ROW_TILES
    ev = ev_ref[...]
    slots = rv_ref[...]
    for e in range(N_EXPERTS):
        slots = slots + jnp.where(ev == e, base_ref[e], 0)
    slotv_ref[...] = slots
    to_smem = pltpu.make_async_copy(slotv_ref, slots_ref, ssem)
    to_smem.start()
    to_smem.wait()

    per_line = LANES // TOP_K

    def line(li, carry):
        for u in range(per_line):
            for kk in range(TOP_K):
                pltpu.make_async_copy(h2_ref.at[_row_tile(li * per_line + u)],
                                      xs_hbm.at[_row_tile(slots_ref[li, u * TOP_K + kk])],
                                      sem).start(priority=kk % 2)
        return carry

    lax.fori_loop(0, tt // per_line, line, 0)
    _expert_weight_layout(w1_ref, w2_ref, p_ref, w1o_ref, w2o_ref)
    for _ in range(TOP_K):
        pltpu.make_async_copy(h2_ref, xs_hbm.at[pl.ds(0, tt * ROW_TILES)], sem).wait()

    @pl.when(pl.program_id(0) == pl.num_programs(0) - 1)
    def _():
        zrow_ref[...] = jnp.zeros_like(zrow_ref)

        def per_expert(e, carry):
            n = cnt_ref[e]
            end = ((n + (MOE_TILE - 1)) // MOE_TILE) * MOE_TILE

            def fill(r, c):
                pltpu.make_async_copy(zrow_ref, xs_hbm.at[_row_tile(base_ref[e] + r)], zsem).start()
                return c

            def drain(r, c):
                pltpu.make_async_copy(zrow_ref, xs_hbm.at[_row_tile(0)], zsem).wait()
                return c

            lax.fori_loop(n, end, fill, 0)
            lax.fori_loop(n, end, drain, 0)
            return carry

        lax.fori_loop(0, N_EXPERTS, per_expert, 0)

        zblk_ref[...] = jnp.zeros_like(zblk_ref)
        tile_rows = MOE_TILE * ROW_TILES
        n_all = xs_hbm.shape[0] // tile_rows

        def tile_copy(ti):
            row0 = pl.multiple_of(ti * tile_rows, tile_rows)
            return pltpu.make_async_copy(zblk_ref, xs_hbm.at[pl.ds(row0, tile_rows)], zsem)

        def fill_tile(ti, c):
            tile_copy(ti).start()
            return c

        def drain_tile(ti, c):
            tile_copy(ti).wait()
            return c

        lax.fori_loop(nt_ref[0], n_all, fill_tile, 0)
        lax.fori_loop(nt_ref[0], n_all, drain_tile, 0)


def _dispatch(h2, e_lines, r_lines, base, counts, n_tiles, n_slots, w1, w2):
    t = h2.shape[0] // ROW_TILES
    ne, d, f2 = w1.shape
    assert t % (ne * 2 * LANES // TOP_K) == 0, "one dispatch step per expert, whole index lines per step"
    tt = t // ne
    lines = tt * TOP_K // LANES
    line_blk = lambda: pl.BlockSpec((lines, LANES), lambda i, *_: (i, 0))
    src = np.concatenate([np.arange(0, SWIGLU_BLOCK, 2), np.arange(1, SWIGLU_BLOCK, 2)])
    perm = np.zeros((SWIGLU_BLOCK, SWIGLU_BLOCK), np.float32)
    perm[src, np.arange(SWIGLU_BLOCK)] = 1.0
    exp_blk = lambda shape: pl.BlockSpec((1,) + shape, lambda i, *_: (i, 0, 0))
    return pl.pallas_call(
        _dispatch_kernel,
        out_shape=[jax.ShapeDtypeStruct((n_slots * ROW_TILES, LANES), F32),
                   jax.ShapeDtypeStruct(w1.shape, BF16), jax.ShapeDtypeStruct(w2.shape, BF16)],
        grid_spec=pltpu.PrefetchScalarGridSpec(
            num_scalar_prefetch=3,
            grid=(ne,),
            in_specs=[line_blk(), line_blk(),
                      pl.BlockSpec((tt * ROW_TILES, LANES), lambda i, *_: (i, 0)),
                      exp_blk((d, f2)), exp_blk(w2.shape[1:]),
                      pl.BlockSpec((SWIGLU_BLOCK, SWIGLU_BLOCK), lambda i, *_: (0, 0))],
            out_specs=[pl.BlockSpec(memory_space=pl.ANY), exp_blk((d, f2)), exp_blk(w2.shape[1:])],
            scratch_shapes=[pltpu.VMEM((ROW_TILES, LANES), F32), pltpu.VMEM((MOE_TILE * ROW_TILES, LANES), F32),
                            pltpu.VMEM((lines, LANES), jnp.int32), pltpu.SMEM((lines, LANES), jnp.int32),
                            pltpu.SemaphoreType.DMA, pltpu.SemaphoreType.DMA, pltpu.SemaphoreType.DMA],
        ),
        compiler_params=_cparams(("arbitrary",)),
        name="moe_dispatch",
    )(base, counts, n_tiles, e_lines, r_lines, h2, w1, w2, jnp.asarray(perm, BF16))


def _ffn_kernel(te_ref, nt_ref, xs_ref, *refs):
    ys_ref = refs[-1]
    tm = MOE_TILE
    rows = tm * ROW_TILES
    n_here = jnp.clip(nt_ref[0] - pl.program_id(0) * FFN_TILES_PER_STEP, 0, FFN_TILES_PER_STEP)

    def tile(t):
        w1_ref, b1_ref, w2_ref, b2_ref = refs[4 * t:4 * t + 4]
        x = jnp.concatenate([xs_ref[pl.ds(t * rows + j, tm, stride=ROW_TILES), :] for j in range(ROW_TILES)], axis=1)
        hid = _dot(x.astype(BF16), w1_ref[0]) + b1_ref[0]
        acts = []
        for j in range(hid.shape[1] // SWIGLU_BLOCK):
            glu = jnp.minimum(hid[:, j * SWIGLU_BLOCK:j * SWIGLU_BLOCK + LANES], SWIGLU_LIMIT)
            lin = jnp.clip(hid[:, j * SWIGLU_BLOCK + LANES:(j + 1) * SWIGLU_BLOCK], -SWIGLU_LIMIT, SWIGLU_LIMIT)
            acts.append((glu * jax.nn.sigmoid(SWIGLU_ALPHA * glu) * (lin + 1.0)).astype(BF16))
        y = _dot(jnp.concatenate(acts, axis=1), w2_ref[0]) + b2_ref[0]
        for j in range(ROW_TILES):
            ys_ref[pl.ds(t * rows + j, tm, stride=ROW_TILES), :] = y[:, j * LANES:(j + 1) * LANES]

    for n_valid in range(FFN_TILES_PER_STEP + 1):
        @pl.when(n_here == n_valid)
        def _(n_valid=n_valid):
            for t in range(n_valid):
                tile(t)
            for t in range(n_valid, FFN_TILES_PER_STEP):
                ys_ref[t * rows:(t + 1) * rows, :] = jnp.zeros((rows, LANES), F32)


def _expert_ffn(xs, tile_expert, n_tiles, w1p, b1p, w2b, b2):
    n_slots = xs.shape[0] // ROW_TILES
    d = w1p.shape[1]
    tps = FFN_TILES_PER_STEP
    tm = MOE_TILE
    f2 = w1p.shape[2]
    assert n_slots % (tm * tps) == 0
    rows_blk = (tps * tm * ROW_TILES, LANES)
    weights = []
    for t in range(tps):
        exp_blk = lambda i, te, nt, t=t: (te[i * tps + t], 0, 0)
        weights += [pl.BlockSpec((1, d, f2), exp_blk), pl.BlockSpec((1, 1, f2), exp_blk),
                    pl.BlockSpec((1, f2 // 2, d), exp_blk), pl.BlockSpec((1, 1, d), exp_blk)]
    return pl.pallas_call(
        _ffn_kernel,
        out_shape=jax.ShapeDtypeStruct(xs.shape, F32),
        grid_spec=pltpu.PrefetchScalarGridSpec(
            num_scalar_prefetch=2,
            grid=(n_slots // (tm * tps),),
            in_specs=[
                pl.BlockSpec(rows_blk, lambda i, te, nt: (jnp.minimum(i, (nt[0] - 1) // tps), 0)),
            ] + weights,
            out_specs=pl.BlockSpec(rows_blk, lambda i, te, nt: (i, 0)),
        ),
        compiler_params=_cparams(("arbitrary",)),
        name="moe_expert_ffn",
    )(tile_expert, n_tiles, xs, *([w1p, b1p, w2b, b2] * tps))


def _combine_kernel(cs_ref, base_ref, w_ref, ev_ref, rv_ref, xm_ref, gate_ref, l2g_ref, l2b_ref,
                    ys_hbm, o_ref, buf_ref, acc_ref, rowv_ref, rows_ref, seg_ref, nch_ref, sem, rsem):
    i = pl.program_id(0)
    tt = xm_ref.shape[0]
    cur = lax.rem(i, 2)
    chunk_rows = COMBINE_CHUNK * ROW_TILES

    def chunk_copy(src_row, dst_row, sl):
        return pltpu.make_async_copy(ys_hbm.at[pl.ds(pl.multiple_of(src_row * ROW_TILES, ROW_TILES), chunk_rows)],
                                     buf_ref.at[sl, pl.ds(pl.multiple_of(dst_row * ROW_TILES, ROW_TILES), chunk_rows)],
                                     sem.at[sl])

    def issue(tile, sl):
        def per_expert(e, off):
            start = cs_ref[tile, e]
            n_chunks = (cs_ref[tile + 1, e] - start + (COMBINE_CHUNK - 1)) // COMBINE_CHUNK
            seg_ref[sl, e] = off - start

            def one(c, carry):
                chunk_copy(base_ref[e] + start + c * COMBINE_CHUNK, off + c * COMBINE_CHUNK, sl).start()
                return carry

            lax.fori_loop(0, n_chunks, one, 0)
            return off + n_chunks * COMBINE_CHUNK

        total = lax.fori_loop(0, N_EXPERTS, per_expert, 0)
        nch_ref[sl] = total // COMBINE_CHUNK

    @pl.when(i == 0)
    def _():
        issue(0, 0)

    @pl.when(i + 1 < pl.num_programs(0))
    def _():
        issue(i + 1, 1 - cur)

    def drain(c, carry):
        chunk_copy(0, 0, cur).wait()
        return carry

    ev = ev_ref[...]
    rows = rv_ref[...]
    for e in range(N_EXPERTS):
        rows = rows + jnp.where(ev == e, seg_ref[cur, e], 0)
    rowv_ref[...] = rows * ROW_TILES
    to_smem = pltpu.make_async_copy(rowv_ref, rows_ref, rsem)
    to_smem.start()
    lax.fori_loop(0, nch_ref[cur], drain, 0)
    to_smem.wait()

    per_line = LANES // TOP_K

    def line(li, carry):
        for u in range(per_line):
            acc = None
            for kk in range(TOP_K):
                a = u * TOP_K + kk
                first = pl.multiple_of(rows_ref[li, a], ROW_TILES)
                term = w_ref[li * LANES + a] * buf_ref[cur, pl.ds(first, ROW_TILES), :]
                acc = term if acc is None else acc + term
            acc_ref[_row_tile(li * per_line + u), :] = acc
        return carry

    lax.fori_loop(0, tt // per_line, line, 0)
    mixed = jnp.concatenate([acc_ref[_row_slab(tt, j), :] for j in range(ROW_TILES)], axis=1)
    pre = DEEPNORM_ALPHA * xm_ref[...] + gate_ref[0] * mixed
    o_ref[...] = _layer_norm(pre) * l2g_ref[...] + l2b_ref[...]


def _combine(ys, tile_starts, base, e_lines, r_lines, w_flat, x_mid, gate2, ln2_g, ln2_b, tiles_per_batch):
    t, d = x_mid.shape
    tt = TOK_TILE
    n = t // tt
    cap = tt * TOP_K + N_EXPERTS * COMBINE_CHUNK
    lines = tt * TOP_K // LANES
    line_blk = lambda: pl.BlockSpec((lines, LANES), lambda i, *_: (i, 0))
    return pl.pallas_call(
        _combine_kernel,
        out_shape=jax.ShapeDtypeStruct((t, d), F32),
        grid_spec=pltpu.PrefetchScalarGridSpec(
            num_scalar_prefetch=2,
            grid=(n,),
            in_specs=[
                pl.BlockSpec((tt * TOP_K,), lambda i, *_: (i,), memory_space=pltpu.SMEM),
                line_blk(), line_blk(),
                pl.BlockSpec((tt, d), lambda i, *_: (i, 0)),
                pl.BlockSpec((1, 1, d), lambda i, *_: (i // tiles_per_batch, 0, 0)),
                pl.BlockSpec((1, d), lambda i, *_: (0, 0)),
                pl.BlockSpec((1, d), lambda i, *_: (0, 0)),
                pl.BlockSpec(memory_space=pl.ANY),
            ],
            out_specs=pl.BlockSpec((tt, d), lambda i, *_: (i, 0)),
            scratch_shapes=[pltpu.VMEM((2, cap * ROW_TILES, LANES), F32), pltpu.VMEM((tt * ROW_TILES, LANES), F32),
                            pltpu.VMEM((lines, LANES), jnp.int32), pltpu.SMEM((lines, LANES), jnp.int32),
                            pltpu.SMEM((2, N_EXPERTS), jnp.int32), pltpu.SMEM((2,), jnp.int32),
                            pltpu.SemaphoreType.DMA((2,)), pltpu.SemaphoreType.DMA],
        ),
        compiler_params=_cparams(("arbitrary",)),
        name="moe_combine",
    )(tile_starts, base, w_flat, e_lines, r_lines, x_mid, gate2, ln2_g, ln2_b, ys)


def kernel(x, c, ctx, c_ctx, ada_w, ada_b, w_in, gla_wa_f, gla_ba_f, gla_wa_b, gla_ba_b, gla_norm_g,
           hy_conv_w, hy_conv_b, hy_flt_w1, hy_flt_b1, hy_flt_w2, hy_flt_b2, hy_flt_wout, hy_flt_freq,
           hy_bias_d, w_out, ln1_g, ln1_b, router_w, router_b, exp_w1, exp_b1, exp_w2, exp_b2, ln2_g, ln2_b):
    batch, seq_len, d = x.shape
    lyr = 0
    ch = HY_WIDTH

    n_rows = 8 * ((batch + 1 + 7) // 8)
    cvec = jnp.zeros((n_rows, d), F32).at[:batch].set(c).at[batch].set(c_ctx)
    mod = _modulation(cvec, ada_w[lyr], ada_b[lyr][None, :])
    part = lambda rows, i: rows[:, None, i * d:(i + 1) * d]
    mod_x = mod[:batch]
    mod_c = jnp.broadcast_to(mod[batch:batch + 1], (batch, 6 * d))

    w = w_in[lyr]
    o_q, o_k, o_v, o_g = 0, GLA_KEY, 2 * GLA_KEY, 2 * GLA_KEY + GLA_VAL
    o_a = o_g + GLA_VAL
    o_h = o_a + 2 * GLA_RANK
    a_cols = jnp.pad(w[:, o_a:o_h], ((0, 0), (0, A_PAD - 2 * GLA_RANK)))
    q_cols = w[:, o_q:o_k] * (GLA_DK ** -0.5)
    w_main = jnp.concatenate([q_cols, w[:, o_k:o_a], a_cols], axis=1).astype(BF16)
    w_ctx = jnp.concatenate([q_cols, w[:, o_k:o_g], a_cols], axis=1).astype(BF16)
    w_hy_t = w[:, o_h:].T.astype(BF16)
    conv_w = jnp.broadcast_to(hy_conv_w[lyr][:, :, None], (HY_CONV, 3 * ch, LANES))
    conv_b = jnp.broadcast_to(hy_conv_b[lyr][:, None], (3 * ch, LANES))

    ctx_qkva = _input_projection(ctx, part(mod_c, 0), part(mod_c, 1), w_ctx)
    q, k, v, g, a_low, u_t = _input_projection(x, part(mod_x, 0), part(mod_x, 1), w_main, w_hy_t, conv_w, conv_b)

    wa = jnp.zeros((2, A_PAD, GLA_KEY), F32)
    wa = wa.at[0, :GLA_RANK].set(gla_wa_f[lyr]).at[1, GLA_RANK:2 * GLA_RANK].set(gla_wa_b[lyr]).astype(BF16)
    ba = jnp.stack([gla_ba_f[lyr], gla_ba_b[lyr]])[:, None, :]
    o_f, o_b = _gla_scan(ctx_qkva, (q, k, v, a_low), wa, ba)

    r1 = 2 * seq_len // LANES
    consts = _dft_constants(r1)
    zz, tn_rows, rate = _filter_inputs(seq_len)
    w1_t = jnp.pad(hy_flt_w1[lyr], ((0, HY_EMB_PAD - HY_EMB), (0, 0))).T
    unit_tile = lambda vec: jnp.broadcast_to(vec[:, None], (HY_FH, LANES))
    wo_t = hy_flt_wout[lyr].reshape(HY_FH, HY_ORDER, 2, ch).transpose(1, 2, 3, 0)
    filt = _filter_mlp(zz, tn_rows, rate, w1_t, unit_tile(hy_flt_b1[lyr]), hy_flt_w2[lyr].T,
                       unit_tile(hy_flt_b2[lyr]), unit_tile(hy_flt_freq[lyr]), wo_t)
    spectra = _filter_spectra(filt.reshape(HY_ORDER, ch, r1, LANES), consts, r1)
    y_hy = _hyena_conv(u_t.reshape(batch // 2, 2, 3 * ch, r1 // 2, LANES), spectra, hy_bias_d[lyr], consts, r1)
    y_hy_t = y_hy.reshape(batch, ch, seq_len)

    wo = w_out[lyr].astype(BF16)
    wr = jnp.pad(router_w[lyr], ((0, 0), (0, LANES - N_EXPERTS))).astype(BF16)
    rb = jnp.pad(router_b[lyr], (0, LANES - N_EXPERTS), constant_values=NEG_BIG)[None, :]
    norm_g = jnp.tile(gla_norm_g[lyr], GLA_HEADS)[None, :]
    x_mid, h2, sel_e, sel_r, sel_w, cnt, tile_cs = _output_projection(
        o_f, o_b, g, y_hy_t, x, part(mod_x, 2), part(mod_x, 3), part(mod_x, 4),
        wo[:GLA_VAL], wo[GLA_VAL:], norm_g, ln1_g[lyr][None], ln1_b[lyr][None], wr, rb)

    t = batch * seq_len
    counts = cnt[0, :N_EXPERTS]
    tiles_e = (counts + (MOE_TILE - 1)) // MOE_TILE
    tile_end = jnp.cumsum(tiles_e)
    base = ((tile_end - tiles_e) * MOE_TILE).astype(jnp.int32)
    n_tiles = tile_end[-1:].astype(jnp.int32)
    max_tiles = t * TOP_K // MOE_TILE + N_EXPERTS + FFN_TILES_PER_STEP
    tile_ids = jnp.minimum(jnp.arange(max_tiles, dtype=jnp.int32), n_tiles[0] - 1)
    tile_expert = jnp.sum(tile_ids[:, None] >= tile_end[None, :], axis=1).astype(jnp.int32)
    as_lines = lambda sel: sel.reshape(t, LANES)[:, :TOP_K].reshape(t * TOP_K // LANES, LANES)
    e_lines, r_lines = as_lines(sel_e), as_lines(sel_r)
    w_flat = as_lines(sel_w).reshape(t * TOP_K)
    tile_starts = jnp.concatenate([tile_cs[:, 0, :], cnt[0:1]], axis=0)

    n_blk = 2 * D_EXPERT // SWIGLU_BLOCK
    b1p_e = exp_b1[lyr].reshape(N_EXPERTS, n_blk, LANES, 2).transpose(0, 1, 3, 2).reshape(N_EXPERTS, 1, 2 * D_EXPERT)
    xs, w1p_e, w2_e = _dispatch(h2.reshape(t * ROW_TILES, LANES), e_lines, r_lines, base, counts, n_tiles,
                                max_tiles * MOE_TILE, exp_w1[lyr], exp_w2[lyr])
    ys = _expert_ffn(xs, tile_expert, n_tiles, w1p_e, b1p_e, w2_e, exp_b2[lyr][:, None, :])
    out = _combine(ys, tile_starts, base, e_lines, r_lines, w_flat, x_mid.reshape(t, d), part(mod_x, 5),
                   ln2_g[lyr][None], ln2_b[lyr][None], seq_len // TOK_TILE)
    return out.reshape(batch, seq_len, d)
```

```python
import functools
import math

import numpy as np
import jax
import jax.numpy as jnp
from jax import lax
from jax.experimental import pallas as pl
from jax.experimental.pallas import tpu as pltpu

F32 = jnp.float32
BF16 = jnp.bfloat16

D_MODEL = 1024
DEPTH = 1
GRID_W = 64
LN_EPS = 1e-6
DEEPNORM_ALPHA = (2 * DEPTH) ** 0.25
GLA_HEADS = 4
GLA_DK = 64
GLA_DV = 128
GLA_KEY = GLA_HEADS * GLA_DK
GLA_VAL = GLA_HEADS * GLA_DV
GLA_RANK = 16
GLA_TAU = 16.0
HY_WIDTH = D_MODEL - GLA_VAL
HY_ORDER = 2
HY_CONV = 3
HY_EMB = 33
HY_FH = 64
HY_TARGET = 1e-2
HY_FAST = 0.3
HY_SLOW = 1.5
N_EXPERTS = 32
TOP_K = 4
D_EXPERT = D_MODEL
SWIGLU_ALPHA = 1.702
SWIGLU_LIMIT = 7.0

LANES = 128
VMEM_LIMIT = 56 * 1024 * 1024

TOK_TILE = 512
PROJ_SUB_TILE = 256
GLA_CHUNK = 128
GLA_BLOCK = 256
GLA_BATCH = 4
GLA_SAFE_EXPONENT = 80.0
A_PAD = LANES
HY_GROUP = 8
HY_CH_BLOCK = 16
FILT_TILE = 1024
HY_EMB_PAD = 40
MOE_TILE = 512
FFN_TILES_PER_STEP = 2
COMBINE_CHUNK = 64
ROW_TILES = D_MODEL // LANES


def _row_slab(n_rows, j):
    return pl.ds(j, n_rows, stride=ROW_TILES)


def _row_tile(r):
    return pl.ds(pl.multiple_of(r * ROW_TILES, ROW_TILES), ROW_TILES)
SWIGLU_BLOCK = 2 * LANES
NEG_BIG = -1e30


def _cparams(sem):
    return pltpu.CompilerParams(dimension_semantics=sem, vmem_limit_bytes=VMEM_LIMIT)


def _layer_norm(x):
    mu = jnp.mean(x, axis=-1, keepdims=True)
    xc = x - mu
    return xc * lax.rsqrt(jnp.mean(xc * xc, axis=-1, keepdims=True) + LN_EPS)


def _dot(a, b):
    return jnp.dot(a, b, preferred_element_type=F32)


def _dot_nt(a, b):
    return lax.dot_general(a, b, (((1,), (1,)), ((), ())), preferred_element_type=F32)


def _dot_tn(a, b):
    return lax.dot_general(a, b, (((0,), (0,)), ((), ())), preferred_element_type=F32)


def _mod_kernel(c_ref, w_ref, b_ref, o_ref):
    c = c_ref[...]
    s = c * jax.nn.sigmoid(c)
    o_ref[...] = _dot(s.astype(BF16), w_ref[...].astype(BF16)) + b_ref[...]


def _modulation(cvec, ada_w, ada_b):
    rows, d = cvec.shape
    n = ada_w.shape[1]
    tn = 2048
    return pl.pallas_call(
        _mod_kernel,
        out_shape=jax.ShapeDtypeStruct((rows, n), F32),
        grid=(n // tn,),
        in_specs=[
            pl.BlockSpec((rows, d), lambda j: (0, 0)),
            pl.BlockSpec((d, tn), lambda j: (0, j)),
            pl.BlockSpec((1, tn), lambda j: (0, j)),
        ],
        out_specs=pl.BlockSpec((rows, tn), lambda j: (0, j)),
        compiler_params=_cparams(("arbitrary",)),
        name="adaln_mod",
    )(cvec, ada_w, ada_b)


def _inproj_kernel(x_ref, sh_ref, sc_ref, wm_ref, *rest, with_hy):
    if with_hy:
        wh_ref, cw_ref, cb_ref, q_ref, k_ref, v_ref, g_ref, a_ref, u_ref = rest
    else:
        q_ref, k_ref, v_ref, a_ref = rest
    tt = x_ref.shape[1]
    n_sub = max(tt // PROJ_SUB_TILE, 1)
    sub = tt // n_sub
    for s in range(n_sub):
        rows = slice(s * sub, (s + 1) * sub)
        h = _layer_norm(x_ref[0, rows, :]) * (1.0 + sc_ref[0]) + sh_ref[0]
        hb = h.astype(BF16)
        u = _dot(hb, wm_ref[...])
        q_ref[0, rows, :] = u[:, :GLA_KEY].astype(BF16)
        k_ref[0, rows, :] = u[:, GLA_KEY:2 * GLA_KEY].astype(BF16)
        v_ref[0, rows, :] = u[:, 2 * GLA_KEY:2 * GLA_KEY + GLA_VAL].astype(BF16)
        off = 2 * GLA_KEY + GLA_VAL
        if with_hy:
            g_ref[0, rows, :] = u[:, off:off + GLA_VAL].astype(BF16)
            off += GLA_VAL
        a_ref[0, rows, :] = u[:, off:off + A_PAD]
        if with_hy:
            ut = _dot_nt(wh_ref[...], hb)
            lane = lax.broadcasted_iota(jnp.int32, (1, LANES), 1) % GRID_W
            not_first = (lane != 0).astype(F32)
            not_last = (lane != GRID_W - 1).astype(F32)
            for j in range(sub // LANES):
                c = ut[:, j * LANES:(j + 1) * LANES]
                left = pltpu.roll(c, 1, axis=1) * not_first
                right = pltpu.roll(c, LANES - 1, axis=1) * not_last
                y = cw_ref[0] * left + cw_ref[1] * c + cw_ref[2] * right + cb_ref[...]
                lanes = slice(s * sub + j * LANES, s * sub + (j + 1) * LANES)
                u_ref[0, :, lanes] = y.astype(BF16)


def _input_projection(x, shift, scale, w_main, w_hy_t=None, conv_w=None, conv_b=None):
    b, l, d = x.shape
    tt = min(TOK_TILE, l)
    with_hy = w_hy_t is not None
    nm = w_main.shape[1]
    tok = lambda width, dt: jax.ShapeDtypeStruct((b, l, width), dt)
    tok_spec = lambda width: pl.BlockSpec((1, tt, width), lambda i, j: (i, j, 0))
    in_specs = [
        pl.BlockSpec((1, tt, d), lambda i, j: (i, j, 0)),
        pl.BlockSpec((1, 1, d), lambda i, j: (i, 0, 0)),
        pl.BlockSpec((1, 1, d), lambda i, j: (i, 0, 0)),
        pl.BlockSpec((d, nm), lambda i, j: (0, 0)),
    ]
    args = [x, shift, scale, w_main]
    if with_hy:
        ch = w_hy_t.shape[0]
        in_specs += [
            pl.BlockSpec((ch, d), lambda i, j: (0, 0)),
            pl.BlockSpec((HY_CONV, ch, LANES), lambda i, j: (0, 0, 0)),
            pl.BlockSpec((ch, LANES), lambda i, j: (0, 0)),
        ]
        args += [w_hy_t, conv_w, conv_b]
        out_shape = [tok(GLA_KEY, BF16), tok(GLA_KEY, BF16), tok(GLA_VAL, BF16), tok(GLA_VAL, BF16),
                     tok(A_PAD, F32), jax.ShapeDtypeStruct((b, ch, l), BF16)]
        out_specs = [tok_spec(GLA_KEY), tok_spec(GLA_KEY), tok_spec(GLA_VAL), tok_spec(GLA_VAL),
                     tok_spec(A_PAD), pl.BlockSpec((1, ch, tt), lambda i, j: (i, 0, j))]
    else:
        out_shape = [tok(GLA_KEY, BF16), tok(GLA_KEY, BF16), tok(GLA_VAL, BF16), tok(A_PAD, F32)]
        out_specs = [tok_spec(GLA_KEY), tok_spec(GLA_KEY), tok_spec(GLA_VAL), tok_spec(A_PAD)]
    return pl.pallas_call(
        functools.partial(_inproj_kernel, with_hy=with_hy),
        out_shape=out_shape,
        grid=(b, l // tt),
        in_specs=in_specs,
        out_specs=out_specs,
        compiler_params=_cparams(("parallel", "arbitrary")),
        name="in_proj_hy" if with_hy else "in_proj_ctx",
    )(*args)


def _gla_kernel(*refs, n_ctx_blocks, chunk, n_batch, n_levels, safe_exponent):
    ins = [refs[0:8], refs[8:16]]
    wa_ref, ba_ref, cw_ref, cm_ref, hm_ref, bd_ref, dq_ref, dk_ref, lm_ref = refs[16:25]
    outs = refs[25:27]
    st_ref = refs[27]
    s = pl.program_id(1)
    blk = ins[0][4].shape[1]
    n_chunks = blk // chunk

    @pl.when(s == 0)
    def _():
        st_ref[...] = jnp.zeros_like(st_ref)

    def two_pass(m, g_hi, g_lo):
        return _dot(m, g_hi) + _dot(m, g_lo)

    is_ctx = s < n_ctx_blocks
    bdmask = bd_ref[...]
    chains = [(d, bi) for d in range(2) for bi in range(n_batch)]

    def pre_activation(d, bi):
        a = jnp.where(is_ctx, ins[d][3][bi], ins[d][7][bi])
        return _dot(a.astype(BF16), wa_ref[d]) + ba_ref[d]

    def advance(d, bi, z, single_reference):
        qc_ref, kc_ref, vc_ref, _, ql_ref, kl_ref, vl_ref, _ = ins[d]
        q = jnp.where(is_ctx, qc_ref[bi], ql_ref[bi]).astype(F32)
        k = jnp.where(is_ctx, kc_ref[bi], kl_ref[bi]).astype(F32)
        v = jnp.where(is_ctx, vc_ref[bi], vl_ref[bi])
        g = (jnp.minimum(z, 0.0) - jnp.log(1.0 + jnp.exp(-jnp.abs(z)))) * (1.0 / GLA_TAU)
        g_hi = g.astype(BF16)
        g_lo = (g - g_hi.astype(F32)).astype(BF16)
        cum = two_pass(cw_ref[d], g_hi, g_lo)
        edge = (lambda c: (c + 1) * chunk - 1) if d == 0 else (lambda c: c * chunk)
        tot = jnp.concatenate([jnp.broadcast_to(cum[edge(c):edge(c) + 1], (chunk, GLA_KEY))
                               for c in range(n_chunks)], axis=0)
        qs_all = (q * jnp.exp(cum)).astype(BF16)
        kst_all = (k * jnp.exp(tot - cum)).astype(BF16)
        dec_all = jnp.exp(tot)

        if single_reference:
            ks_all = (k * jnp.exp(-cum)).astype(BF16)
            cmask = cm_ref[d]
            rows_out = []
            for c in range(n_chunks):
                rows = slice(c * chunk, (c + 1) * chunk)
                parts = []
                for h in range(GLA_HEADS):
                    a_h = _dot_nt(qs_all[rows] * hm_ref[h], ks_all[rows]) * cmask
                    parts.append(_dot(a_h.astype(BF16), v[rows, h * GLA_DV:(h + 1) * GLA_DV]))
                rows_out.append(jnp.concatenate(parts, axis=1))
            o_intra = jnp.concatenate(rows_out, axis=0)
        else:
            def level(lv, acc):
                ql = (q * jnp.exp(two_pass(dq_ref[d, lv], g_hi, g_lo))).astype(BF16)
                kl = (k * jnp.exp(two_pass(dk_ref[d, lv], g_hi, g_lo))).astype(BF16)
                msk = lm_ref[d, lv]
                parts = []
                for h in range(GLA_HEADS):
                    a_h = _dot_nt(ql * hm_ref[h], kl) * msk
                    parts.append(_dot(a_h.astype(BF16), v[:, h * GLA_DV:(h + 1) * GLA_DV]))
                return acc + jnp.concatenate(parts, axis=1)
            o_intra = lax.fori_loop(0, n_levels, level, jnp.zeros((blk, GLA_VAL), F32))

        for n in range(n_chunks):
            ci = n if d == 0 else n_chunks - 1 - n
            rows = slice(ci * chunk, (ci + 1) * chunk)
            o = _dot_nt(qs_all[rows], st_ref[d, bi].astype(BF16)) + o_intra[rows]
            outs[d][bi, rows, :] = o.astype(outs[d].dtype)
            st_ref[d, bi] = (st_ref[d, bi] * dec_all[ci * chunk:ci * chunk + 1]
                             + _dot_tn(v[rows], kst_all[rows]) * bdmask)

    zs = [pre_activation(d, bi) for d, bi in chains]
    z_low = zs[0]
    for z in zs[1:]:
        z_low = jnp.minimum(z_low, z)
    bound = (jnp.maximum(-jnp.min(z_low), 0.0) + math.log(2.0)) * (chunk / GLA_TAU)

    def step(single_reference):
        def run():
            for (d, bi), z in zip(chains, zs):
                advance(d, bi, z, single_reference)
        return run

    lax.cond(bound < safe_exponent, step(True), step(False))


def _gla_level_tables(blk, chunk):
    sizes = []
    s = chunk // 2
    while s >= 1:
        sizes.append(s)
        s //= 2
    n_lv = len(sizes) + 1
    dq = np.zeros((2, n_lv, blk, blk), np.float32)
    dk = np.zeros((2, n_lv, blk, blk), np.float32)
    lm = np.zeros((2, n_lv, blk, blk), np.float32)
    idx = np.arange(blk)
    for lv, s in enumerate(sizes):
        blk_id = idx // (2 * s)
        right = (idx % (2 * s)) >= s
        b = blk_id * 2 * s + s
        same = blk_id[:, None] == blk_id[None, :]
        m = idx[None, :]
        dq[0, lv] = (right[:, None] & (m >= b[:, None]) & (m <= idx[:, None]))
        dk[0, lv] = (~right[:, None] & (m > idx[:, None]) & (m < b[:, None]))
        lm[0, lv] = same & right[:, None] & ~right[None, :]
        dq[1, lv] = (~right[:, None] & (m >= idx[:, None]) & (m < b[:, None]))
        dk[1, lv] = (right[:, None] & (m >= b[:, None]) & (m < idx[:, None]))
        lm[1, lv] = same & ~right[:, None] & right[None, :]
    lm[:, n_lv - 1] = np.eye(blk, dtype=np.float32)
    return dq, dk, lm, n_lv


def _gla_scan(ctx_qkva, lat_qkva, wa, ba, chunk=GLA_CHUNK, blk=GLA_BLOCK, safe_exponent=GLA_SAFE_EXPONENT):
    qc, kc, vc, ac = ctx_qkva
    ql, kl, vl, al = lat_qkva
    b, l, _ = ql.shape
    ctx_len = qc.shape[1]
    n_ctx = ctx_len // blk
    n_lat = l // blk
    c = chunk
    idx = np.arange(c)
    tri = np.stack([idx[:, None] >= idx[None, :], idx[:, None] <= idx[None, :]]).astype(np.float32)
    cw = np.stack([np.kron(np.eye(blk // c, dtype=np.float32), tri[dd]) for dd in range(2)])
    dq, dk, lm, n_levels = _gla_level_tables(blk, c)
    hmask = np.zeros((GLA_HEADS, 1, GLA_KEY), np.float32)
    for h in range(GLA_HEADS):
        hmask[h, 0, h * GLA_DK:(h + 1) * GLA_DK] = 1.0
    bd = (np.arange(GLA_VAL)[:, None] // GLA_DV == np.arange(GLA_KEY)[None, :] // GLA_DK).astype(np.float32)

    nb = GLA_BATCH if b % GLA_BATCH == 0 else 1

    def lat_block(d):
        def f(i, s):
            t = jnp.maximum(s - n_ctx, 0)
            return (i, t if d == 0 else n_lat - 1 - t, 0)
        return f

    def ctx_block(d):
        def f(i, s):
            t = jnp.minimum(s, n_ctx - 1)
            return (i, t if d == 0 else n_ctx - 1 - t, 0)
        return f

    widths = (GLA_KEY, GLA_KEY, GLA_VAL, A_PAD)
    dir_specs = lambda d: ([pl.BlockSpec((nb, blk, w), ctx_block(d)) for w in widths]
                           + [pl.BlockSpec((nb, blk, w), lat_block(d)) for w in widths])
    consts = [wa, ba, jnp.asarray(cw, BF16), jnp.asarray(tri, F32), jnp.asarray(hmask, BF16), jnp.asarray(bd, F32),
              jnp.asarray(dq, BF16), jnp.asarray(dk, BF16), jnp.asarray(lm, F32)]
    full = lambda arr: pl.BlockSpec(arr.shape, lambda i, s: (0,) * arr.ndim)
    o_sds = jax.ShapeDtypeStruct((b, l, GLA_VAL), BF16)
    dir_args = [qc, kc, vc, ac, ql, kl, vl, al]
    return pl.pallas_call(
        functools.partial(_gla_kernel, n_ctx_blocks=n_ctx, chunk=c, n_batch=nb, n_levels=n_levels,
                          safe_exponent=safe_exponent),
        out_shape=[o_sds, o_sds],
        grid=(b // nb, n_ctx + n_lat),
        in_specs=dir_specs(0) + dir_specs(1) + [full(arr) for arr in consts],
        out_specs=[pl.BlockSpec((nb, blk, GLA_VAL), lat_block(0)), pl.BlockSpec((nb, blk, GLA_VAL), lat_block(1))],
        scratch_shapes=[pltpu.VMEM((2, nb, GLA_VAL, GLA_KEY), F32)],
        compiler_params=_cparams(("parallel", "arbitrary")),
        name="gla_scan",
    )(*dir_args, *dir_args, *consts)


def _dft_constants(r1):
    n = r1 * LANES
    h = r1 // 2
    k1 = np.arange(r1)
    f1 = np.exp(-2j * np.pi * np.outer(k1, k1) / r1)
    f2 = np.exp(-2j * np.pi * np.outer(np.arange(LANES), np.arange(LANES)) / LANES)
    tw = np.exp(-2j * np.pi * np.outer(k1, np.arange(LANES)) / n)
    fa_c = np.block([[f1.real[:, :h], -f1.imag[:, :h]], [f1.imag[:, :h], f1.real[:, :h]]])
    fa_r = np.concatenate([f1.real, f1.imag], axis=0)
    gc = np.block([[f2.real, f2.imag], [-f2.imag, f2.real]])
    gci = np.block([[f2.real, -f2.imag], [f2.imag, f2.real]])
    fai = np.block([[f1.real[:h], f1.imag[:h]], [-f1.imag[:h], f1.real[:h]]]) / n
    tw_lane = np.tile(tw, (1, HY_GROUP))
    tw_row = np.tile(tw, (HY_GROUP, 1))
    f = lambda a: jnp.asarray(a, F32)
    return dict(fa_c=f(fa_c), fa_r=f(fa_r), gc=f(gc), gci=f(gci), fai=f(fai),
                twl_r=f(tw_lane.real), twl_i=f(tw_lane.imag), twr_r=f(tw_row.real), twr_i=f(tw_row.imag))


def _fwd_dft(rhs, fa, gc, twl_r, twl_i, r1):
    a = _dot(fa, rhs)
    ar, ai = a[:r1], a[r1:]
    br = (ar * twl_r - ai * twl_i).astype(BF16)
    bi = (ar * twl_i + ai * twl_r).astype(BF16)
    lhs = jnp.concatenate(
        [jnp.concatenate([br[:, c * LANES:(c + 1) * LANES], bi[:, c * LANES:(c + 1) * LANES]], axis=1)
         for c in range(HY_GROUP)], axis=0)
    return _dot(lhs, gc)


def _inv_dft(yr, yi, gci, fai, twr_r, twr_i, r1):
    lhs = jnp.concatenate([yr, yi], axis=1).astype(BF16)
    c = _dot(lhs, gci)
    cr, ci = c[:, :LANES], c[:, LANES:]
    dr = (cr * twr_r + ci * twr_i).astype(BF16)
    di = (ci * twr_r - cr * twr_i).astype(BF16)
    rhs = jnp.concatenate(
        [jnp.concatenate([dr[g * r1:(g + 1) * r1], di[g * r1:(g + 1) * r1]], axis=0)
         for g in range(HY_GROUP)], axis=1)
    return _dot(fai, rhs)


def _filter_mlp_kernel(zt_ref, tn_ref, rate_ref, w1_ref, b1_ref, w2_ref, b2_ref, fr_ref, wo_ref, o_ref):
    hp = lax.Precision.HIGHEST
    lt = tn_ref.shape[1]
    lanes = lambda ref: jnp.concatenate([ref[...]] * (lt // LANES), axis=1)
    fr = lanes(fr_ref)
    hid = jnp.sin(fr * (jnp.dot(w1_ref[...], zt_ref[...], precision=hp, preferred_element_type=F32) + lanes(b1_ref)))
    hid = jnp.sin(fr * (jnp.dot(w2_ref[...], hid, precision=hp, preferred_element_type=F32) + lanes(b2_ref)))
    window = jnp.exp(-tn_ref[0:1, :] * lanes(rate_ref)) * tn_ref[1:2, :]
    for o in range(HY_ORDER):
        o_ref[o] = jnp.dot(wo_ref[o, 0], hid, precision=hp, preferred_element_type=F32) * window


def _filter_mlp(z_t, tn_rows, rate, w1_t, b1, w2_t, b2, freq, wo_t):
    emb, n2l = z_t.shape
    l = n2l // 2
    lt = min(FILT_TILE, l)
    nt = l // lt
    ch = rate.shape[0]
    full = lambda shape: pl.BlockSpec(shape, lambda d, j: (0,) * len(shape))
    return pl.pallas_call(
        _filter_mlp_kernel,
        out_shape=jax.ShapeDtypeStruct((HY_ORDER, ch, n2l), F32),
        grid=(2, nt),
        in_specs=[
            pl.BlockSpec((emb, lt), lambda d, j: (0, d * nt + j)),
            pl.BlockSpec((8, lt), lambda d, j: (0, d * nt + j)),
            full((ch, LANES)),
            full((HY_FH, emb)), full((HY_FH, LANES)), full((HY_FH, HY_FH)), full((HY_FH, LANES)),
            full((HY_FH, LANES)),
            pl.BlockSpec((HY_ORDER, 1, ch, HY_FH), lambda d, j: (0, d, 0, 0)),
        ],
        out_specs=pl.BlockSpec((HY_ORDER, ch, lt), lambda d, j: (0, 0, d * nt + j)),
        compiler_params=_cparams(("arbitrary", "arbitrary")),
        name="hyena_filter_mlp",
    )(z_t, tn_rows, rate, w1_t, b1, w2_t, b2, freq, wo_t)


def _filter_fft_kernel(f_ref, fa_ref, gc_ref, twl_r_ref, twl_i_ref, h_ref, *, r1):
    fa = fa_ref[...].astype(BF16)
    gc = gc_ref[...].astype(BF16)
    nc = f_ref.shape[1]
    for g0 in range(0, nc, HY_GROUP):
        rhs = jnp.concatenate([f_ref[0, g0 + c].astype(BF16) for c in range(HY_GROUP)], axis=1)
        x = _fwd_dft(rhs, fa, gc, twl_r_ref[...], twl_i_ref[...], r1)
        for c in range(HY_GROUP):
            h_ref[0, g0 + c] = x[c * r1:(c + 1) * r1].astype(h_ref.dtype)


def _filter_spectra(filt, consts, r1):
    order, ch = filt.shape[:2]
    nc = HY_CH_BLOCK
    full = lambda a: pl.BlockSpec(a.shape, lambda o, j: (0,) * a.ndim)
    cs = [consts["fa_r"], consts["gc"], consts["twl_r"], consts["twl_i"]]
    return pl.pallas_call(
        functools.partial(_filter_fft_kernel, r1=r1),
        out_shape=jax.ShapeDtypeStruct((order, ch, r1, 2 * LANES), BF16),
        grid=(order, ch // nc),
        in_specs=[pl.BlockSpec((1, nc, r1, LANES), lambda o, j: (o, j, 0, 0))] + [full(a) for a in cs],
        out_specs=pl.BlockSpec((1, nc, r1, 2 * LANES), lambda o, j: (o, j, 0, 0)),
        compiler_params=_cparams(("arbitrary", "arbitrary")),
        name="hyena_filter_fft",
    )(filt, *cs)


def _hyena_kernel(dbias_ref, v_ref, x1_ref, x2_ref, h_ref, fa_ref, gc_ref, gci_ref, fai_ref,
                  twl_r_ref, twl_i_ref, twr_r_ref, twr_i_ref, y_ref, *, r1):
    fa = fa_ref[...].astype(BF16)
    gc = gc_ref[...].astype(BF16)
    gci = gci_ref[...].astype(BF16)
    fai = fai_ref[...].astype(BF16)
    twl_r, twl_i = twl_r_ref[...], twl_i_ref[...]
    twr_r, twr_i = twr_r_ref[...], twr_i_ref[...]
    nc = v_ref.shape[2]
    half = r1 // 2
    c_base = pl.program_id(0) * nc

    def conv(sig, sig_b, order, g0):
        rhs = jnp.concatenate(
            [jnp.concatenate([sig_b[c][0], sig_b[c][1]], axis=0) for c in range(HY_GROUP)], axis=1)
        x = _fwd_dft(rhs, fa, gc, twl_r, twl_i, r1)
        xr, xi = x[:, :LANES], x[:, LANES:]
        hh = jnp.concatenate([h_ref[order, g0 + c] for c in range(HY_GROUP)], axis=0).astype(F32)
        hr, hi = hh[:, :LANES], hh[:, LANES:]
        y = _inv_dft(xr * hr - xi * hi, xr * hi + xi * hr, gci, fai, twr_r, twr_i, r1)
        out = []
        for c in range(HY_GROUP):
            dcoef = dbias_ref[order, c_base + g0 + c]
            yc = y[:, c * LANES:(c + 1) * LANES]
            out.append([yc[:half] + dcoef * sig[c][0], yc[half:] + dcoef * sig[c][1]])
        return out

    for g0 in range(0, nc, HY_GROUP):
        v_b = [[v_ref[0, b, g0 + c] for b in range(2)] for c in range(HY_GROUP)]
        v = [[t.astype(F32) for t in pair] for pair in v_b]
        y1 = conv(v, v_b, 0, g0)
        z = [[x1_ref[0, b, g0 + c].astype(F32) * y1[c][b] for b in range(2)] for c in range(HY_GROUP)]
        z_b = [[t.astype(BF16) for t in pair] for pair in z]
        y2 = conv(z, z_b, 1, g0)
        for c in range(HY_GROUP):
            for b in range(2):
                y_ref[0, b, g0 + c] = (x2_ref[0, b, g0 + c].astype(F32) * y2[c][b]).astype(y_ref.dtype)


def _hyena_conv(u_t, spectra, d_bias, consts, r1):
    bp, _, ch3, half, _ = u_t.shape
    ch = ch3 // 3
    nc = HY_CH_BLOCK
    nblk = ch // nc
    names = ["fa_c", "gc", "gci", "fai", "twl_r", "twl_i", "twr_r", "twr_i"]
    cs = [consts[k] for k in names]
    full = lambda a: pl.BlockSpec(a.shape, lambda j, p: (0,) * a.ndim)
    part = lambda k: pl.BlockSpec((1, 2, nc, half, LANES), lambda j, p: (p, 0, k * nblk + j, 0, 0))
    return pl.pallas_call(
        functools.partial(_hyena_kernel, r1=r1),
        out_shape=jax.ShapeDtypeStruct((bp, 2, ch, half, LANES), BF16),
        grid=(nblk, bp),
        in_specs=[pl.BlockSpec(memory_space=pltpu.SMEM), part(0), part(1), part(2),
                  pl.BlockSpec((HY_ORDER, nc, r1, 2 * LANES), lambda j, p: (0, j, 0, 0))] + [full(a) for a in cs],
        out_specs=pl.BlockSpec((1, 2, nc, half, LANES), lambda j, p: (p, 0, j, 0, 0)),
        compiler_params=_cparams(("arbitrary", "arbitrary")),
        name="hyena_conv",
    )(d_bias, u_t, u_t, u_t, spectra, *cs)


def _filter_inputs(l):
    n = jnp.arange(2 * l, dtype=jnp.int32)
    t = jnp.where(n < l, n, 2 * l - n).astype(F32)
    valid = (n != l).astype(F32)
    t_norm = t / (l - 1)
    bands = (HY_EMB - 1) // 2
    f = jnp.linspace(1e-4, bands - 1, bands, dtype=F32)
    ang = (2.0 * math.pi * t / l)[:, None] * f[None, :]
    z = jnp.concatenate([t_norm[:, None], jnp.cos(ang), -jnp.sin(ang)], -1)
    zz = jnp.pad(z, ((0, 0), (0, HY_EMB_PAD - HY_EMB))).T
    tn_rows = jnp.zeros((8, 2 * l), F32).at[0].set(t_norm).at[1].set(valid)
    deltas = jnp.linspace(math.log(HY_TARGET) / HY_SLOW, math.log(HY_TARGET) / HY_FAST, HY_WIDTH, dtype=F32)
    rate = jnp.broadcast_to(jnp.abs(deltas)[:, None], (HY_WIDTH, LANES))
    return zz, tn_rows, rate


def _outproj_kernel(of_ref, ob_ref, g_ref, yh_ref, x_ref, gate_ref, sh_ref, sc_ref,
                    wg_ref, wh_ref, ng_ref, l1g_ref, l1b_ref, wr_ref, rb_ref, ltri_ref,
                    xm_ref, h2_ref, se_ref, sr_ref, sw_ref, cnt_ref, cs_ref, carry_ref):
    @pl.when((pl.program_id(0) == 0) & (pl.program_id(1) == 0))
    def _():
        carry_ref[...] = jnp.zeros_like(carry_ref)

    cs_ref[0] = jnp.broadcast_to(carry_ref[0:1, :], cs_ref.shape[1:]).astype(jnp.int32)

    tt = x_ref.shape[1]
    sub = ltri_ref.shape[0]
    for s in range(tt // sub):
        rows = slice(s * sub, (s + 1) * sub)
        o = of_ref[0, rows, :].astype(F32) + ob_ref[0, rows, :].astype(F32)
        g = g_ref[0, rows, :].astype(F32)
        parts = []
        for h in range(GLA_HEADS):
            oh = o[:, h * GLA_DV:(h + 1) * GLA_DV]
            parts.append(oh * lax.rsqrt(jnp.mean(oh * oh, axis=-1, keepdims=True) + LN_EPS))
        y_gla = jnp.concatenate(parts, axis=1) * ng_ref[...] * (g * jax.nn.sigmoid(g))
        f = _dot(y_gla.astype(BF16), wg_ref[...]) + _dot_tn(yh_ref[0, :, rows], wh_ref[...])
        x_mid = (_layer_norm(DEEPNORM_ALPHA * x_ref[0, rows, :] + gate_ref[0] * f) * l1g_ref[...]
                 + l1b_ref[...])
        xm_ref[0, rows, :] = x_mid
        h2 = _layer_norm(x_mid) * (1.0 + sc_ref[0]) + sh_ref[0]
        for j in range(ROW_TILES):
            h2_ref[0, pl.ds(s * sub * ROW_TILES + j, sub, stride=ROW_TILES), :] = h2[:, j * LANES:(j + 1) * LANES]

        logits = _dot(h2.astype(BF16), wr_ref[...]) + rb_ref[...]
        lane = lax.broadcasted_iota(jnp.int32, logits.shape, 1).astype(F32)
        hits, idxs, exps = [], [], []
        m0 = None
        for _ in range(TOP_K):
            m = jnp.max(logits, axis=-1, keepdims=True)
            idx = jnp.min(jnp.where(logits == m, lane, float(LANES)), axis=-1, keepdims=True)
            hit = lane == idx
            m0 = m if m0 is None else m0
            hits.append(hit)
            idxs.append(idx)
            exps.append(jnp.exp(m - m0))
            logits = jnp.where(hit, NEG_BIG, logits)
        denom = exps[0]
        sel = jnp.where(hits[0], 1.0, 0.0)
        for kk in range(1, TOP_K):
            denom = denom + exps[kk]
            sel = sel + jnp.where(hits[kk], 1.0, 0.0)
        rank_all = _dot(ltri_ref[...], sel.astype(BF16)) + carry_ref[0:1, :]
        carry_ref[0:1, :] = carry_ref[0:1, :] + jnp.sum(sel, axis=0, keepdims=True)
        se = jnp.zeros(logits.shape, F32)
        sr = jnp.zeros(logits.shape, F32)
        sw = jnp.zeros(logits.shape, F32)
        for kk in range(TOP_K):
            rk = jnp.sum(jnp.where(hits[kk], rank_all, 0.0), axis=-1, keepdims=True)
            col = lane == float(kk)
            se = jnp.where(col, idxs[kk], se)
            sr = jnp.where(col, rk, sr)
            sw = jnp.where(col, exps[kk] / denom, sw)
        se_ref[0, rows, :] = se.astype(jnp.int32)
        sr_ref[0, rows, :] = sr.astype(jnp.int32)
        sw_ref[0, rows, :] = sw
    cnt_ref[...] = jnp.broadcast_to(carry_ref[0:1, :], cnt_ref.shape).astype(jnp.int32)


def _output_projection(o_f, o_b, g, y_hy_t, x, gate1, shift2, scale2, w_gla, w_hy, norm_g, ln1_g, ln1_b, wr, rb):
    b, l, d = x.shape
    tt = TOK_TILE
    tok = lambda w: pl.BlockSpec((1, tt, w), lambda i, j: (i, j, 0))
    row = lambda: pl.BlockSpec((1, 1, d), lambda i, j: (i, 0, 0))
    full = lambda a: pl.BlockSpec(a.shape, lambda i, j: (0,) * a.ndim)
    sub = tt
    ltri = jnp.asarray(np.tril(np.ones((sub, sub), np.float32), -1), BF16)
    consts = [w_gla, w_hy, norm_g, ln1_g, ln1_b, wr, rb, ltri]
    lane_i = jax.ShapeDtypeStruct((b, l, LANES), jnp.int32)
    return pl.pallas_call(
        _outproj_kernel,
        out_shape=[jax.ShapeDtypeStruct((b, l, d), F32), jax.ShapeDtypeStruct((b, l * ROW_TILES, LANES), F32),
                   lane_i, lane_i, jax.ShapeDtypeStruct((b, l, LANES), F32),
                   jax.ShapeDtypeStruct((8, LANES), jnp.int32),
                   jax.ShapeDtypeStruct((b * (l // tt), 8, LANES), jnp.int32)],
        grid=(b, l // tt),
        in_specs=[
            tok(GLA_VAL), tok(GLA_VAL), tok(GLA_VAL),
            pl.BlockSpec((1, HY_WIDTH, tt), lambda i, j: (i, 0, j)),
            tok(d), row(), row(), row(),
        ] + [full(a) for a in consts],
        out_specs=[tok(d), pl.BlockSpec((1, tt * ROW_TILES, LANES), lambda i, j: (i, j, 0)),
                   tok(LANES), tok(LANES), tok(LANES),
                   pl.BlockSpec((8, LANES), lambda i, j: (0, 0)),
                   pl.BlockSpec((1, 8, LANES), lambda i, j: (i * (l // tt) + j, 0, 0))],
        scratch_shapes=[pltpu.VMEM((8, LANES), F32)],
        compiler_params=_cparams(("arbitrary", "arbitrary")),
        name="out_proj_router",
    )(o_f, o_b, g, y_hy_t, x, gate1, shift2, scale2, *consts)


def _expert_weight_layout(w1_ref, w2_ref, p_ref, w1o_ref, w2o_ref):
    p = p_ref[...]
    for j in range(w1_ref.shape[2] // SWIGLU_BLOCK):
        cols = slice(j * SWIGLU_BLOCK, (j + 1) * SWIGLU_BLOCK)
        w1o_ref[0, :, cols] = _dot(w1_ref[0, :, cols].astype(BF16), p).astype(BF16)
    w2o_ref[0] = w2_ref[0].astype(BF16)


def _dispatch_kernel(base_ref, cnt_ref, nt_ref, ev_ref, rv_ref, h2_ref, w1_ref, w2_ref, p_ref,
                     xs_hbm, w1o_ref, w2o_ref, zrow_ref, zblk_ref, slotv_ref, slots_ref, sem, zsem, ssem):
    tt = h2_ref.shape[0] // ROW_TILES
    ev = ev_ref[...]
    slots = rv_ref[...]
    for e in range(N_EXPERTS):
        slots = slots + jnp.where(ev == e, base_ref[e], 0)
    slotv_ref[...] = slots
    to_smem = pltpu.make_async_copy(slotv_ref, slots_ref, ssem)
    to_smem.start()
    to_smem.wait()

    per_line = LANES // TOP_K

    def line(li, carry):
        for u in range(per_line):
            for kk in range(TOP_K):
                pltpu.make_async_copy(h2_ref.at[_row_tile(li * per_line + u)],
                                      xs_hbm.at[_row_tile(slots_ref[li, u * TOP_K + kk])],
                                      sem).start(priority=kk % 2)
        return carry

    lax.fori_loop(0, tt // per_line, line, 0)
    _expert_weight_layout(w1_ref, w2_ref, p_ref, w1o_ref, w2o_ref)
    for _ in range(TOP_K):
        pltpu.make_async_copy(h2_ref, xs_hbm.at[pl.ds(0, tt * ROW_TILES)], sem).wait()

    @pl.when(pl.program_id(0) == pl.num_programs(0) - 1)
    def _():
        zrow_ref[...] = jnp.zeros_like(zrow_ref)

        def per_expert(e, carry):
            n = cnt_ref[e]
            end = ((n + (MOE_TILE - 1)) // MOE_TILE) * MOE_TILE

            def fill(r, c):
                pltpu.make_async_copy(zrow_ref, xs_hbm.at[_row_tile(base_ref[e] + r)], zsem).start()
                return c

            def drain(r, c):
                pltpu.make_async_copy(zrow_ref, xs_hbm.at[_row_tile(0)], zsem).wait()
                return c

            lax.fori_loop(n, end, fill, 0)
            lax.fori_loop(n, end, drain, 0)
            return carry

        lax.fori_loop(0, N_EXPERTS, per_expert, 0)

        zblk_ref[...] = jnp.zeros_like(zblk_ref)
        tile_rows = MOE_TILE * ROW_TILES
        n_all = xs_hbm.shape[0] // tile_rows

        def tile_copy(ti):
            row0 = pl.multiple_of(ti * tile_rows, tile_rows)
            return pltpu.make_async_copy(zblk_ref, xs_hbm.at[pl.ds(row0, tile_rows)], zsem)

        def fill_tile(ti, c):
            tile_copy(ti).start()
            return c

        def drain_tile(ti, c):
            tile_copy(ti).wait()
            return c

        lax.fori_loop(nt_ref[0], n_all, fill_tile, 0)
        lax.fori_loop(nt_ref[0], n_all, drain_tile, 0)


def _dispatch(h2, e_lines, r_lines, base, counts, n_tiles, n_slots, w1, w2):
    t = h2.shape[0] // ROW_TILES
    ne, d, f2 = w1.shape
    assert t % (ne * 2 * LANES // TOP_K) == 0, "one dispatch step per expert, whole index lines per step"
    tt = t // ne
    lines = tt * TOP_K // LANES
    line_blk = lambda: pl.BlockSpec((lines, LANES), lambda i, *_: (i, 0))
    src = np.concatenate([np.arange(0, SWIGLU_BLOCK, 2), np.arange(1, SWIGLU_BLOCK, 2)])
    perm = np.zeros((SWIGLU_BLOCK, SWIGLU_BLOCK), np.float32)
    perm[src, np.arange(SWIGLU_BLOCK)] = 1.0
    exp_blk = lambda shape: pl.BlockSpec((1,) + shape, lambda i, *_: (i, 0, 0))
    return pl.pallas_call(
        _dispatch_kernel,
        out_shape=[jax.ShapeDtypeStruct((n_slots * ROW_TILES, LANES), F32),
                   jax.ShapeDtypeStruct(w1.shape, BF16), jax.ShapeDtypeStruct(w2.shape, BF16)],
        grid_spec=pltpu.PrefetchScalarGridSpec(
            num_scalar_prefetch=3,
            grid=(ne,),
            in_specs=[line_blk(), line_blk(),
                      pl.BlockSpec((tt * ROW_TILES, LANES), lambda i, *_: (i, 0)),
                      exp_blk((d, f2)), exp_blk(w2.shape[1:]),
                      pl.BlockSpec((SWIGLU_BLOCK, SWIGLU_BLOCK), lambda i, *_: (0, 0))],
            out_specs=[pl.BlockSpec(memory_space=pl.ANY), exp_blk((d, f2)), exp_blk(w2.shape[1:])],
            scratch_shapes=[pltpu.VMEM((ROW_TILES, LANES), F32), pltpu.VMEM((MOE_TILE * ROW_TILES, LANES), F32),
                            pltpu.VMEM((lines, LANES), jnp.int32), pltpu.SMEM((lines, LANES), jnp.int32),
                            pltpu.SemaphoreType.DMA, pltpu.SemaphoreType.DMA, pltpu.SemaphoreType.DMA],
        ),
        compiler_params=_cparams(("arbitrary",)),
        name="moe_dispatch",
    )(base, counts, n_tiles, e_lines, r_lines, h2, w1, w2, jnp.asarray(perm, BF16))


def _ffn_kernel(te_ref, nt_ref, xs_ref, *refs):
    ys_ref = refs[-1]
    tm = MOE_TILE
    rows = tm * ROW_TILES
    n_here = jnp.clip(nt_ref[0] - pl.program_id(0) * FFN_TILES_PER_STEP, 0, FFN_TILES_PER_STEP)

    def tile(t):
        w1_ref, b1_ref, w2_ref, b2_ref = refs[4 * t:4 * t + 4]
        x = jnp.concatenate([xs_ref[pl.ds(t * rows + j, tm, stride=ROW_TILES), :] for j in range(ROW_TILES)], axis=1)
        hid = _dot(x.astype(BF16), w1_ref[0]) + b1_ref[0]
        acts = []
        for j in range(hid.shape[1] // SWIGLU_BLOCK):
            glu = jnp.minimum(hid[:, j * SWIGLU_BLOCK:j * SWIGLU_BLOCK + LANES], SWIGLU_LIMIT)
            lin = jnp.clip(hid[:, j * SWIGLU_BLOCK + LANES:(j + 1) * SWIGLU_BLOCK], -SWIGLU_LIMIT, SWIGLU_LIMIT)
            acts.append((glu * jax.nn.sigmoid(SWIGLU_ALPHA * glu) * (lin + 1.0)).astype(BF16))
        y = _dot(jnp.concatenate(acts, axis=1), w2_ref[0]) + b2_ref[0]
        for j in range(ROW_TILES):
            ys_ref[pl.ds(t * rows + j, tm, stride=ROW_TILES), :] = y[:, j * LANES:(j + 1) * LANES]

    for n_valid in range(FFN_TILES_PER_STEP + 1):
        @pl.when(n_here == n_valid)
        def _(n_valid=n_valid):
            for t in range(n_valid):
                tile(t)
            for t in range(n_valid, FFN_TILES_PER_STEP):
                ys_ref[t * rows:(t + 1) * rows, :] = jnp.zeros((rows, LANES), F32)


def _expert_ffn(xs, tile_expert, n_tiles, w1p, b1p, w2b, b2):
    n_slots = xs.shape[0] // ROW_TILES
    d = w1p.shape[1]
    tps = FFN_TILES_PER_STEP
    tm = MOE_TILE
    f2 = w1p.shape[2]
    assert n_slots % (tm * tps) == 0
    rows_blk = (tps * tm * ROW_TILES, LANES)
    weights = []
    for t in range(tps):
        exp_blk = lambda i, te, nt, t=t: (te[i * tps + t], 0, 0)
        weights += [pl.BlockSpec((1, d, f2), exp_blk), pl.BlockSpec((1, 1, f2), exp_blk),
                    pl.BlockSpec((1, f2 // 2, d), exp_blk), pl.BlockSpec((1, 1, d), exp_blk)]
    return pl.pallas_call(
        _ffn_kernel,
        out_shape=jax.ShapeDtypeStruct(xs.shape, F32),
        grid_spec=pltpu.PrefetchScalarGridSpec(
            num_scalar_prefetch=2,
            grid=(n_slots // (tm * tps),),
            in_specs=[
                pl.BlockSpec(rows_blk, lambda i, te, nt: (jnp.minimum(i, (nt[0] - 1) // tps), 0)),
            ] + weights,
            out_specs=pl.BlockSpec(rows_blk, lambda i, te, nt: (i, 0)),
        ),
        compiler_params=_cparams(("arbitrary",)),
        name="moe_expert_ffn",
    )(tile_expert, n_tiles, xs, *([w1p, b1p, w2b, b2] * tps))


def _combine_kernel(cs_ref, base_ref, w_ref, ev_ref, rv_ref, xm_ref, gate_ref, l2g_ref, l2b_ref,
                    ys_hbm, o_ref, buf_ref, acc_ref, rowv_ref, rows_ref, seg_ref, nch_ref, sem, rsem):
    i = pl.program_id(0)
    tt = xm_ref.shape[0]
    cur = lax.rem(i, 2)
    chunk_rows = COMBINE_CHUNK * ROW_TILES

    def chunk_copy(src_row, dst_row, sl):
        return pltpu.make_async_copy(ys_hbm.at[pl.ds(pl.multiple_of(src_row * ROW_TILES, ROW_TILES), chunk_rows)],
                                     buf_ref.at[sl, pl.ds(pl.multiple_of(dst_row * ROW_TILES, ROW_TILES), chunk_rows)],
                                     sem.at[sl])

    def issue(tile, sl):
        def per_expert(e, off):
            start = cs_ref[tile, e]
            n_chunks = (cs_ref[tile + 1, e] - start + (COMBINE_CHUNK - 1)) // COMBINE_CHUNK
            seg_ref[sl, e] = off - start

            def one(c, carry):
                chunk_copy(base_ref[e] + start + c * COMBINE_CHUNK, off + c * COMBINE_CHUNK, sl).start()
                return carry

            lax.fori_loop(0, n_chunks, one, 0)
            return off + n_chunks * COMBINE_CHUNK

        total = lax.fori_loop(0, N_EXPERTS, per_expert, 0)
        nch_ref[sl] = total // COMBINE_CHUNK

    @pl.when(i == 0)
    def _():
        issue(0, 0)

    @pl.when(i + 1 < pl.num_programs(0))
    def _():
        issue(i + 1, 1 - cur)

    def drain(c, carry):
        chunk_copy(0, 0, cur).wait()
        return carry

    ev = ev_ref[...]
    rows = rv_ref[...]
    for e in range(N_EXPERTS):
        rows = rows + jnp.where(ev == e, seg_ref[cur, e], 0)
    rowv_ref[...] = rows * ROW_TILES
    to_smem = pltpu.make_async_copy(rowv_ref, rows_ref, rsem)
    to_smem.start()
    lax.fori_loop(0, nch_ref[cur], drain, 0)
    to_smem.wait()

    per_line = LANES // TOP_K

    def line(li, carry):
        for u in range(per_line):
            acc = None
            for kk in range(TOP_K):
                a = u * TOP_K + kk
                first = pl.multiple_of(rows_ref[li, a], ROW_TILES)
                term = w_ref[li * LANES + a] * buf_ref[cur, pl.ds(first, ROW_TILES), :]
                acc = term if acc is None else acc + term
            acc_ref[_row_tile(li * per_line + u), :] = acc
        return carry

    lax.fori_loop(0, tt // per_line, line, 0)
    mixed = jnp.concatenate([acc_ref[_row_slab(tt, j), :] for j in range(ROW_TILES)], axis=1)
    pre = DEEPNORM_ALPHA * xm_ref[...] + gate_ref[0] * mixed
    o_ref[...] = _layer_norm(pre) * l2g_ref[...] + l2b_ref[...]


def _combine(ys, tile_starts, base, e_lines, r_lines, w_flat, x_mid, gate2, ln2_g, ln2_b, tiles_per_batch):
    t, d = x_mid.shape
    tt = TOK_TILE
    n = t // tt
    cap = tt * TOP_K + N_EXPERTS * COMBINE_CHUNK
    lines = tt * TOP_K // LANES
    line_blk = lambda: pl.BlockSpec((lines, LANES), lambda i, *_: (i, 0))
    return pl.pallas_call(
        _combine_kernel,
        out_shape=jax.ShapeDtypeStruct((t, d), F32),
        grid_spec=pltpu.PrefetchScalarGridSpec(
            num_scalar_prefetch=2,
            grid=(n,),
            in_specs=[
                pl.BlockSpec((tt * TOP_K,), lambda i, *_: (i,), memory_space=pltpu.SMEM),
                line_blk(), line_blk(),
                pl.BlockSpec((tt, d), lambda i, *_: (i, 0)),
                pl.BlockSpec((1, 1, d), lambda i, *_: (i // tiles_per_batch, 0, 0)),
                pl.BlockSpec((1, d), lambda i, *_: (0, 0)),
                pl.BlockSpec((1, d), lambda i, *_: (0, 0)),
                pl.BlockSpec(memory_space=pl.ANY),
            ],
            out_specs=pl.BlockSpec((tt, d), lambda i, *_: (i, 0)),
            scratch_shapes=[pltpu.VMEM((2, cap * ROW_TILES, LANES), F32), pltpu.VMEM((tt * ROW_TILES, LANES), F32),
                            pltpu.VMEM((lines, LANES), jnp.int32), pltpu.SMEM((lines, LANES), jnp.int32),
                            pltpu.SMEM((2, N_EXPERTS), jnp.int32), pltpu.SMEM((2,), jnp.int32),
                            pltpu.SemaphoreType.DMA((2,)), pltpu.SemaphoreType.DMA],
        ),
        compiler_params=_cparams(("arbitrary",)),
        name="moe_combine",
    )(tile_starts, base, w_flat, e_lines, r_lines, x_mid, gate2, ln2_g, ln2_b, ys)


def kernel(x, c, ctx, c_ctx, ada_w, ada_b, w_in, gla_wa_f, gla_ba_f, gla_wa_b, gla_ba_b, gla_norm_g,
           hy_conv_w, hy_conv_b, hy_flt_w1, hy_flt_b1, hy_flt_w2, hy_flt_b2, hy_flt_wout, hy_flt_freq,
           hy_bias_d, w_out, ln1_g, ln1_b, router_w, router_b, exp_w1, exp_b1, exp_w2, exp_b2, ln2_g, ln2_b):
    batch, seq_len, d = x.shape
    lyr = 0
    ch = HY_WIDTH

    n_rows = 8 * ((batch + 1 + 7) // 8)
    cvec = jnp.zeros((n_rows, d), F32).at[:batch].set(c).at[batch].set(c_ctx)
    mod = _modulation(cvec, ada_w[lyr], ada_b[lyr][None, :])
    part = lambda rows, i: rows[:, None, i * d:(i + 1) * d]
    mod_x = mod[:batch]
    mod_c = jnp.broadcast_to(mod[batch:batch + 1], (batch, 6 * d))

    w = w_in[lyr]
    o_q, o_k, o_v, o_g = 0, GLA_KEY, 2 * GLA_KEY, 2 * GLA_KEY + GLA_VAL
    o_a = o_g + GLA_VAL
    o_h = o_a + 2 * GLA_RANK
    a_cols = jnp.pad(w[:, o_a:o_h], ((0, 0), (0, A_PAD - 2 * GLA_RANK)))
    q_cols = w[:, o_q:o_k] * (GLA_DK ** -0.5)
    w_main = jnp.concatenate([q_cols, w[:, o_k:o_a], a_cols], axis=1).astype(BF16)
    w_ctx = jnp.concatenate([q_cols, w[:, o_k:o_g], a_cols], axis=1).astype(BF16)
    w_hy_t = w[:, o_h:].T.astype(BF16)
    conv_w = jnp.broadcast_to(hy_conv_w[lyr][:, :, None], (HY_CONV, 3 * ch, LANES))
    conv_b = jnp.broadcast_to(hy_conv_b[lyr][:, None], (3 * ch, LANES))

    ctx_qkva = _input_projection(ctx, part(mod_c, 0), part(mod_c, 1), w_ctx)
    q, k, v, g, a_low, u_t = _input_projection(x, part(mod_x, 0), part(mod_x, 1), w_main, w_hy_t, conv_w, conv_b)

    wa = jnp.zeros((2, A_PAD, GLA_KEY), F32)
    wa = wa.at[0, :GLA_RANK].set(gla_wa_f[lyr]).at[1, GLA_RANK:2 * GLA_RANK].set(gla_wa_b[lyr]).astype(BF16)
    ba = jnp.stack([gla_ba_f[lyr], gla_ba_b[lyr]])[:, None, :]
    o_f, o_b = _gla_scan(ctx_qkva, (q, k, v, a_low), wa, ba)

    r1 = 2 * seq_len // LANES
    consts = _dft_constants(r1)
    zz, tn_rows, rate = _filter_inputs(seq_len)
    w1_t = jnp.pad(hy_flt_w1[lyr], ((0, HY_EMB_PAD - HY_EMB), (0, 0))).T
    unit_tile = lambda vec: jnp.broadcast_to(vec[:, None], (HY_FH, LANES))
    wo_t = hy_flt_wout[lyr].reshape(HY_FH, HY_ORDER, 2, ch).transpose(1, 2, 3, 0)
    filt = _filter_mlp(zz, tn_rows, rate, w1_t, unit_tile(hy_flt_b1[lyr]), hy_flt_w2[lyr].T,
                       unit_tile(hy_flt_b2[lyr]), unit_tile(hy_flt_freq[lyr]), wo_t)
    spectra = _filter_spectra(filt.reshape(HY_ORDER, ch, r1, LANES), consts, r1)
    y_hy = _hyena_conv(u_t.reshape(batch // 2, 2, 3 * ch, r1 // 2, LANES), spectra, hy_bias_d[lyr], consts, r1)
    y_hy_t = y_hy.reshape(batch, ch, seq_len)

    wo = w_out[lyr].astype(BF16)
    wr = jnp.pad(router_w[lyr], ((0, 0), (0, LANES - N_EXPERTS))).astype(BF16)
    rb = jnp.pad(router_b[lyr], (0, LANES - N_EXPERTS), constant_values=NEG_BIG)[None, :]
    norm_g = jnp.tile(gla_norm_g[lyr], GLA_HEADS)[None, :]
    x_mid, h2, sel_e, sel_r, sel_w, cnt, tile_cs = _output_projection(
        o_f, o_b, g, y_hy_t, x, part(mod_x, 2), part(mod_x, 3), part(mod_x, 4),
        wo[:GLA_VAL], wo[GLA_VAL:], norm_g, ln1_g[lyr][None], ln1_b[lyr][None], wr, rb)

    t = batch * seq_len
    counts = cnt[0, :N_EXPERTS]
    tiles_e = (counts + (MOE_TILE - 1)) // MOE_TILE
    tile_end = jnp.cumsum(tiles_e)
    base = ((tile_end - tiles_e) * MOE_TILE).astype(jnp.int32)
    n_tiles = tile_end[-1:].astype(jnp.int32)
    max_tiles = t * TOP_K // MOE_TILE + N_EXPERTS + FFN_TILES_PER_STEP
    tile_ids = jnp.minimum(jnp.arange(max_tiles, dtype=jnp.int32), n_tiles[0] - 1)
    tile_expert = jnp.sum(tile_ids[:, None] >= tile_end[None, :], axis=1).astype(jnp.int32)
    as_lines = lambda sel: sel.reshape(t, LANES)[:, :TOP_K].reshape(t * TOP_K // LANES, LANES)
    e_lines, r_lines = as_lines(sel_e), as_lines(sel_r)
    w_flat = as_lines(sel_w).reshape(t * TOP_K)
    tile_starts = jnp.concatenate([tile_cs[:, 0, :], cnt[0:1]], axis=0)

    n_blk = 2 * D_EXPERT // SWIGLU_BLOCK
    b1p_e = exp_b1[lyr].reshape(N_EXPERTS, n_blk, LANES, 2).transpose(0, 1, 3, 2).reshape(N_EXPERTS, 1, 2 * D_EXPERT)
    xs, w1p_e, w2_e = _dispatch(h2.reshape(t * ROW_TILES, LANES), e_lines, r_lines, base, counts, n_tiles,
                                max_tiles * MOE_TILE, exp_w1[lyr], exp_w2[lyr])
    ys = _expert_ffn(xs, tile_expert, n_tiles, w1p_e, b1p_e, w2_e, exp_b2[lyr][:, None, :])
    out = _combine(ys, tile_starts, base, e_lines, r_lines, w_flat, x_mid.reshape(t, d), part(mod_x, 5),
                   ln2_g[lyr][None], ln2_b[lyr][None], seq_len // TOK_TILE)
    return out.reshape(batch, seq_len, d)
```
